```python
import jax, jax.numpy as jnp
from jax import lax
import numpy as np

D_MODEL = 1024
BATCH = 8
SEQ = 2048
DEPTH = 2
DEC_BATCH = 32
DEC_SEQ = 8
PAST_LEN = 8192
PAGE_SIZE = 128

H_A = 8
HD_A = 64
W_A = H_A * HD_A
G_B = 4
W_B = 512
CH_B = W_B // G_B
CHUNK_B = 128
H_C = 4
K_C = 128
V_C = 128
W_C = H_C * V_C
HGRN_CHUNK = 64
D_FF = 2816
CONV_W = 3
Q_BLOCK = 128
EPS = 1e-6
PROJ_SIZES = (W_A, W_A, W_A, H_A, W_B, W_B, H_C * K_C, H_C * K_C, W_C, W_C, D_MODEL, D_MODEL, D_MODEL)
D_IN = sum(PROJ_SIZES)

kernel_name = 'hybrid_fox_gmlp_hgrn2_convffn_step'


def rmsnorm(x, g):
    xf = x.astype(jnp.float32)
    y = xf * lax.rsqrt(jnp.mean(xf * xf, axis=-1, keepdims=True) + EPS)
    return (y * g.astype(jnp.float32)).astype(x.dtype)


def fox_prompt(q, k, v, logf):
    B, T, H, Dh = q.shape
    nb = T // Q_BLOCK
    c = jnp.cumsum(logf, axis=1).transpose(0, 2, 1)
    qb = q.reshape(B, nb, Q_BLOCK, H, Dh).transpose(1, 0, 2, 3, 4)
    cb = c.reshape(B, H, nb, Q_BLOCK).transpose(2, 0, 1, 3)
    kpos = jnp.arange(T)
    scale = Dh ** -0.5

    def one_block(args):
        i, qi, ci = args
        s = jnp.einsum('bqhd,bkhd->bhqk', qi, k).astype(jnp.float32) * scale
        s = s + ci[..., :, None] - c[..., None, :]
        qpos = i * Q_BLOCK + jnp.arange(Q_BLOCK)
        s = jnp.where(kpos[None, :] <= qpos[:, None], s, -jnp.inf)
        p = jax.nn.softmax(s, axis=-1).astype(v.dtype)
        return jnp.einsum('bhqk,bkhd->bqhd', p, v)

    o = lax.map(one_block, (jnp.arange(nb), qb, cb))
    return o.transpose(1, 0, 2, 3, 4).reshape(B, T, H * Dh)


def fox_sample(q, k, v, logf, k_past, v_past, logf_past):
    B, T, H, Dh = q.shape
    P = k_past.shape[1]
    c_past = jnp.cumsum(logf_past.astype(jnp.float32), axis=1)
    c_new = c_past[:, -1:] + jnp.cumsum(logf, axis=1)
    c_all = jnp.concatenate([c_past, c_new], axis=1).transpose(0, 2, 1)
    c_q = c_new.transpose(0, 2, 1)
    k_all = jnp.concatenate([k_past.astype(k.dtype), k], axis=1)
    v_all = jnp.concatenate([v_past.astype(v.dtype), v], axis=1)
    s = jnp.einsum('bqhd,bkhd->bhqk', q, k_all).astype(jnp.float32) * (Dh ** -0.5)
    s = s + c_q[..., :, None] - c_all[..., None, :]
    mask = jnp.arange(P + T)[None, :] <= (P + jnp.arange(T))[:, None]
    s = jnp.where(mask, s, -jnp.inf)
    p = jax.nn.softmax(s, axis=-1).astype(v.dtype)
    return jnp.einsum('bhqk,bkhd->bqhd', p, v_all).reshape(B, T, H * Dh)


def gmlp_chunked(u, v, w_s, b_s):
    B, T, _ = v.shape
    Tp = -(-T // CHUNK_B) * CHUNK_B
    vp = jnp.pad(v, ((0, 0), (0, Tp - T), (0, 0))).reshape(B, Tp // CHUNK_B, CHUNK_B, G_B, CH_B)
    w = jnp.where(jnp.tril(jnp.ones((CHUNK_B, CHUNK_B), bool)), w_s, 0)
    s = jnp.einsum('gts,bnsgc->bntgc', w, vp) + b_s.T[None, None, :, :, None]
    return u * s.reshape(B, Tp, W_B)[:, :T]


def hgrn2_chunked(q, logf, kf, v, s0):
    B, T, H, K = q.shape
    C = HGRN_CHUNK if T % HGRN_CHUNK == 0 else T
    n = T // C

    def to_chunks(a):
        return a.reshape(B, n, C, H, a.shape[-1]).transpose(1, 0, 3, 2, 4)

    tri = jnp.tril(jnp.ones((C, C), bool))

    def step(S, xs):
        qc, gc, kc, vc = xs
        b = jnp.cumsum(gc, axis=2)
        o_inter = jnp.einsum('bhtk,bhkv->bhtv', qc * jnp.exp(b), S)
        diff = b[:, :, :, None, :] - b[:, :, None, :, :]
        decay = jnp.exp(jnp.where(tri[:, :, None], diff, -jnp.inf))
        a = jnp.einsum('bhtk,bhsk,bhtsk->bhts', qc, kc, decay)
        o = o_inter + jnp.einsum('bhts,bhsv->bhtv', a, vc)
        bl = b[:, :, -1:, :]
        S_new = jnp.exp(b[:, :, -1, :])[..., None] * S + jnp.einsum('bhsk,bhsv->bhkv', kc * jnp.exp(bl - b), vc)
        return S_new, o

    S, o = lax.scan(step, s0, (to_chunks(q), to_chunks(logf), to_chunks(kf), to_chunks(v)))
    return o.transpose(1, 0, 3, 2, 4).reshape(B, T, H, v.shape[-1]), S


def conv_ffn(h, buf, w_up, w_conv, b_conv, w_down):
    T = h.shape[1]
    up = h @ w_up
    ext = jnp.concatenate([buf.astype(up.dtype), up], axis=1)
    c = b_conv + sum(ext[:, j:j + T] * w_conv[j] for j in range(CONV_W))
    gate, val = jnp.split(c, 2, axis=-1)
    return (jax.nn.silu(gate) * val) @ w_down, ext[:, T:]


def trunk_layer(x, lp, lb, past):
    (norm1, w_in, b_f, gmlp_norm, w_s, b_s, hgrn_norm, w_oa, w_ob, w_oc, w_o,
     norm2, w_up, w_conv, b_conv, w_down) = lp
    B, T, _ = x.shape
    h = rmsnorm(x, norm1)
    parts = jnp.split(h @ w_in, np.cumsum(PROJ_SIZES)[:-1].tolist(), axis=-1)
    qa, ka, va, fa, ub, vb, qc, fc, ic, gc, gate_a, gate_b, gate_c = parts
    qa = qa.reshape(B, T, H_A, HD_A)
    ka = ka.reshape(B, T, H_A, HD_A)
    va = va.reshape(B, T, H_A, HD_A)
    logf_a = jax.nn.log_sigmoid(fa.astype(jnp.float32) + b_f.astype(jnp.float32))
    if past is None:
        ya = fox_prompt(qa, ka, va, logf_a)
        s0 = jnp.zeros((B, H_C, K_C, V_C), jnp.float32)
        buf = jnp.zeros((B, CONV_W - 1, 2 * D_FF), x.dtype)
    else:
        k_past, v_past, logf_past, s0, buf = past
        ya = fox_sample(qa, ka, va, logf_a, k_past, v_past, logf_past)
    ub = jax.nn.gelu(ub)
    vb = rmsnorm(jax.nn.gelu(vb), gmlp_norm)
    yb = gmlp_chunked(ub, vb, w_s, b_s)
    f_c = lb + (1.0 - lb) * jax.nn.sigmoid(fc.astype(jnp.float32))
    yc, s_new = hgrn2_chunked(
        jax.nn.silu(qc.astype(jnp.float32)).reshape(B, T, H_C, K_C),
        jnp.log(f_c).reshape(B, T, H_C, K_C),
        (1.0 - f_c).reshape(B, T, H_C, K_C),
        ic.astype(jnp.float32).reshape(B, T, H_C, V_C),
        s0.astype(jnp.float32))
    yc = rmsnorm(yc.astype(x.dtype), hgrn_norm.reshape(H_C, V_C)).reshape(B, T, W_C) * jax.nn.silu(gc)
    merged = (jax.nn.sigmoid(gate_a) * (ya @ w_oa)
              + jax.nn.sigmoid(gate_b) * (yb @ w_ob)
              + jax.nn.sigmoid(gate_c) * (yc @ w_oc))
    x = x + merged @ w_o
    y_f, new_buf = conv_ffn(rmsnorm(x, norm2), buf, w_up, w_conv, b_conv, w_down)
    x = x + y_f
    return x, (ka, va, logf_a, vb, s_new.astype(x.dtype), new_buf)


def setup_inputs(seed: int = 0) -> dict:
    key = jax.random.key(seed)
    ks = jax.random.split(key, 32)
    f32 = jnp.float32

    def nrm(k, shape, scale=1.0):
        return scale * jax.random.normal(k, shape, f32)

    n_pages = PAST_LEN // PAGE_SIZE
    n_used = DEC_BATCH * n_pages
    n_phys = n_used + (n_used + 3) // 4
    perm = jax.random.permutation(ks[0], n_phys)
    page_table = perm[:n_used].reshape(DEC_BATCH, n_pages).astype(jnp.int32)
    return {
        'x_prompt': nrm(ks[1], (BATCH, SEQ, D_MODEL)),
        'x_sample': nrm(ks[2], (DEC_BATCH, DEC_SEQ, D_MODEL)),
        'cache_k': nrm(ks[3], (DEPTH, n_phys, PAGE_SIZE, H_A, HD_A)),
        'cache_v': nrm(ks[4], (DEPTH, n_phys, PAGE_SIZE, H_A, HD_A)),
        'cache_logf': jax.nn.log_sigmoid(3.0 + nrm(ks[5], (DEPTH, n_phys, PAGE_SIZE, H_A))),
        'state_hgrn': nrm(ks[6], (DEPTH, DEC_BATCH, H_C, K_C, V_C), 0.5),
        'state_conv': nrm(ks[7], (DEPTH, DEC_BATCH, CONV_W - 1, 2 * D_FF)),
        'page_table': page_table,
        'norm1': 1.0 + nrm(ks[8], (DEPTH, D_MODEL), 0.01),
        'w_in': nrm(ks[9], (DEPTH, D_MODEL, D_IN), D_MODEL ** -0.5),
        'b_f': 3.0 + nrm(ks[10], (DEPTH, H_A), 0.5),
        'gmlp_norm': 1.0 + nrm(ks[11], (DEPTH, W_B), 0.01),
        'w_s': nrm(ks[12], (DEPTH, G_B, CHUNK_B, CHUNK_B), CHUNK_B ** -0.5),
        'b_s': nrm(ks[13], (DEPTH, G_B, CHUNK_B), 0.05),
        'hgrn_norm': 1.0 + nrm(ks[14], (DEPTH, W_C), 0.01),
        'hgrn_lb': nrm(ks[15], (DEPTH, H_C * K_C)),
        'w_oa': nrm(ks[16], (DEPTH, W_A, D_MODEL), W_A ** -0.5),
        'w_ob': nrm(ks[17], (DEPTH, W_B, D_MODEL), W_B ** -0.5),
        'w_oc': nrm(ks[18], (DEPTH, W_C, D_MODEL), W_C ** -0.5),
        'w_o': nrm(ks[19], (DEPTH, D_MODEL, D_MODEL), D_MODEL ** -0.5),
        'norm2': 1.0 + nrm(ks[20], (DEPTH, D_MODEL), 0.01),
        'w_up': nrm(ks[21], (DEPTH, D_MODEL, 2 * D_FF), D_MODEL ** -0.5),
        'w_conv': nrm(ks[22], (DEPTH, CONV_W, 2 * D_FF), CONV_W ** -0.5),
        'b_conv': nrm(ks[23], (DEPTH, 2 * D_FF), 0.02),
        'w_down': nrm(ks[24], (DEPTH, D_FF, D_MODEL), D_FF ** -0.5),
        'norm_f': 1.0 + nrm(ks[25], (D_MODEL,), 0.01),
    }


def reference(x_prompt, x_sample, cache_k, cache_v, cache_logf, state_hgrn, state_conv, page_table,
              norm1, w_in, b_f, gmlp_norm, w_s, b_s, hgrn_norm, hgrn_lb, w_oa, w_ob, w_oc, w_o,
              norm2, w_up, w_conv, b_conv, w_down, norm_f):
    sm = jax.nn.softmax(hgrn_lb.astype(jnp.float32), axis=0)
    lower = jnp.cumsum(sm, axis=0) - sm[:1]
    n_seq, n_pages = page_table.shape
    past_len = n_pages * cache_k.shape[2]
    xp, xs = x_prompt, x_sample
    st_p, st_s = [], []
    for l in range(DEPTH):
        lp = (norm1[l], w_in[l], b_f[l], gmlp_norm[l], w_s[l], b_s[l], hgrn_norm[l],
              w_oa[l], w_ob[l], w_oc[l], w_o[l], norm2[l], w_up[l], w_conv[l], b_conv[l], w_down[l])
        xp, st = trunk_layer(xp, lp, lower[l], None)
        st_p.append(st)
        past = (cache_k[l, page_table].reshape(n_seq, past_len, H_A, HD_A),
                cache_v[l, page_table].reshape(n_seq, past_len, H_A, HD_A),
                cache_logf[l, page_table].reshape(n_seq, past_len, H_A),
                state_hgrn[l], state_conv[l])
        xs, st = trunk_layer(xs, lp, lower[l], past)
        st_s.append(st)
    y_prompt = rmsnorm(xp, norm_f)
    y_sample = rmsnorm(xs, norm_f)

    def stk(sts, i):
        return jnp.stack([st[i] for st in sts], axis=0)

    k_prompt, v_prompt, logf_prompt = stk(st_p, 0), stk(st_p, 1), stk(st_p, 2)
    hgrn_prompt, conv_prompt = stk(st_p, 4), stk(st_p, 5)
    k_sample, v_sample, logf_sample = stk(st_s, 0), stk(st_s, 1), stk(st_s, 2)
    gmlp_v_sample, hgrn_sample, conv_sample = stk(st_s, 3), stk(st_s, 4), stk(st_s, 5)
    return (y_prompt, y_sample, k_prompt, v_prompt, logf_prompt, hgrn_prompt, conv_prompt,
            k_sample, v_sample, logf_sample, gmlp_v_sample, hgrn_sample, conv_sample)
```

```python
import functools

import jax
import jax.numpy as jnp
from jax import lax
from jax.experimental import pallas as pl
from jax.experimental.pallas import tpu as pltpu

F32 = jnp.float32
BF16 = jnp.bfloat16
EPS = 1e-6
NEG_INF = float("-inf")

LANES = 128
SUBLANES = 8
VMEM_LIMIT_BYTES = 56 * 1024 * 1024

ROW_TILE = 256
ATTN_TILE = 256
PAGES_PER_STEP = 8
HGRN_SUB = 16
HGRN_TILE = 256
FFN_COLS = 256


def _dot(a, b):
    return jnp.dot(a, b, preferred_element_type=F32)


def _dot_nt(a, b):
    return lax.dot_general(a, b, (((1,), (1,)), ((), ())), preferred_element_type=F32)


def _dot_tn(a, b):
    return lax.dot_general(a, b, (((0,), (0,)), ((), ())), preferred_element_type=F32)


def _rms(x, g):
    return x * lax.rsqrt(jnp.mean(x * x, axis=-1, keepdims=True) + EPS) * g


def _sigmoid(x):
    return 1.0 / (1.0 + jnp.exp(-x))


def _silu(x):
    return x * _sigmoid(x)


def _gelu_tanh(x):
    c = 0.7978845608028654
    return x * (0.5 * (1.0 + jnp.tanh(c * (x + 0.044715 * (x * x * x)))))


def _log_sigmoid(z):
    return jnp.minimum(z, 0.0) - jnp.log1p(jnp.exp(-jnp.abs(z)))


def _split3(x):
    hi = x.astype(BF16)
    r = x - hi.astype(F32)
    mid = r.astype(BF16)
    lo = (r - mid.astype(F32)).astype(BF16)
    return hi, mid, lo


def _ones_tri(n, upper):
    r = lax.broadcasted_iota(jnp.int32, (n, n), 0)
    c = lax.broadcasted_iota(jnp.int32, (n, n), 1)
    keep = (r <= c) if upper else (r >= c)
    return jnp.where(keep, 1.0, 0.0).astype(BF16)


def _const_spec(shape):
    nd = len(shape)
    return pl.BlockSpec(shape, lambda *_: (0,) * nd, pipeline_mode=pl.Buffered(1))


def _params(*sem):
    return pltpu.CompilerParams(dimension_semantics=sem, vmem_limit_bytes=VMEM_LIMIT_BYTES)


def _inproj_kernel(x_ref, n1_ref, wa_ref, wf_ref, wft_ref, wb_ref, wc_ref, wg_ref,
                   bf_ref, bfc_ref, gn_ref, lb_ref,
                   q_ref, k_ref, v_ref, lf_ref, lft_ref, ub_ref, vb_ref,
                   qs_ref, gl_ref, kf_ref, iv_ref, gc_ref, gt_ref):
    wa_w = k_ref.shape[-1]
    wb_w = ub_ref.shape[-1]
    wc_w = qs_ref.shape[-1]
    h = _rms(x_ref[...], n1_ref[...]).astype(BF16)

    pa = _dot(h, wa_ref[...])
    q_ref[...] = pa[:, :wa_w].astype(q_ref.dtype)
    k_ref[...] = pa[:, wa_w:2 * wa_w]
    v_ref[...] = pa[:, 2 * wa_w:]

    lf_ref[...] = _log_sigmoid(_dot(h, wf_ref[...]) + bf_ref[...])
    lft_ref[...] = _log_sigmoid(_dot_nt(wft_ref[...], h) + bfc_ref[...])

    pb = _dot(h, wb_ref[...])
    ub_ref[...] = _gelu_tanh(pb[:, :wb_w])
    vb_ref[...] = _rms(_gelu_tanh(pb[:, wb_w:]), gn_ref[...])

    pc = _dot(h, wc_ref[...])
    lb = lb_ref[...]
    qs_ref[...] = _silu(pc[:, :wc_w])
    f = lb + (1.0 - lb) * _sigmoid(pc[:, wc_w:2 * wc_w])
    gl_ref[...] = jnp.log(f)
    kf_ref[...] = 1.0 - f
    iv_ref[...] = pc[:, 2 * wc_w:3 * wc_w]
    gc_ref[...] = _silu(pc[:, 3 * wc_w:])

    gt_ref[...] = _sigmoid(_dot(h, wg_ref[...]))


def _inproj(x, n1, wa, wf, wft, wb, wc, wg, bf_row, bf_col, gn, lb):
    n, d = x.shape
    tm = min(ROW_TILE, n)
    wa_w, wb_w, wc_w = wa.shape[1] // 3, wb.shape[1] // 2, wc.shape[1] // 4
    nh = wft.shape[0]

    def rows(w):
        return pl.BlockSpec((tm, w), lambda i: (i, 0))

    out_shape = (
        jax.ShapeDtypeStruct((n, wa_w), BF16),
        jax.ShapeDtypeStruct((n, wa_w), F32),
        jax.ShapeDtypeStruct((n, wa_w), F32),
        jax.ShapeDtypeStruct((n, LANES), F32),
        jax.ShapeDtypeStruct((nh, n), F32),
        jax.ShapeDtypeStruct((n, wb_w), F32),
        jax.ShapeDtypeStruct((n, wb_w), F32),
        jax.ShapeDtypeStruct((n, wc_w), F32),
        jax.ShapeDtypeStruct((n, wc_w), F32),
        jax.ShapeDtypeStruct((n, wc_w), F32),
        jax.ShapeDtypeStruct((n, wc_w), F32),
        jax.ShapeDtypeStruct((n, wc_w), F32),
        jax.ShapeDtypeStruct((n, wg.shape[1]), F32),
    )
    out_specs = (rows(wa_w), rows(wa_w), rows(wa_w), rows(LANES),
                 pl.BlockSpec((nh, tm), lambda i: (0, i)),
                 rows(wb_w), rows(wb_w), rows(wc_w), rows(wc_w), rows(wc_w), rows(wc_w), rows(wc_w),
                 rows(wg.shape[1]))
    in_specs = [rows(d), _const_spec(n1.shape), _const_spec(wa.shape), _const_spec(wf.shape),
                _const_spec(wft.shape), _const_spec(wb.shape), _const_spec(wc.shape), _const_spec(wg.shape),
                _const_spec(bf_row.shape), _const_spec(bf_col.shape), _const_spec(gn.shape),
                _const_spec(lb.shape)]
    return pl.pallas_call(
        _inproj_kernel, grid=(n // tm,), in_specs=in_specs, out_specs=out_specs, out_shape=out_shape,
        compiler_params=_params("parallel"), name="inproj",
    )(x, n1, wa, wf, wft, wb, wc, wg, bf_row, bf_col, gn, lb)


def _fox_prompt_kernel(q_ref, k_ref, v_ref, lf_ref, lft_ref, o_ref, ccol_s, crow_s, *, nh, hd, tq):
    qi = pl.program_id(1)
    t_len = k_ref.shape[1]
    nblk = t_len // tq

    @pl.when(qi == 0)
    def _():
        lower = _ones_tri(tq, upper=False)
        upper = _ones_tri(tq, upper=True)
        carry_r = jnp.zeros((1, LANES), F32)
        carry_c = jnp.zeros((nh, 1), F32)
        for blk in range(nblk):
            hi, mid, lo = _split3(lf_ref[0, blk * tq:(blk + 1) * tq, :])
            cs = _dot(lower, hi) + _dot(lower, mid) + _dot(lower, lo) + carry_r
            ccol_s[blk * tq:(blk + 1) * tq, :] = cs
            carry_r = cs[tq - 1:tq, :]
            hi, mid, lo = _split3(lft_ref[:, blk * tq:(blk + 1) * tq])
            cr = _dot(hi, upper) + _dot(mid, upper) + _dot(lo, upper) + carry_c
            crow_s[blk] = cr
            carry_c = cr[:, tq - 1:tq]

    row = lax.broadcasted_iota(jnp.int32, (tq, tq), 0)
    col = lax.broadcasted_iota(jnp.int32, (tq, tq), 1)
    q0 = pl.multiple_of(qi * tq, tq)

    for h in range(nh):
        hs = slice(h * hd, (h + 1) * hd)
        qh = q_ref[0, :, hs]
        cq = ccol_s[pl.ds(q0, tq), h:h + 1]

        def tile(j, carry, masked, hs=hs, qh=qh, cq=cq, h=h):
            m, l, acc = carry
            k0 = pl.multiple_of(j * tq, tq)
            kj = k_ref[0, pl.ds(k0, tq), hs].astype(BF16)
            vj = v_ref[0, pl.ds(k0, tq), hs].astype(BF16)
            ck = crow_s[j][h:h + 1, :]
            s = _dot_nt(qh, kj) + (cq - ck)
            if masked:
                s = jnp.where(col <= row, s, NEG_INF)
            m_new = jnp.maximum(m, jnp.max(s, axis=-1, keepdims=True))
            alpha = jnp.exp(m - m_new)
            p = jnp.exp(s - m_new)
            l = alpha * l + jnp.sum(p, axis=-1, keepdims=True)
            acc = alpha * acc + _dot(p.astype(BF16), vj)
            return m_new, l, acc

        init = (jnp.full((tq, 1), NEG_INF, F32), jnp.zeros((tq, 1), F32), jnp.zeros((tq, hd), F32))
        carry = lax.fori_loop(0, qi, functools.partial(tile, masked=False), init)
        _, l, acc = tile(qi, carry, True)
        o_ref[0, :, hs] = (acc / l).astype(o_ref.dtype)


def _fox_prompt(q, k, v, lf, lft, nh):
    b, t, w = k.shape
    hd = w // nh
    tq = min(ATTN_TILE, t)
    kern = functools.partial(_fox_prompt_kernel, nh=nh, hd=hd, tq=tq)
    return pl.pallas_call(
        kern, grid=(b, t // tq),
        in_specs=[pl.BlockSpec((1, tq, w), lambda i, j: (i, j, 0)),
                  pl.BlockSpec((1, t, w), lambda i, j: (i, 0, 0)),
                  pl.BlockSpec((1, t, w), lambda i, j: (i, 0, 0)),
                  pl.BlockSpec((1, t, LANES), lambda i, j: (i, 0, 0)),
                  pl.BlockSpec((nh, t), lambda i, j: (0, i))],
        out_specs=pl.BlockSpec((1, tq, w), lambda i, j: (i, j, 0)),
        out_shape=jax.ShapeDtypeStruct((b, t, w), BF16),
        scratch_shapes=[pltpu.VMEM((t, LANES), F32), pltpu.VMEM((t // tq, nh, tq), F32)],
        compiler_params=_params("parallel", "arbitrary"), name="fox_prompt",
    )(q, k, v, lf, lft)


def _fox_sample_kernel(pt_ref, q_ref, kn_ref, vn_ref, lnt_ref, *rest, nh, hd, g_pages):
    k_pages = rest[:g_pages]
    v_pages = rest[g_pages:2 * g_pages]
    f_pages = rest[2 * g_pages:3 * g_pages]
    o_ref, qbd_s, m_s, l_s, acc_s, carry_s, cn_s = rest[3 * g_pages:]
    del pt_ref
    step = pl.program_id(1)
    nt = q_ref.shape[0]
    w = q_ref.shape[1]
    page = k_pages[0].shape[0]
    rows = nt * nh

    def update(s3, vs):
        m_old = m_s[...]
        m_new = jnp.maximum(m_old, jnp.max(s3, axis=-1, keepdims=True))
        alpha = jnp.exp(m_old - m_new)
        p3 = jnp.exp(s3 - m_new)
        l_s[...] = alpha * l_s[...] + jnp.sum(p3, axis=-1, keepdims=True)
        p = p3.reshape(rows, p3.shape[-1]).astype(BF16)
        pv = None
        off = 0
        for vt in vs:
            part = _dot(p[:, off:off + vt.shape[0]], vt)
            pv = part if pv is None else pv + part
            off += vt.shape[0]
        acc_s[...] = alpha * acc_s[...] + pv.reshape(nt, nh, w)
        m_s[...] = m_new

    @pl.when(step == 0)
    def _():
        rep_r = lax.broadcasted_iota(jnp.int32, (rows, nt), 0) // nh
        rep_c = lax.broadcasted_iota(jnp.int32, (rows, nt), 1)
        q_rep = _dot(jnp.where(rep_r == rep_c, 1.0, 0.0), q_ref[...])
        head_of_lane = lax.broadcasted_iota(jnp.int32, (rows, w), 1) // hd
        head_of_row = lax.broadcasted_iota(jnp.int32, (rows, w), 0) % nh
        qbd = jnp.where(head_of_lane == head_of_row, q_rep, 0.0).astype(BF16)
        qbd_s[...] = qbd

        lane = lax.broadcasted_iota(jnp.int32, (nh, page), 1)
        cn = jnp.concatenate([lnt_ref[...], jnp.zeros((nh, page - nt), F32)], axis=1)
        sh = 1
        while sh < nt:
            cn = cn + jnp.where(lane >= sh, pltpu.roll(cn, sh, 1), 0.0)
            sh *= 2
        cn_s[...] = cn
        carry_s[...] = jnp.zeros_like(carry_s)
        m_s[...] = jnp.full(m_s.shape, NEG_INF, F32)
        l_s[...] = jnp.zeros_like(l_s)
        acc_s[...] = jnp.zeros_like(acc_s)

        kn = jnp.concatenate([kn_ref[...], jnp.zeros((page - nt, w), F32)], axis=0).astype(BF16)
        vn = jnp.concatenate([vn_ref[...], jnp.zeros((page - nt, w), F32)], axis=0).astype(BF16)
        s3 = _dot_nt(qbd, kn).reshape(nt, nh, page)
        cq = jnp.stack([cn[:, t:t + 1] for t in range(nt)], axis=0)
        s3 = s3 + (cq - cn[None, :, :])
        tk = lax.broadcasted_iota(jnp.int32, (nt, nh, page), 2)
        tq = lax.broadcasted_iota(jnp.int32, (nt, nh, page), 0)
        update(jnp.where(tk <= tq, s3, NEG_INF), [vn])

    lane = lax.broadcasted_iota(jnp.int32, (nh, page), 1)
    cn = cn_s[...]
    cq = jnp.stack([cn[:, t:t + 1] for t in range(nt)], axis=0)
    qbd = qbd_s[...]
    carry = carry_s[...]
    s_tiles, v_tiles = [], []
    for g in range(g_pages):
        x = f_pages[g][...]
        incl = x
        sh = 1
        while sh < page:
            incl = incl + jnp.where(lane + sh < page, pltpu.roll(incl, page - sh, 1), 0.0)
            sh *= 2
        r = carry + (incl - x)
        carry = carry + incl[:, 0:1]
        s3 = _dot_nt(qbd, k_pages[g][...].astype(BF16)).reshape(nt, nh, page)
        s_tiles.append(s3 + (cq + r[None, :, :]))
        v_tiles.append(v_pages[g][...].astype(BF16))
    carry_s[...] = carry
    update(jnp.concatenate(s_tiles, axis=-1), v_tiles)

    @pl.when(step == pl.num_programs(1) - 1)
    def _():
        head_of_lane = lax.broadcasted_iota(jnp.int32, (nt, nh, w), 2) // hd
        head_of_row = lax.broadcasted_iota(jnp.int32, (nt, nh, w), 1)
        o3 = jnp.where(head_of_lane == head_of_row, acc_s[...] / l_s[...], 0.0)
        o2 = o3.reshape(rows, w).astype(BF16)
        sel_r = lax.broadcasted_iota(jnp.int32, (nt, rows), 0)
        sel_c = lax.broadcasted_iota(jnp.int32, (nt, rows), 1) // nh
        sel = jnp.where(sel_r == sel_c, 1.0, 0.0).astype(BF16)
        o_ref[...] = _dot(sel, o2).astype(o_ref.dtype)


def _fox_sample(q, kn, vn, lnt, cache_k, cache_v, cache_ft, page_table, layer, nh):
    nseq, n_pages = page_table.shape
    n, w = q.shape
    nt = n // nseq
    hd = w // nh
    page = cache_k.shape[2]
    g_pages = min(PAGES_PER_STEP, n_pages)
    n_steps = n_pages // g_pages

    def page_spec(shape, g):
        def idx(b, s, pt):
            return (layer, pt[b * n_pages + (n_pages - 1 - (s * g_pages + g))], 0, 0)
        return pl.BlockSpec((None, None) + shape, idx)

    seq_rows = pl.BlockSpec((nt, w), lambda b, s, pt: (b, 0))
    in_specs = [seq_rows, seq_rows, seq_rows, pl.BlockSpec((None, nh, nt), lambda b, s, pt: (b, 0, 0))]
    in_specs += [page_spec((page, w), g) for g in range(g_pages)]
    in_specs += [page_spec((page, w), g) for g in range(g_pages)]
    in_specs += [page_spec((nh, page), g) for g in range(g_pages)]
    kern = functools.partial(_fox_sample_kernel, nh=nh, hd=hd, g_pages=g_pages)
    grid_spec = pltpu.PrefetchScalarGridSpec(
        num_scalar_prefetch=1, grid=(nseq, n_steps), in_specs=in_specs,
        out_specs=pl.BlockSpec((nt, w), lambda b, s, pt: (b, 0)),
        scratch_shapes=[pltpu.VMEM((nt * nh, w), BF16),
                        pltpu.VMEM((nt, nh, 1), F32), pltpu.VMEM((nt, nh, 1), F32),
                        pltpu.VMEM((nt, nh, w), F32),
                        pltpu.VMEM((nh, page), F32), pltpu.VMEM((nh, page), F32)])
    return pl.pallas_call(
        kern, grid_spec=grid_spec, out_shape=jax.ShapeDtypeStruct((n, w), F32),
        compiler_params=_params("parallel", "arbitrary"), name="fox_sample",
    )(page_table.reshape(-1), q, kn, vn, lnt,
      *([cache_k] * g_pages), *([cache_v] * g_pages), *([cache_ft] * g_pages))


def _hgrn_kernel(q_ref, g_ref, kf_ref, v_ref, gc_ref, nw_ref, s0_ref, y_ref, sn_ref, st_s, *, c, nsub):
    ti = pl.program_id(2)

    @pl.when(ti == 0)
    def _():
        st_s[...] = s0_ref[0, 0].T

    rowi = lax.broadcasted_iota(jnp.int32, (c, 1), 0)
    nw = nw_ref[...]

    def body(i, carry):
        r0 = 0 if nsub == 1 else pl.multiple_of(i * c, c)
        rs = pl.ds(r0, c)
        q = q_ref[0, rs, :]
        g = g_ref[0, rs, :]
        kf = kf_ref[0, rs, :]
        v = v_ref[0, rs, :]
        st = st_s[...]

        b = jnp.zeros_like(g)
        b_rows = []
        run = None
        for s in range(c):
            gs = g[s:s + 1, :]
            b = b + jnp.where(rowi >= s, gs, 0.0)
            run = gs if run is None else run + gs
            b_rows.append(run)
        bl = b_rows[-1]

        o = _dot_nt((q * jnp.exp(b)).astype(BF16), st.astype(BF16))
        for s in range(c):
            xs = q * jnp.exp(jnp.minimum(b - b_rows[s], 0.0)) * kf[s:s + 1, :]
            a = jnp.sum(xs, axis=-1, keepdims=True)
            o = o + jnp.where(rowi >= s, a, 0.0) * v[s:s + 1, :]

        kd = kf * jnp.exp(bl - b)
        st_s[...] = st * jnp.exp(bl) + _dot_tn(v.astype(BF16), kd.astype(BF16))

        y = _rms(o, nw) * gc_ref[0, rs, :]
        y_ref[0, rs, :] = y.astype(y_ref.dtype)
        return carry

    if nsub == 1:
        body(0, 0)
    else:
        lax.fori_loop(0, nsub, body, 0)

    @pl.when(ti == pl.num_programs(2) - 1)
    def _():
        sn_ref[0, 0] = st_s[...].T


def _hgrn(qs, gl, kf, iv, gcs, nw, s0):
    b, t, _ = qs.shape
    _, nh, kd, vd = s0.shape
    c = HGRN_SUB if t % HGRN_SUB == 0 else t
    tt = HGRN_TILE if t % HGRN_TILE == 0 else t
    kern = functools.partial(_hgrn_kernel, c=c, nsub=tt // c)
    seq = pl.BlockSpec((1, tt, kd), lambda i, h, j: (i, j, h))
    state = pl.BlockSpec((1, 1, kd, vd), lambda i, h, j: (i, h, 0, 0))
    return pl.pallas_call(
        kern, grid=(b, nh, t // tt),
        in_specs=[seq, seq, seq, seq, seq, pl.BlockSpec((1, vd), lambda i, h, j: (0, h)), state],
        out_specs=(seq, state),
        out_shape=(jax.ShapeDtypeStruct((b, t, nh * vd), BF16), jax.ShapeDtypeStruct(s0.shape, F32)),
        scratch_shapes=[pltpu.VMEM((vd, kd), F32)],
        compiler_params=_params("parallel", "parallel", "arbitrary"), name="hgrn",
    )(qs, gl, kf, iv, gcs, nw, s0)


def _merge_kernel(x_ref, ya_ref, ub_ref, vb_ref, yc_ref, gt_ref, gm_ref, gb_ref,
                  woa_ref, wob_ref, woc_ref, wo_ref, o_ref, *, n_groups):
    tm = x_ref.shape[0]
    d = x_ref.shape[1]
    cn = gm_ref.shape[1]
    gw = ub_ref.shape[1] // n_groups
    yb_rows = []
    for ci in range(tm // cn):
        rs = slice(ci * cn, (ci + 1) * cn)
        vb = vb_ref[rs, :].astype(BF16)
        gb = gb_ref[...]
        parts = []
        for g in range(n_groups):
            parts.append(_dot(gm_ref[g], vb[:, g * gw:(g + 1) * gw]) + gb[:, g:g + 1])
        yb_rows.append(ub_ref[rs, :] * jnp.concatenate(parts, axis=1))
    yb = jnp.concatenate(yb_rows, axis=0) if len(yb_rows) > 1 else yb_rows[0]
    gt = gt_ref[...]
    merged = (gt[:, :d] * _dot(ya_ref[...].astype(BF16), woa_ref[...])
              + gt[:, d:2 * d] * _dot(yb.astype(BF16), wob_ref[...])
              + gt[:, 2 * d:] * _dot(yc_ref[...].astype(BF16), woc_ref[...]))
    o_ref[...] = x_ref[...] + _dot(merged.astype(BF16), wo_ref[...])


def _merge(x, ya, ub, vb, yc, gt, gm, gb, woa, wob, woc, wo):
    n, d = x.shape
    tm = min(ROW_TILE, n)

    def rows(w):
        return pl.BlockSpec((tm, w), lambda i: (i, 0))

    kern = functools.partial(_merge_kernel, n_groups=gm.shape[0])
    return pl.pallas_call(
        kern, grid=(n // tm,),
        in_specs=[rows(d), rows(ya.shape[1]), rows(ub.shape[1]), rows(vb.shape[1]), rows(yc.shape[1]),
                  rows(gt.shape[1]), _const_spec(gm.shape), _const_spec(gb.shape),
                  _const_spec(woa.shape), _const_spec(wob.shape), _const_spec(woc.shape),
                  _const_spec(wo.shape)],
        out_specs=rows(d), out_shape=jax.ShapeDtypeStruct((n, d), F32),
        compiler_params=_params("parallel"), name="merge",
    )(x, ya, ub, vb, yc, gt, gm, gb, woa, wob, woc, wo)


def _ffn_kernel(x_ref, n2_ref, wup_ref, wcv_ref, bcv_ref, wdn_ref, buf_ref, nf_ref,
                y_ref, nb_ref, carry_s, ext_s, *, shift, base, final_norm):
    ti = pl.program_id(1)
    tm = x_ref.shape[1]
    ncw = wup_ref.shape[0]
    cw = wdn_ref.shape[1]
    hist = 2 * shift

    @pl.when(ti == 0)
    def _():
        carry_s[...] = buf_ref[0]

    x = x_ref[0]
    h = _rms(x, n2_ref[...]).astype(BF16)
    acc = jnp.zeros(x.shape, F32)
    for c in range(ncw):
        up = _dot(h, wup_ref[c])
        ext_s[base - hist:base, :] = carry_s[c]
        ext_s[base:base + tm, :] = up
        wcv = wcv_ref[c]
        conv = bcv_ref[c] + (wcv[0:1, :] * ext_s[base - hist:base - hist + tm, :]
                             + wcv[1:2, :] * ext_s[base - shift:base - shift + tm, :]
                             + wcv[2:3, :] * up)
        carry_s[c] = up[tm - hist:, :]
        act = _silu(conv[:, :cw]) * conv[:, cw:]
        acc = acc + _dot(act.astype(BF16), wdn_ref[c])
    y = x + acc
    if final_norm:
        y = _rms(y, nf_ref[...])
    y_ref[0] = y
    nb_ref[0] = carry_s[...]


def _ffn(x, n2, wup, wcv, bcv, wdn, buf, nf, shift, final_norm):
    nseq, t, d = x.shape
    tm = min(ROW_TILE, t)
    ncw, _, cw2 = wup.shape
    hist = 2 * shift
    base = -(-hist // SUBLANES) * SUBLANES
    kern = functools.partial(_ffn_kernel, shift=shift, base=base, final_norm=final_norm)
    state = pl.BlockSpec((1, ncw, hist, cw2), lambda i, j: (i, 0, 0, 0))
    return pl.pallas_call(
        kern, grid=(nseq, t // tm),
        in_specs=[pl.BlockSpec((1, tm, d), lambda i, j: (i, j, 0)), _const_spec(n2.shape),
                  _const_spec(wup.shape), _const_spec(wcv.shape), _const_spec(bcv.shape),
                  _const_spec(wdn.shape), state, _const_spec(nf.shape)],
        out_specs=(pl.BlockSpec((1, tm, d), lambda i, j: (i, j, 0)), state),
        out_shape=(jax.ShapeDtypeStruct(x.shape, F32), jax.ShapeDtypeStruct(buf.shape, F32)),
        scratch_shapes=[pltpu.VMEM((ncw, hist, cw2), F32), pltpu.VMEM((base + tm, cw2), F32)],
        compiler_params=_params("parallel", "arbitrary"), name="conv_ffn",
    )(x, n2, wup, wcv, bcv, wdn, buf, nf)


def _ffn_cols_to_chunks(a, d_ff, cw):
    ncw = d_ff // cw
    lead = a.shape[:-1]
    a = a.reshape(lead + (2, ncw, cw))
    a = jnp.moveaxis(a, -2, 0)
    return a.reshape((ncw,) + lead + (2 * cw,))


def _ffn_chunks_to_cols(a, d_ff, cw):
    ncw = d_ff // cw
    lead = a.shape[1:-1]
    a = a.reshape((ncw,) + lead + (2, cw))
    a = jnp.moveaxis(a, 0, -2)
    return a.reshape(lead + (2 * d_ff,))


def kernel(x_prompt, x_sample, cache_k, cache_v, cache_logf, state_hgrn, state_conv, page_table,
           norm1, w_in, b_f, gmlp_norm, w_s, b_s, hgrn_norm, hgrn_lb, w_oa, w_ob, w_oc, w_o,
           norm2, w_up, w_conv, b_conv, w_down, norm_f):
    depth = w_in.shape[0]
    bp, tp, d = x_prompt.shape
    bs, ts, _ = x_sample.shape
    _, n_phys, page, nh_a, hd_a = cache_k.shape
    w_a = nh_a * hd_a
    n_grp, chunk_b, _ = w_s.shape[1:]
    w_b = gmlp_norm.shape[1]
    _, _, nh_c, k_c, v_c = state_hgrn.shape
    w_c = nh_c * v_c
    d_ff = w_down.shape[1]
    cw = FFN_COLS
    ncw = d_ff // cw

    sm = jax.nn.softmax(hgrn_lb.astype(F32), axis=0)
    lower = jnp.cumsum(sm, axis=0) - sm[:1]

    cache_k2 = cache_k.reshape(depth, n_phys, page, w_a)
    cache_v2 = cache_v.reshape(depth, n_phys, page, w_a)
    cache_ft = jnp.swapaxes(cache_logf, 2, 3)

    xp = x_prompt.reshape(bp * tp, d)
    xs = x_sample.reshape(bs * ts, d)
    st_p, st_s = [], []
    for l in range(depth):
        o = 0
        wl = w_in[l]
        wa = wl[:, o:o + 3 * w_a]; o += 3 * w_a
        wf = wl[:, o:o + nh_a]; o += nh_a
        wb = wl[:, o:o + 2 * w_b]; o += 2 * w_b
        wc = wl[:, o:o + 4 * w_c]; o += 4 * w_c
        wg = wl[:, o:]
        qscale = jnp.concatenate([jnp.full((w_a,), hd_a ** -0.5, F32), jnp.ones((2 * w_a,), F32)])
        wa = (wa * qscale).astype(BF16)
        wf_pad = jnp.pad(wf, ((0, 0), (0, LANES - nh_a))).astype(BF16)
        wft = wf.T.astype(BF16)
        bf_row = jnp.pad(b_f[l], (0, LANES - nh_a)).reshape(1, LANES)
        bf_col = b_f[l].reshape(nh_a, 1)
        inproj_w = (norm1[l].reshape(1, d), wa, wf_pad, wft, wb.astype(BF16), wc.astype(BF16),
                    wg.astype(BF16), bf_row, bf_col, gmlp_norm[l].reshape(1, w_b), lower[l].reshape(1, w_c))

        tril_w = jnp.where(jnp.tril(jnp.ones((chunk_b, chunk_b), bool)), w_s[l], 0)
        woa, wob, woc, wo = (w_oa[l].astype(BF16), w_ob[l].astype(BF16), w_oc[l].astype(BF16),
                             w_o[l].astype(BF16))
        nw = hgrn_norm[l].reshape(1, w_c)

        n2 = norm2[l].reshape(1, d)
        wup = _ffn_cols_to_chunks(w_up[l], d_ff, cw).astype(BF16)
        wcv = _ffn_cols_to_chunks(w_conv[l], d_ff, cw)
        bcv = _ffn_cols_to_chunks(b_conv[l].reshape(1, -1), d_ff, cw)
        wdn = w_down[l].reshape(ncw, cw, d).astype(BF16)
        nf = norm_f.reshape(1, d)
        last = l == depth - 1

        (q, k, v, lf, lft, ub, vb, qs, gl, kf, iv, gcs, gt) = _inproj(xp, *inproj_w)
        ya = _fox_prompt(q.reshape(bp, tp, w_a), k.reshape(bp, tp, w_a), v.reshape(bp, tp, w_a),
                         lf.reshape(bp, tp, LANES), lft, nh_a)
        s0 = jnp.zeros((bp, nh_c, k_c, v_c), F32)
        yc, s_new = _hgrn(*(a.reshape(bp, tp, w_c) for a in (qs, gl, kf, iv, gcs)), nw, s0)
        x1 = _merge(xp, ya.reshape(bp * tp, w_a), ub, vb, yc.reshape(bp * tp, w_c), gt,
                    tril_w.astype(BF16), b_s[l].T, woa, wob, woc, wo)
        buf0 = jnp.zeros((bp, ncw, 2, 2 * cw), F32)
        x2, nb = _ffn(x1.reshape(bp, tp, d), n2, wup, wcv, bcv, wdn, buf0, nf, 1, last)
        xp = x2.reshape(bp * tp, d)
        conv_p = _ffn_chunks_to_cols(jnp.moveaxis(nb, 1, 0), d_ff, cw)
        st_p.append((k.reshape(bp, tp, nh_a, hd_a), v.reshape(bp, tp, nh_a, hd_a),
                     lf[:, :nh_a].reshape(bp, tp, nh_a), s_new, conv_p))

        (q, k, v, lf, lft, ub, vb, qs, gl, kf, iv, gcs, gt) = _inproj(xs, *inproj_w)
        lnt = jnp.swapaxes(lf[:, :nh_a].reshape(bs, ts, nh_a), 1, 2)
        ya = _fox_sample(q.astype(F32), k, v, lnt, cache_k2, cache_v2, cache_ft, page_table, l, nh_a)
        yc, s_new = _hgrn(*(a.reshape(bs, ts, w_c) for a in (qs, gl, kf, iv, gcs)), nw, state_hgrn[l])
        gm = jnp.einsum("ab,gts->gatbs", jnp.eye(bs, dtype=F32), tril_w[:, :ts, :ts])
        gm = gm.reshape(n_grp, bs * ts, bs * ts).astype(BF16)
        gb = jnp.tile(b_s[l][:, :ts].T, (bs, 1))
        x1 = _merge(xs, ya, ub, vb, yc.reshape(bs * ts, w_c), gt, gm, gb, woa, wob, woc, wo)
        x1t = jnp.swapaxes(x1.reshape(bs, ts, d), 0, 1).reshape(1, ts * bs, d)
        buf_t = jnp.swapaxes(state_conv[l], 0, 1).reshape(2 * bs, 2 * d_ff)
        buf_t = _ffn_cols_to_chunks(buf_t, d_ff, cw)[None]
        x2t, nb = _ffn(x1t, n2, wup, wcv, bcv, wdn, buf_t, nf, bs, last)
        xs = jnp.swapaxes(x2t.reshape(ts, bs, d), 0, 1).reshape(bs * ts, d)
        conv_s = _ffn_chunks_to_cols(nb[0], d_ff, cw).reshape(2, bs, 2 * d_ff)
        st_s.append((k.reshape(bs, ts, nh_a, hd_a), v.reshape(bs, ts, nh_a, hd_a),
                     lf[:, :nh_a].reshape(bs, ts, nh_a), vb.reshape(bs, ts, w_b), s_new,
                     jnp.swapaxes(conv_s, 0, 1)))

    def stk(sts, i):
        return jnp.stack([st[i] for st in sts], axis=0)

    return (xp.reshape(bp, tp, d), xs.reshape(bs, ts, d),
            stk(st_p, 0), stk(st_p, 1), stk(st_p, 2), stk(st_p, 3), stk(st_p, 4),
            stk(st_s, 0), stk(st_s, 1), stk(st_s, 2), stk(st_s, 3), stk(st_s, 4), stk(st_s, 5))
```

```python
import functools

import numpy as np

import jax
import jax.numpy as jnp
from jax import lax
from jax.experimental import pallas as pl
from jax.experimental.pallas import tpu as pltpu

F32 = jnp.float32
BF16 = jnp.bfloat16
EPS = 1e-6
NEG_INF = float("-inf")

LANES = 128
SUBLANES = 8
VMEM_LIMIT_BYTES = 56 * 1024 * 1024

ROW_TILE = 256
ATTN_TILE = 256
PAGES_PER_STEP = 8
HGRN_SUB = 16
HGRN_TILE = 256
FFN_COLS = 256
N_SPLIT = 3


def _dot(a, b):
    return jnp.dot(a, b, preferred_element_type=F32)


def _dot_nt(a, b):
    return lax.dot_general(a, b, (((1,), (1,)), ((), ())), preferred_element_type=F32)


def _dot_tn(a, b):
    return lax.dot_general(a, b, (((0,), (0,)), ((), ())), preferred_element_type=F32)


def _rms(x, g):
    return x * lax.rsqrt(jnp.mean(x * x, axis=-1, keepdims=True) + EPS) * g


def _sigmoid(x):
    return 1.0 / (1.0 + jnp.exp(-x))


def _silu(x):
    return x * _sigmoid(x)


def _gelu_tanh(x):
    c = 0.7978845608028654
    return x * (0.5 * (1.0 + jnp.tanh(c * (x + 0.044715 * (x * x * x)))))


def _log_sigmoid(z):
    return jnp.minimum(z, 0.0) - jnp.log1p(jnp.exp(-jnp.abs(z)))


def _split3(x):
    hi = x.astype(BF16)
    r = x - hi.astype(F32)
    mid = r.astype(BF16)
    lo = (r - mid.astype(F32)).astype(BF16)
    return hi, mid, lo


def _ones_tri(n, upper):
    r = lax.broadcasted_iota(jnp.int32, (n, n), 0)
    c = lax.broadcasted_iota(jnp.int32, (n, n), 1)
    keep = (r <= c) if upper else (r >= c)
    return jnp.where(keep, 1.0, 0.0).astype(BF16)


def _const_spec(shape):
    nd = len(shape)
    return pl.BlockSpec(shape, lambda *_: (0,) * nd, pipeline_mode=pl.Buffered(1))


def _params(*sem):
    return pltpu.CompilerParams(dimension_semantics=sem, vmem_limit_bytes=VMEM_LIMIT_BYTES)


def _inproj_kernel(x_ref, n1_ref, wa_ref, wf_ref, wb_ref, wc_ref, wg_ref, bf_ref, gn_ref, lb_ref,
                   q_ref, k_ref, v_ref, lf_ref, ub_ref, vb_ref,
                   qs_ref, gl_ref, kf_ref, iv_ref, gc_ref, gt_ref):
    wa_w = k_ref.shape[-1]
    wb_w = ub_ref.shape[-1]
    wc_w = qs_ref.shape[-1]
    h = _rms(x_ref[...], n1_ref[...]).astype(BF16)

    pa = _dot(h, wa_ref[...])
    q_ref[...] = pa[:, :wa_w].astype(q_ref.dtype)
    k_ref[...] = pa[:, wa_w:2 * wa_w]
    v_ref[...] = pa[:, 2 * wa_w:]

    lf_ref[...] = _log_sigmoid(_dot(h, wf_ref[...]) + bf_ref[...])

    pb = _dot(h, wb_ref[...])
    ub_ref[...] = _gelu_tanh(pb[:, :wb_w])
    vb_ref[...] = _rms(_gelu_tanh(pb[:, wb_w:]), gn_ref[...])

    pc = _dot(h, wc_ref[...])
    lb = lb_ref[...]
    qs_ref[...] = _silu(pc[:, :wc_w])
    f = lb + (1.0 - lb) * _sigmoid(pc[:, wc_w:2 * wc_w])
    gl_ref[...] = jnp.log(f)
    kf_ref[...] = 1.0 - f
    iv_ref[...] = pc[:, 2 * wc_w:3 * wc_w]
    gc_ref[...] = _silu(pc[:, 3 * wc_w:])

    gt_ref[...] = _sigmoid(_dot(h, wg_ref[...]))


def _inproj(x, n1, wa, wf, wb, wc, wg, bf_row, gn, lb):
    n, d = x.shape
    tm = min(ROW_TILE, n)
    wa_w, wb_w, wc_w = wa.shape[1] // 3, wb.shape[1] // 2, wc.shape[1] // 4

    def rows(w):
        return pl.BlockSpec((tm, w), lambda i: (i, 0))

    out_shape = (
        jax.ShapeDtypeStruct((n, wa_w), BF16),
        jax.ShapeDtypeStruct((n, wa_w), F32),
        jax.ShapeDtypeStruct((n, wa_w), F32),
        jax.ShapeDtypeStruct((n, LANES), F32),
        jax.ShapeDtypeStruct((n, wb_w), F32),
        jax.ShapeDtypeStruct((n, wb_w), F32),
        jax.ShapeDtypeStruct((n, wc_w), F32),
        jax.ShapeDtypeStruct((n, wc_w), F32),
        jax.ShapeDtypeStruct((n, wc_w), F32),
        jax.ShapeDtypeStruct((n, wc_w), F32),
        jax.ShapeDtypeStruct((n, wc_w), F32),
        jax.ShapeDtypeStruct((n, wg.shape[1]), F32),
    )
    out_specs = (rows(wa_w), rows(wa_w), rows(wa_w), rows(LANES),
                 rows(wb_w), rows(wb_w), rows(wc_w), rows(wc_w), rows(wc_w), rows(wc_w), rows(wc_w),
                 rows(wg.shape[1]))
    in_specs = [rows(d), _const_spec(n1.shape), _const_spec(wa.shape), _const_spec(wf.shape),
                _const_spec(wb.shape), _const_spec(wc.shape), _const_spec(wg.shape),
                _const_spec(bf_row.shape), _const_spec(gn.shape), _const_spec(lb.shape)]
    return pl.pallas_call(
        _inproj_kernel, grid=(n // tm,), in_specs=in_specs, out_specs=out_specs, out_shape=out_shape,
        compiler_params=_params("parallel"), name="inproj",
    )(x, n1, wa, wf, wb, wc, wg, bf_row, gn, lb)


def _decay_selectors(nh, hd):
    sel_q = np.zeros((N_SPLIT * LANES, nh * hd), np.float32)
    sel_k = np.zeros((N_SPLIT * LANES, nh * hd), np.float32)
    one_q = np.zeros((1, nh * hd), np.float32)
    one_k = np.zeros((1, nh * hd), np.float32)
    for h in range(nh):
        for p in range(N_SPLIT):
            sel_k[p * LANES + h, h * hd + p] = -1.0
            one_k[0, h * hd + N_SPLIT + p] = 1.0
            one_q[0, h * hd + p] = 1.0
            sel_q[p * LANES + h, h * hd + N_SPLIT + p] = 1.0
    return (jnp.asarray(sel_q, BF16), jnp.asarray(sel_k, BF16), jnp.asarray(one_q), jnp.asarray(one_k))


def _fox_prep_kernel(q_ref, k_ref, v_ref, lf_ref, selq_ref, selk_ref, oneq_ref, onek_ref,
                     qa_ref, ka_ref, vt_ref, carry_s):
    ti = pl.program_id(1)
    tm = q_ref.shape[1]
    w = q_ref.shape[2]

    @pl.when(ti == 0)
    def _():
        carry_s[...] = jnp.zeros_like(carry_s)

    lower = _ones_tri(tm, upper=False)
    hi, mid, lo = _split3(lf_ref[0])
    c = _dot(lower, hi) + _dot(lower, mid) + _dot(lower, lo) + carry_s[...]
    carry_s[...] = c[tm - 1:tm, :]
    c3 = jnp.concatenate(_split3(c), axis=1)
    eq = _dot(c3, selq_ref[...]) + oneq_ref[...]
    ek = _dot(c3, selk_ref[...]) + onek_ref[...]
    q = q_ref[0].astype(F32)
    k = k_ref[0]
    qa_parts, ka_parts = [], []
    for p in range(w // LANES):
        ls = slice(p * LANES, (p + 1) * LANES)
        qa_parts += [q[:, ls], eq[:, ls]]
        ka_parts += [k[:, ls], ek[:, ls]]
    qa_ref[0] = jnp.concatenate(qa_parts, axis=1).astype(BF16)
    ka_ref[0] = jnp.concatenate(ka_parts, axis=1).astype(BF16)
    vt_ref[0, 0] = v_ref[0].T.astype(BF16)


def _fox_prompt_kernel(qa_ref, ka_ref, vt_ref, o_ref, qh_s, m_s, l_s, acc_s, *, hd, tq):
    qi = pl.program_id(1)
    heads_per_slab = LANES // hd
    n_slab = vt_ref.shape[2] // LANES
    heads = [(sl, hh) for sl in range(n_slab) for hh in range(heads_per_slab)]
    lane2 = lax.broadcasted_iota(jnp.int32, (tq, 2 * LANES), 1) % LANES
    key = lax.broadcasted_iota(jnp.int32, (tq, tq), 0)
    qry = lax.broadcasted_iota(jnp.int32, (tq, tq), 1)

    for n, (sl, hh) in enumerate(heads):
        qa = qa_ref[0, :, sl * 2 * LANES:(sl + 1) * 2 * LANES].astype(F32)
        qh_s[n] = jnp.where(lane2 // hd == hh, qa, 0.0).astype(BF16)
        m_s[n] = jnp.full((1, tq), NEG_INF, F32)
        l_s[n] = jnp.zeros((1, tq), F32)
        acc_s[n] = jnp.zeros((LANES, tq), F32)

    def tile(j, masked):
        k0 = pl.multiple_of(j * tq, tq)
        s_all = []
        for n, (sl, _) in enumerate(heads):
            s = _dot_nt(ka_ref[0, pl.ds(k0, tq), sl * 2 * LANES:(sl + 1) * 2 * LANES], qh_s[n])
            s_all.append(jnp.where(key <= qry, s, NEG_INF) if masked else s)
        m_old = [m_s[n] for n in range(len(heads))]
        m_new = [jnp.maximum(m, jnp.max(s, axis=0, keepdims=True)) for m, s in zip(m_old, s_all)]
        p_all = [jnp.exp(s - m) for s, m in zip(s_all, m_new)]
        for n, (sl, _) in enumerate(heads):
            alpha = jnp.exp(m_old[n] - m_new[n])
            l_s[n] = alpha * l_s[n] + jnp.sum(p_all[n], axis=0, keepdims=True)
            pv = _dot(vt_ref[0, j, sl * LANES:(sl + 1) * LANES, :], p_all[n].astype(BF16))
            acc_s[n] = alpha * acc_s[n] + pv
            m_s[n] = m_new[n]

    def body(j, carry):
        tile(j, False)
        return carry

    lax.fori_loop(0, qi, body, 0)
    tile(qi, True)
    for sl in range(n_slab):
        parts = [(acc_s[n] / l_s[n])[hh * hd:(hh + 1) * hd, :] for n, (s2, hh) in enumerate(heads) if s2 == sl]
        o_ref[0, :, sl * LANES:(sl + 1) * LANES] = jnp.concatenate(parts, axis=0).T.astype(o_ref.dtype)


def _fox_prompt(q, k, v, lf, nh):
    b, t, w = k.shape
    hd = w // nh
    tq = min(ATTN_TILE, t)
    nkv = t // tq
    sel_q, sel_k, one_q, one_k = _decay_selectors(nh, hd)
    blk = lambda width: pl.BlockSpec((1, tq, width), lambda i, j: (i, j, 0))
    qa, ka, vt = pl.pallas_call(
        _fox_prep_kernel, grid=(b, t // tq),
        in_specs=[blk(w), blk(w), blk(w), blk(LANES), _const_spec(sel_q.shape), _const_spec(sel_k.shape),
                  _const_spec(one_q.shape), _const_spec(one_k.shape)],
        out_specs=(blk(2 * w), blk(2 * w), pl.BlockSpec((1, 1, w, tq), lambda i, j: (i, j, 0, 0))),
        out_shape=(jax.ShapeDtypeStruct((b, t, 2 * w), BF16), jax.ShapeDtypeStruct((b, t, 2 * w), BF16),
                   jax.ShapeDtypeStruct((b, nkv, w, tq), BF16)),
        scratch_shapes=[pltpu.VMEM((1, LANES), F32)],
        compiler_params=_params("parallel", "arbitrary"), name="fox_prep",
    )(q, k, v, lf, sel_q, sel_k, one_q, one_k)
    kern = functools.partial(_fox_prompt_kernel, hd=hd, tq=tq)
    return pl.pallas_call(
        kern, grid=(b, nkv),
        in_specs=[pl.BlockSpec((1, tq, 2 * w), lambda i, j: (i, j, 0)),
                  pl.BlockSpec((1, t, 2 * w), lambda i, j: (i, 0, 0)),
                  pl.BlockSpec((1, nkv, w, tq), lambda i, j: (i, 0, 0, 0))],
        out_specs=pl.BlockSpec((1, tq, w), lambda i, j: (i, j, 0)),
        out_shape=jax.ShapeDtypeStruct((b, t, w), BF16),
        scratch_shapes=[pltpu.VMEM((nh, tq, 2 * LANES), BF16), pltpu.VMEM((nh, 1, tq), F32),
                        pltpu.VMEM((nh, 1, tq), F32), pltpu.VMEM((nh, LANES, tq), F32)],
        compiler_params=_params("parallel", "arbitrary"), name="fox_prompt",
    )(qa, ka, vt)


def _fox_sample_kernel(pt_ref, q_ref, kn_ref, vn_ref, lnr_ref, lnc_ref, *rest, g_pages):
    k_pages = rest[:g_pages]
    v_pages = rest[g_pages:2 * g_pages]
    f_pages = rest[2 * g_pages:3 * g_pages]
    o_ref, m_s, l_s, acc_s, carry_s, cnc_s = rest[3 * g_pages:]
    del pt_ref
    step = pl.program_id(1)
    nt, nh, hd = q_ref.shape
    page = k_pages[0].shape[0]
    rows = nt * nh
    n_blk = page * nh // LANES

    qa = q_ref[...].reshape(rows, hd).astype(BF16)

    def update(s_list, v_list):
        m_old = m_s[...]
        m_new = m_old
        for s in s_list:
            m_new = jnp.maximum(m_new, jnp.max(s, axis=-1, keepdims=True))
        alpha = jnp.exp(m_old - m_new)
        l_new = alpha * l_s[...]
        acc = alpha * acc_s[...]
        for s, v2 in zip(s_list, v_list):
            p = jnp.exp(s - m_new)
            l_new = l_new + jnp.sum(p, axis=-1, keepdims=True)
            acc = acc + _dot(p.astype(BF16), v2)
        l_s[...] = l_new
        acc_s[...] = acc
        m_s[...] = m_new

    @pl.when(step == 0)
    def _():
        m_s[...] = jnp.full(m_s.shape, NEG_INF, F32)
        l_s[...] = jnp.zeros_like(l_s)
        acc_s[...] = jnp.zeros_like(acc_s)
        carry_s[...] = jnp.zeros_like(carry_s)

        x = lnc_ref[...]
        parts = []
        run = None
        for t in range(nt):
            xt = x[t * nh:(t + 1) * nh, :]
            run = xt if run is None else run + xt
            parts.append(run)
        cn_col = jnp.concatenate(parts, axis=0)
        cnc_s[...] = cn_col
        lane = lax.broadcasted_iota(jnp.int32, (1, LANES), 1)
        cn_row = lnr_ref[...]
        sh = nh
        while sh < rows:
            cn_row = cn_row + jnp.where(lane >= sh, pltpu.roll(cn_row, sh, 1), 0.0)
            sh *= 2

        kn = kn_ref[...].reshape(rows, hd).astype(BF16)
        vn = vn_ref[...].reshape(rows, hd).astype(BF16)
        s = _dot_nt(qa, kn) + (cn_col - cn_row[:, :rows])
        r_i = lax.broadcasted_iota(jnp.int32, (rows, rows), 0)
        c_i = lax.broadcasted_iota(jnp.int32, (rows, rows), 1)
        keep = (r_i % nh == c_i % nh) & (c_i // nh <= r_i // nh)
        update([jnp.where(keep, s, NEG_INF)], [vn])

    lane = lax.broadcasted_iota(jnp.int32, (n_blk, LANES), 1)
    sub = lax.broadcasted_iota(jnp.int32, (n_blk, LANES), 0)
    own = (lax.broadcasted_iota(jnp.int32, (rows, LANES), 0) % nh
           == lax.broadcasted_iota(jnp.int32, (rows, LANES), 1) % nh)
    cn_col = cnc_s[...]
    carry = carry_s[...]
    s_list, v_list = [], []
    for g in range(g_pages):
        x = f_pages[g][...]
        incl = x
        sh = nh
        while sh < LANES:
            incl = incl + jnp.where(lane + sh < LANES, pltpu.roll(incl, LANES - sh, 1), 0.0)
            sh *= 2
        tot = jnp.where(lane < nh, incl, 0.0)
        sh = nh
        while sh < LANES:
            tot = tot + pltpu.roll(tot, sh, 1)
            sh *= 2
        rsum = tot
        sh = 1
        while sh < n_blk:
            rsum = rsum + jnp.where(sub + sh < n_blk, pltpu.roll(rsum, n_blk - sh, 0), 0.0)
            sh *= 2
        r = carry + (rsum - tot) + (incl - x)
        carry = carry + rsum[0:1, :]

        k2 = k_pages[g][...].reshape(page * nh, hd).astype(BF16)
        v2 = v_pages[g][...].reshape(page * nh, hd).astype(BF16)
        s_all = _dot_nt(qa, k2)
        blocks = []
        for i in range(n_blk):
            sb = s_all[:, i * LANES:(i + 1) * LANES] + (cn_col + r[i:i + 1, :])
            blocks.append(jnp.where(own, sb, NEG_INF))
        s_list.append(jnp.concatenate(blocks, axis=1))
        v_list.append(v2)
    carry_s[...] = carry
    update(s_list, v_list)

    @pl.when(step == pl.num_programs(1) - 1)
    def _():
        o_ref[...] = (acc_s[...] / l_s[...]).reshape(nt, nh, hd)


def _fox_sample(q, kn, vn, lf_new, cache_k, cache_v, cache_f, page_table, layer):
    nseq, n_pages = page_table.shape
    _, nt, nh, hd = q.shape
    page = cache_k.shape[2]
    rows = nt * nh
    g_pages = min(PAGES_PER_STEP, n_pages)
    n_steps = n_pages // g_pages
    lnr = jnp.pad(lf_new.reshape(nseq, 1, rows), ((0, 0), (0, 0), (0, LANES - rows)))
    lnc = lf_new.reshape(nseq, rows, 1)

    def page_spec(shape, g):
        nd = len(shape)

        def idx(b, s, pt):
            return (layer, pt[b * n_pages + (n_pages - 1 - (s * g_pages + g))]) + (0,) * nd
        return pl.BlockSpec((None, None) + shape, idx)

    seq = pl.BlockSpec((None, nt, nh, hd), lambda b, s, pt: (b, 0, 0, 0))
    in_specs = [seq, seq, seq,
                pl.BlockSpec((None, 1, LANES), lambda b, s, pt: (b, 0, 0)),
                pl.BlockSpec((None, rows, 1), lambda b, s, pt: (b, 0, 0))]
    in_specs += [page_spec((page, nh, hd), g) for g in range(g_pages)]
    in_specs += [page_spec((page, nh, hd), g) for g in range(g_pages)]
    in_specs += [page_spec(cache_f.shape[2:], g) for g in range(g_pages)]
    kern = functools.partial(_fox_sample_kernel, g_pages=g_pages)
    grid_spec = pltpu.PrefetchScalarGridSpec(
        num_scalar_prefetch=1, grid=(nseq, n_steps), in_specs=in_specs, out_specs=seq,
        scratch_shapes=[pltpu.VMEM((rows, 1), F32), pltpu.VMEM((rows, 1), F32), pltpu.VMEM((rows, hd), F32),
                        pltpu.VMEM((1, LANES), F32), pltpu.VMEM((rows, 1), F32)])
    return pl.pallas_call(
        kern, grid_spec=grid_spec, out_shape=jax.ShapeDtypeStruct(q.shape, F32),
        compiler_params=_params("parallel", "arbitrary"), name="fox_sample",
    )(page_table.reshape(-1), q, kn, vn, lnr, lnc,
      *([cache_k] * g_pages), *([cache_v] * g_pages), *([cache_f] * g_pages))


def _hgrn_kernel(q_ref, g_ref, kf_ref, v_ref, gc_ref, nw_ref, s0_ref, y_ref, sn_ref, st_s, *, c, nsub, kd):
    ti = pl.program_id(1)
    nh = s0_ref.shape[1]

    @pl.when(ti == 0)
    def _():
        for h in range(nh):
            st_s[h] = s0_ref[0, h].T

    rowi = lax.broadcasted_iota(jnp.int32, (c, 1), 0)

    def head_step(rs, h):
        hs = slice(h * kd, (h + 1) * kd)
        q = q_ref[0, rs, hs]
        g = g_ref[0, rs, hs]
        kf = kf_ref[0, rs, hs]
        v = v_ref[0, rs, hs]
        st = st_s[h]

        b = jnp.zeros_like(g)
        b_rows = []
        run = None
        for s in range(c):
            gs = g[s:s + 1, :]
            b = b + jnp.where(rowi >= s, gs, 0.0)
            run = gs if run is None else run + gs
            b_rows.append(run)
        bl = b_rows[-1]

        o = _dot_nt((q * jnp.exp(b)).astype(BF16), st.astype(BF16))
        for s in range(c):
            xs = q * jnp.exp(jnp.minimum(b - b_rows[s], 0.0)) * kf[s:s + 1, :]
            a = jnp.sum(xs, axis=-1, keepdims=True)
            o = o + jnp.where(rowi >= s, a, 0.0) * v[s:s + 1, :]

        kd_ = kf * jnp.exp(bl - b)
        st_s[h] = st * jnp.exp(bl) + _dot_tn(v.astype(BF16), kd_.astype(BF16))

        y = _rms(o, nw_ref[:, hs]) * gc_ref[0, rs, hs]
        y_ref[0, rs, hs] = y.astype(y_ref.dtype)

    def body(i, carry):
        rs = pl.ds(0 if nsub == 1 else pl.multiple_of(i * c, c), c)
        for h in range(nh):
            head_step(rs, h)
        return carry

    if nsub == 1:
        body(0, 0)
    else:
        lax.fori_loop(0, nsub, body, 0)

    @pl.when(ti == pl.num_programs(1) - 1)
    def _():
        for h in range(nh):
            sn_ref[0, h] = st_s[h].T


def _hgrn(qs, gl, kf, iv, gcs, nw, s0):
    b, t, w = qs.shape
    _, nh, kd, vd = s0.shape
    c = HGRN_SUB if t % HGRN_SUB == 0 else t
    tt = HGRN_TILE if t % HGRN_TILE == 0 else t
    kern = functools.partial(_hgrn_kernel, c=c, nsub=tt // c, kd=kd)
    seq = pl.BlockSpec((1, tt, w), lambda i, j: (i, j, 0))
    state = pl.BlockSpec((1, nh, kd, vd), lambda i, j: (i, 0, 0, 0))
    return pl.pallas_call(
        kern, grid=(b, t // tt),
        in_specs=[seq, seq, seq, seq, seq, _const_spec(nw.shape), state],
        out_specs=(seq, state),
        out_shape=(jax.ShapeDtypeStruct((b, t, nh * vd), BF16), jax.ShapeDtypeStruct(s0.shape, F32)),
        scratch_shapes=[pltpu.VMEM((nh, vd, kd), F32)],
        compiler_params=_params("parallel", "arbitrary"), name="hgrn",
    )(qs, gl, kf, iv, gcs, nw, s0)


def _merge_kernel(x_ref, ya_ref, ub_ref, vb_ref, yc_ref, gt_ref, gm_ref, gb_ref,
                  woa_ref, wob_ref, woc_ref, wo_ref, o_ref, *, n_groups):
    tm = x_ref.shape[0]
    d = x_ref.shape[1]
    cn = gm_ref.shape[1]
    gw = ub_ref.shape[1] // n_groups
    yb_rows = []
    for ci in range(tm // cn):
        rs = slice(ci * cn, (ci + 1) * cn)
        vb = vb_ref[rs, :].astype(BF16)
        gb = gb_ref[...]
        parts = []
        for g in range(n_groups):
            parts.append(_dot(gm_ref[g], vb[:, g * gw:(g + 1) * gw]) + gb[:, g:g + 1])
        yb_rows.append(ub_ref[rs, :] * jnp.concatenate(parts, axis=1))
    yb = jnp.concatenate(yb_rows, axis=0) if len(yb_rows) > 1 else yb_rows[0]
    gt = gt_ref[...]
    merged = (gt[:, :d] * _dot(ya_ref[...].astype(BF16), woa_ref[...])
              + gt[:, d:2 * d] * _dot(yb.astype(BF16), wob_ref[...])
              + gt[:, 2 * d:] * _dot(yc_ref[...].astype(BF16), woc_ref[...]))
    o_ref[...] = x_ref[...] + _dot(merged.astype(BF16), wo_ref[...])


def _merge(x, ya, ub, vb, yc, gt, gm, gb, woa, wob, woc, wo):
    n, d = x.shape
    tm = min(ROW_TILE, n)

    def rows(w):
        return pl.BlockSpec((tm, w), lambda i: (i, 0))

    kern = functools.partial(_merge_kernel, n_groups=gm.shape[0])
    return pl.pallas_call(
        kern, grid=(n // tm,),
        in_specs=[rows(d), rows(ya.shape[1]), rows(ub.shape[1]), rows(vb.shape[1]), rows(yc.shape[1]),
                  rows(gt.shape[1]), _const_spec(gm.shape), _const_spec(gb.shape),
                  _const_spec(woa.shape), _const_spec(wob.shape), _const_spec(woc.shape),
                  _const_spec(wo.shape)],
        out_specs=rows(d), out_shape=jax.ShapeDtypeStruct((n, d), F32),
        compiler_params=_params("parallel"), name="merge",
    )(x, ya, ub, vb, yc, gt, gm, gb, woa, wob, woc, wo)


def _ffn_kernel(x_ref, n2_ref, wup_ref, wcv_ref, bcv_ref, wdn_ref, buf_ref, nf_ref,
                y_ref, nb_ref, carry_s, ext_s, *, shift, base, final_norm):
    ti = pl.program_id(1)
    tm = x_ref.shape[1]
    ncw = wup_ref.shape[0]
    cw = wdn_ref.shape[1]
    hist = 2 * shift

    @pl.when(ti == 0)
    def _():
        carry_s[...] = buf_ref[0]

    x = x_ref[0]
    h = _rms(x, n2_ref[...]).astype(BF16)
    acc = jnp.zeros(x.shape, F32)
    for c in range(ncw):
        up = _dot(h, wup_ref[c])
        ext_s[base - hist:base, :] = carry_s[c]
        ext_s[base:base + tm, :] = up
        wcv = wcv_ref[c]
        conv = bcv_ref[c] + (wcv[0:1, :] * ext_s[base - hist:base - hist + tm, :]
                             + wcv[1:2, :] * ext_s[base - shift:base - shift + tm, :]
                             + wcv[2:3, :] * up)
        carry_s[c] = up[tm - hist:, :]
        act = _silu(conv[:, :cw]) * conv[:, cw:]
        acc = acc + _dot(act.astype(BF16), wdn_ref[c])
    y = x + acc
    if final_norm:
        y = _rms(y, nf_ref[...])
    y_ref[0] = y
    nb_ref[0] = carry_s[...]


def _ffn(x, n2, wup, wcv, bcv, wdn, buf, nf, shift, final_norm):
    nseq, t, d = x.shape
    tm = min(ROW_TILE, t)
    ncw, _, cw2 = wup.shape
    hist = 2 * shift
    base = -(-hist // SUBLANES) * SUBLANES
    kern = functools.partial(_ffn_kernel, shift=shift, base=base, final_norm=final_norm)
    state = pl.BlockSpec((1, ncw, hist, cw2), lambda i, j: (i, 0, 0, 0))
    return pl.pallas_call(
        kern, grid=(nseq, t // tm),
        in_specs=[pl.BlockSpec((1, tm, d), lambda i, j: (i, j, 0)), _const_spec(n2.shape),
                  _const_spec(wup.shape), _const_spec(wcv.shape), _const_spec(bcv.shape),
                  _const_spec(wdn.shape), state, _const_spec(nf.shape)],
        out_specs=(pl.BlockSpec((1, tm, d), lambda i, j: (i, j, 0)), state),
        out_shape=(jax.ShapeDtypeStruct(x.shape, F32), jax.ShapeDtypeStruct(buf.shape, F32)),
        scratch_shapes=[pltpu.VMEM((ncw, hist, cw2), F32), pltpu.VMEM((base + tm, cw2), F32)],
        compiler_params=_params("parallel", "arbitrary"), name="conv_ffn",
    )(x, n2, wup, wcv, bcv, wdn, buf, nf)


def _ffn_cols_to_chunks(a, d_ff, cw):
    ncw = d_ff // cw
    lead = a.shape[:-1]
    a = a.reshape(lead + (2, ncw, cw))
    a = jnp.moveaxis(a, -2, 0)
    return a.reshape((ncw,) + lead + (2 * cw,))


def _ffn_chunks_to_cols(a, d_ff, cw):
    ncw = d_ff // cw
    lead = a.shape[1:-1]
    a = a.reshape((ncw,) + lead + (2, cw))
    a = jnp.moveaxis(a, 0, -2)
    return a.reshape(lead + (2 * d_ff,))


def kernel(x_prompt, x_sample, cache_k, cache_v, cache_logf, state_hgrn, state_conv, page_table,
           norm1, w_in, b_f, gmlp_norm, w_s, b_s, hgrn_norm, hgrn_lb, w_oa, w_ob, w_oc, w_o,
           norm2, w_up, w_conv, b_conv, w_down, norm_f):
    depth = w_in.shape[0]
    bp, tp, d = x_prompt.shape
    bs, ts, _ = x_sample.shape
    _, n_phys, page, nh_a, hd_a = cache_k.shape
    w_a = nh_a * hd_a
    n_grp, chunk_b, _ = w_s.shape[1:]
    w_b = gmlp_norm.shape[1]
    _, _, nh_c, k_c, v_c = state_hgrn.shape
    w_c = nh_c * v_c
    d_ff = w_down.shape[1]
    cw = FFN_COLS
    ncw = d_ff // cw

    sm = jax.nn.softmax(hgrn_lb.astype(F32), axis=0)
    lower = jnp.cumsum(sm, axis=0) - sm[:1]

    cache_f = cache_logf.reshape(depth, n_phys, page * nh_a // LANES, LANES)

    xp = x_prompt.reshape(bp * tp, d)
    xs = x_sample.reshape(bs * ts, d)
    st_p, st_s = [], []
    for l in range(depth):
        o = 0
        wl = w_in[l]
        wa = wl[:, o:o + 3 * w_a]; o += 3 * w_a
        wf = wl[:, o:o + nh_a]; o += nh_a
        wb = wl[:, o:o + 2 * w_b]; o += 2 * w_b
        wc = wl[:, o:o + 4 * w_c]; o += 4 * w_c
        wg = wl[:, o:]
        qscale = jnp.concatenate([jnp.full((w_a,), hd_a ** -0.5, F32), jnp.ones((2 * w_a,), F32)])
        wa = (wa * qscale).astype(BF16)
        wf_pad = jnp.pad(wf, ((0, 0), (0, LANES - nh_a))).astype(BF16)
        bf_row = jnp.pad(b_f[l], (0, LANES - nh_a)).reshape(1, LANES)
        inproj_w = (norm1[l].reshape(1, d), wa, wf_pad, wb.astype(BF16), wc.astype(BF16),
                    wg.astype(BF16), bf_row, gmlp_norm[l].reshape(1, w_b), lower[l].reshape(1, w_c))

        tril_w = jnp.where(jnp.tril(jnp.ones((chunk_b, chunk_b), bool)), w_s[l], 0)
        woa, wob, woc, wo = (w_oa[l].astype(BF16), w_ob[l].astype(BF16), w_oc[l].astype(BF16),
                             w_o[l].astype(BF16))
        nw = hgrn_norm[l].reshape(1, w_c)

        n2 = norm2[l].reshape(1, d)
        wup = _ffn_cols_to_chunks(w_up[l], d_ff, cw).astype(BF16)
        wcv = _ffn_cols_to_chunks(w_conv[l], d_ff, cw)
        bcv = _ffn_cols_to_chunks(b_conv[l].reshape(1, -1), d_ff, cw)
        wdn = w_down[l].reshape(ncw, cw, d).astype(BF16)
        nf = norm_f.reshape(1, d)
        last = l == depth - 1

        (q, k, v, lf, ub, vb, qs, gl, kf, iv, gcs, gt) = _inproj(xp, *inproj_w)
        ya = _fox_prompt(q.reshape(bp, tp, w_a), k.reshape(bp, tp, w_a), v.reshape(bp, tp, w_a),
                         lf.reshape(bp, tp, LANES), nh_a)
        s0 = jnp.zeros((bp, nh_c, k_c, v_c), F32)
        yc, s_new = _hgrn(*(a.reshape(bp, tp, w_c) for a in (qs, gl, kf, iv, gcs)), nw, s0)
        x1 = _merge(xp, ya.reshape(bp * tp, w_a), ub, vb, yc.reshape(bp * tp, w_c), gt,
                    tril_w.astype(BF16), b_s[l].T, woa, wob, woc, wo)
        buf0 = jnp.zeros((bp, ncw, 2, 2 * cw), F32)
        x2, nb = _ffn(x1.reshape(bp, tp, d), n2, wup, wcv, bcv, wdn, buf0, nf, 1, last)
        xp = x2.reshape(bp * tp, d)
        conv_p = _ffn_chunks_to_cols(jnp.moveaxis(nb, 1, 0), d_ff, cw)
        st_p.append((k.reshape(bp, tp, nh_a, hd_a), v.reshape(bp, tp, nh_a, hd_a),
                     lf[:, :nh_a].reshape(bp, tp, nh_a), s_new, conv_p))

        (q, k, v, lf, ub, vb, qs, gl, kf, iv, gcs, gt) = _inproj(xs, *inproj_w)
        k4 = k.reshape(bs, ts, nh_a, hd_a)
        v4 = v.reshape(bs, ts, nh_a, hd_a)
        lf3 = lf[:, :nh_a].reshape(bs, ts, nh_a)
        ya = _fox_sample(q.astype(F32).reshape(bs, ts, nh_a, hd_a), k4, v4, lf3,
                         cache_k, cache_v, cache_f, page_table, l)
        yc, s_new = _hgrn(*(a.reshape(bs, ts, w_c) for a in (qs, gl, kf, iv, gcs)), nw, state_hgrn[l])
        gm = jnp.einsum("ab,gts->gatbs", jnp.eye(bs, dtype=F32), tril_w[:, :ts, :ts])
        gm = gm.reshape(n_grp, bs * ts, bs * ts).astype(BF16)
        gb = jnp.tile(b_s[l][:, :ts].T, (bs, 1))
        x1 = _merge(xs, ya.reshape(bs * ts, w_a), ub, vb, yc.reshape(bs * ts, w_c), gt, gm, gb,
                    woa, wob, woc, wo)
        x1t = jnp.swapaxes(x1.reshape(bs, ts, d), 0, 1).reshape(1, ts * bs, d)
        buf_t = jnp.swapaxes(state_conv[l], 0, 1).reshape(2 * bs, 2 * d_ff)
        buf_t = _ffn_cols_to_chunks(buf_t, d_ff, cw)[None]
        x2t, nb = _ffn(x1t, n2, wup, wcv, bcv, wdn, buf_t, nf, bs, last)
        xs = jnp.swapaxes(x2t.reshape(ts, bs, d), 0, 1).reshape(bs * ts, d)
        conv_s = _ffn_chunks_to_cols(nb[0], d_ff, cw).reshape(2, bs, 2 * d_ff)
        st_s.append((k4, v4, lf3, vb.reshape(bs, ts, w_b), s_new, jnp.swapaxes(conv_s, 0, 1)))

    def stk(sts, i):
        return jnp.stack([st[i] for st in sts], axis=0)

    return (xp.reshape(bp, tp, d), xs.reshape(bs, ts, d),
            stk(st_p, 0), stk(st_p, 1), stk(st_p, 2), stk(st_p, 3), stk(st_p, 4),
            stk(st_s, 0), stk(st_s, 1), stk(st_s, 2), stk(st_s, 3), stk(st_s, 4), stk(st_s, 5))
```

```python
import functools

import numpy as np

import jax
import jax.numpy as jnp
from jax import lax
from jax.experimental import pallas as pl
from jax.experimental.pallas import tpu as pltpu

F32 = jnp.float32
BF16 = jnp.bfloat16
EPS = 1e-6
NEG_INF = float("-inf")

LANES = 128
SUBLANES = 8
VMEM_LIMIT_BYTES = 56 * 1024 * 1024

ROW_TILE = 256
ATTN_TILE = 256
PAGES_PER_STEP = 8
HGRN_SUB = 16
HGRN_TILE = 256
FFN_COLS = 256
N_SPLIT = 3


def _dot(a, b):
    return jnp.dot(a, b, preferred_element_type=F32)


def _dot_nt(a, b):
    return lax.dot_general(a, b, (((1,), (1,)), ((), ())), preferred_element_type=F32)


def _dot_tn(a, b):
    return lax.dot_general(a, b, (((0,), (0,)), ((), ())), preferred_element_type=F32)


def _rms(x, g):
    return x * lax.rsqrt(jnp.mean(x * x, axis=-1, keepdims=True) + EPS) * g


def _sigmoid(x):
    return 1.0 / (1.0 + jnp.exp(-x))


def _silu(x):
    return x * _sigmoid(x)


def _gelu_tanh(x):
    c = 0.7978845608028654
    return x * (0.5 * (1.0 + jnp.tanh(c * (x + 0.044715 * (x * x * x)))))


def _log_sigmoid(z):
    return jnp.minimum(z, 0.0) - jnp.log1p(jnp.exp(-jnp.abs(z)))


def _split3(x):
    hi = x.astype(BF16)
    r = x - hi.astype(F32)
    mid = r.astype(BF16)
    lo = (r - mid.astype(F32)).astype(BF16)
    return hi, mid, lo


def _ones_tri(n, upper):
    r = lax.broadcasted_iota(jnp.int32, (n, n), 0)
    c = lax.broadcasted_iota(jnp.int32, (n, n), 1)
    keep = (r <= c) if upper else (r >= c)
    return jnp.where(keep, 1.0, 0.0).astype(BF16)


def _const_spec(shape):
    nd = len(shape)
    return pl.BlockSpec(shape, lambda *_: (0,) * nd, pipeline_mode=pl.Buffered(1))


def _params(*sem):
    return pltpu.CompilerParams(dimension_semantics=sem, vmem_limit_bytes=VMEM_LIMIT_BYTES)


def _inproj_kernel(x_ref, n1_ref, wa_ref, wf_ref, wb_ref, wc_ref, wg_ref, bf_ref, gn_ref, lb_ref,
                   q_ref, k_ref, v_ref, lf_ref, ub_ref, vb_ref,
                   qs_ref, gl_ref, kf_ref, iv_ref, gc_ref, gt_ref):
    wa_w = k_ref.shape[-1]
    wb_w = ub_ref.shape[-1]
    wc_w = qs_ref.shape[-1]
    h = _rms(x_ref[...], n1_ref[...]).astype(BF16)

    pa = _dot(h, wa_ref[...])
    q_ref[...] = pa[:, :wa_w].astype(q_ref.dtype)
    k_ref[...] = pa[:, wa_w:2 * wa_w]
    v_ref[...] = pa[:, 2 * wa_w:]

    lf_ref[...] = _log_sigmoid(_dot(h, wf_ref[...]) + bf_ref[...])

    pb = _dot(h, wb_ref[...])
    ub_ref[...] = _gelu_tanh(pb[:, :wb_w])
    vb_ref[...] = _rms(_gelu_tanh(pb[:, wb_w:]), gn_ref[...])

    pc = _dot(h, wc_ref[...])
    lb = lb_ref[...]
    qs_ref[...] = _silu(pc[:, :wc_w])
    f = lb + (1.0 - lb) * _sigmoid(pc[:, wc_w:2 * wc_w])
    gl_ref[...] = jnp.log(f)
    kf_ref[...] = 1.0 - f
    iv_ref[...] = pc[:, 2 * wc_w:3 * wc_w]
    gc_ref[...] = _silu(pc[:, 3 * wc_w:])

    gt_ref[...] = _sigmoid(_dot(h, wg_ref[...]))


def _inproj(x, n1, wa, wf, wb, wc, wg, bf_row, gn, lb):
    n, d = x.shape
    tm = min(ROW_TILE, n)
    wa_w, wb_w, wc_w = wa.shape[1] // 3, wb.shape[1] // 2, wc.shape[1] // 4

    def rows(w):
        return pl.BlockSpec((tm, w), lambda i: (i, 0))

    out_shape = (
        jax.ShapeDtypeStruct((n, wa_w), BF16),
        jax.ShapeDtypeStruct((n, wa_w), F32),
        jax.ShapeDtypeStruct((n, wa_w), F32),
        jax.ShapeDtypeStruct((n, LANES), F32),
        jax.ShapeDtypeStruct((n, wb_w), F32),
        jax.ShapeDtypeStruct((n, wb_w), F32),
        jax.ShapeDtypeStruct((n, wc_w), F32),
        jax.ShapeDtypeStruct((n, wc_w), F32),
        jax.ShapeDtypeStruct((n, wc_w), F32),
        jax.ShapeDtypeStruct((n, wc_w), F32),
        jax.ShapeDtypeStruct((n, wc_w), F32),
        jax.ShapeDtypeStruct((n, wg.shape[1]), F32),
    )
    out_specs = (rows(wa_w), rows(wa_w), rows(wa_w), rows(LANES),
                 rows(wb_w), rows(wb_w), rows(wc_w), rows(wc_w), rows(wc_w), rows(wc_w), rows(wc_w),
                 rows(wg.shape[1]))
    in_specs = [rows(d), _const_spec(n1.shape), _const_spec(wa.shape), _const_spec(wf.shape),
                _const_spec(wb.shape), _const_spec(wc.shape), _const_spec(wg.shape),
                _const_spec(bf_row.shape), _const_spec(gn.shape), _const_spec(lb.shape)]
    return pl.pallas_call(
        _inproj_kernel, grid=(n // tm,), in_specs=in_specs, out_specs=out_specs, out_shape=out_shape,
        compiler_params=_params("parallel"), name="inproj",
    )(x, n1, wa, wf, wb, wc, wg, bf_row, gn, lb)


def _decay_selectors(nh, hd):
    sel_q = np.zeros((N_SPLIT * LANES, nh * hd), np.float32)
    sel_k = np.zeros((N_SPLIT * LANES, nh * hd), np.float32)
    one_q = np.zeros((1, nh * hd), np.float32)
    one_k = np.zeros((1, nh * hd), np.float32)
    for h in range(nh):
        for p in range(N_SPLIT):
            sel_k[p * LANES + h, h * hd + p] = -1.0
            one_k[0, h * hd + N_SPLIT + p] = 1.0
            one_q[0, h * hd + p] = 1.0
            sel_q[p * LANES + h, h * hd + N_SPLIT + p] = 1.0
    return (jnp.asarray(sel_q, BF16), jnp.asarray(sel_k, BF16), jnp.asarray(one_q), jnp.asarray(one_k))


def _fox_prep_kernel(q_ref, k_ref, v_ref, lf_ref, selq_ref, selk_ref, oneq_ref, onek_ref,
                     qa_ref, ka_ref, vt_ref, carry_s):
    ti = pl.program_id(1)
    tm = q_ref.shape[1]
    w = q_ref.shape[2]

    @pl.when(ti == 0)
    def _():
        carry_s[...] = jnp.zeros_like(carry_s)

    lower = _ones_tri(tm, upper=False)
    hi, mid, lo = _split3(lf_ref[0])
    c = _dot(lower, hi) + _dot(lower, mid) + _dot(lower, lo) + carry_s[...]
    carry_s[...] = c[tm - 1:tm, :]
    c3 = jnp.concatenate(_split3(c), axis=1)
    eq = _dot(c3, selq_ref[...]) + oneq_ref[...]
    ek = _dot(c3, selk_ref[...]) + onek_ref[...]
    q = q_ref[0].astype(F32)
    k = k_ref[0]
    qa_parts, ka_parts = [], []
    for p in range(w // LANES):
        ls = slice(p * LANES, (p + 1) * LANES)
        qa_parts += [q[:, ls], eq[:, ls]]
        ka_parts += [k[:, ls], ek[:, ls]]
    qa_ref[0] = jnp.concatenate(qa_parts, axis=1).astype(BF16)
    ka_ref[0] = jnp.concatenate(ka_parts, axis=1).astype(BF16)
    vt_ref[0, 0] = v_ref[0].T.astype(BF16)


def _fox_prompt_kernel(qa_ref, ka_ref, vt_ref, o_ref, qh_s, m_s, l_s, acc_s, *, hd, tq):
    qi = pl.program_id(1)
    heads_per_slab = LANES // hd
    n_slab = vt_ref.shape[2] // LANES
    heads = [(sl, hh) for sl in range(n_slab) for hh in range(heads_per_slab)]
    lane2 = lax.broadcasted_iota(jnp.int32, (tq, 2 * LANES), 1) % LANES
    key = lax.broadcasted_iota(jnp.int32, (tq, tq), 0)
    qry = lax.broadcasted_iota(jnp.int32, (tq, tq), 1)

    for n, (sl, hh) in enumerate(heads):
        qa = qa_ref[0, :, sl * 2 * LANES:(sl + 1) * 2 * LANES].astype(F32)
        qh_s[n] = jnp.where(lane2 // hd == hh, qa, 0.0).astype(BF16)
        m_s[n] = jnp.full((1, tq), NEG_INF, F32)
        l_s[n] = jnp.zeros((1, tq), F32)
        acc_s[n] = jnp.zeros((LANES, tq), F32)

    def tile(j, masked):
        k0 = pl.multiple_of(j * tq, tq)
        s_all = []
        for n, (sl, _) in enumerate(heads):
            s = _dot_nt(ka_ref[0, pl.ds(k0, tq), sl * 2 * LANES:(sl + 1) * 2 * LANES], qh_s[n])
            s_all.append(jnp.where(key <= qry, s, NEG_INF) if masked else s)
        m_old = [m_s[n] for n in range(len(heads))]
        m_new = [jnp.maximum(m, jnp.max(s, axis=0, keepdims=True)) for m, s in zip(m_old, s_all)]
        p_all = [jnp.exp(s - m) for s, m in zip(s_all, m_new)]
        for n, (sl, _) in enumerate(heads):
            alpha = jnp.exp(m_old[n] - m_new[n])
            l_s[n] = alpha * l_s[n] + jnp.sum(p_all[n], axis=0, keepdims=True)
            pv = _dot(vt_ref[0, j, sl * LANES:(sl + 1) * LANES, :], p_all[n].astype(BF16))
            acc_s[n] = alpha * acc_s[n] + pv
            m_s[n] = m_new[n]

    def body(j, carry):
        tile(j, False)
        return carry

    lax.fori_loop(0, qi, body, 0)
    tile(qi, True)
    for sl in range(n_slab):
        parts = [(acc_s[n] / l_s[n])[hh * hd:(hh + 1) * hd, :] for n, (s2, hh) in enumerate(heads) if s2 == sl]
        o_ref[0, :, sl * LANES:(sl + 1) * LANES] = jnp.concatenate(parts, axis=0).T.astype(o_ref.dtype)


def _fox_prompt(q, k, v, lf, nh):
    b, t, w = k.shape
    hd = w // nh
    tq = min(ATTN_TILE, t)
    nkv = t // tq
    sel_q, sel_k, one_q, one_k = _decay_selectors(nh, hd)
    blk = lambda width: pl.BlockSpec((1, tq, width), lambda i, j: (i, j, 0))
    qa, ka, vt = pl.pallas_call(
        _fox_prep_kernel, grid=(b, t // tq),
        in_specs=[blk(w), blk(w), blk(w), blk(LANES), _const_spec(sel_q.shape), _const_spec(sel_k.shape),
                  _const_spec(one_q.shape), _const_spec(one_k.shape)],
        out_specs=(blk(2 * w), blk(2 * w), pl.BlockSpec((1, 1, w, tq), lambda i, j: (i, j, 0, 0))),
        out_shape=(jax.ShapeDtypeStruct((b, t, 2 * w), BF16), jax.ShapeDtypeStruct((b, t, 2 * w), BF16),
                   jax.ShapeDtypeStruct((b, nkv, w, tq), BF16)),
        scratch_shapes=[pltpu.VMEM((1, LANES), F32)],
        compiler_params=_params("parallel", "arbitrary"), name="fox_prep",
    )(q, k, v, lf, sel_q, sel_k, one_q, one_k)
    kern = functools.partial(_fox_prompt_kernel, hd=hd, tq=tq)
    return pl.pallas_call(
        kern, grid=(b, nkv),
        in_specs=[pl.BlockSpec((1, tq, 2 * w), lambda i, j: (i, j, 0)),
                  pl.BlockSpec((1, t, 2 * w), lambda i, j: (i, 0, 0)),
                  pl.BlockSpec((1, nkv, w, tq), lambda i, j: (i, 0, 0, 0))],
        out_specs=pl.BlockSpec((1, tq, w), lambda i, j: (i, j, 0)),
        out_shape=jax.ShapeDtypeStruct((b, t, w), BF16),
        scratch_shapes=[pltpu.VMEM((nh, tq, 2 * LANES), BF16), pltpu.VMEM((nh, 1, tq), F32),
                        pltpu.VMEM((nh, 1, tq), F32), pltpu.VMEM((nh, LANES, tq), F32)],
        compiler_params=_params("parallel", "arbitrary"), name="fox_prompt",
    )(qa, ka, vt)


def _fox_sample_kernel(pt_ref, q_ref, kn_ref, vn_ref, lnt_ref, *rest, nh, hd, g_pages):
    k_pages = rest[:g_pages]
    v_pages = rest[g_pages:2 * g_pages]
    f_pages = rest[2 * g_pages:3 * g_pages]
    o_ref, qbd_s, m_s, l_s, acc_s, carry_s, cn_s = rest[3 * g_pages:]
    del pt_ref
    step = pl.program_id(1)
    nt = q_ref.shape[0]
    w = q_ref.shape[1]
    page = k_pages[0].shape[-1]
    rows = nt * nh

    def update(s3, vs):
        m_old = m_s[...]
        m_new = jnp.maximum(m_old, jnp.max(s3, axis=-1, keepdims=True))
        alpha = jnp.exp(m_old - m_new)
        p3 = jnp.exp(s3 - m_new)
        l_s[...] = alpha * l_s[...] + jnp.sum(p3, axis=-1, keepdims=True)
        p = p3.reshape(rows, p3.shape[-1]).astype(BF16)
        pv = None
        off = 0
        for vt in vs:
            part = _dot_nt(p[:, off:off + vt.shape[1]], vt)
            pv = part if pv is None else pv + part
            off += vt.shape[1]
        acc_s[...] = alpha * acc_s[...] + pv.reshape(nt, nh, w)
        m_s[...] = m_new

    @pl.when(step == 0)
    def _():
        rep_r = lax.broadcasted_iota(jnp.int32, (rows, nt), 0) // nh
        rep_c = lax.broadcasted_iota(jnp.int32, (rows, nt), 1)
        q_rep = _dot(jnp.where(rep_r == rep_c, 1.0, 0.0), q_ref[...])
        head_of_lane = lax.broadcasted_iota(jnp.int32, (rows, w), 1) // hd
        head_of_row = lax.broadcasted_iota(jnp.int32, (rows, w), 0) % nh
        qbd = jnp.where(head_of_lane == head_of_row, q_rep, 0.0).astype(BF16)
        qbd_s[...] = qbd

        lane = lax.broadcasted_iota(jnp.int32, (nh, page), 1)
        cn = jnp.concatenate([lnt_ref[...], jnp.zeros((nh, page - nt), F32)], axis=1)
        sh = 1
        while sh < nt:
            cn = cn + jnp.where(lane >= sh, pltpu.roll(cn, sh, 1), 0.0)
            sh *= 2
        cn_s[...] = cn
        carry_s[...] = jnp.zeros_like(carry_s)
        m_s[...] = jnp.full(m_s.shape, NEG_INF, F32)
        l_s[...] = jnp.zeros_like(l_s)
        acc_s[...] = jnp.zeros_like(acc_s)

        kn = jnp.concatenate([kn_ref[...], jnp.zeros((page - nt, w), F32)], axis=0).astype(BF16)
        vn = jnp.concatenate([vn_ref[...], jnp.zeros((page - nt, w), F32)], axis=0).T.astype(BF16)
        s3 = _dot_nt(qbd, kn).reshape(nt, nh, page)
        cq = jnp.stack([cn[:, t:t + 1] for t in range(nt)], axis=0)
        s3 = s3 + (cq - cn[None, :, :])
        tk = lax.broadcasted_iota(jnp.int32, (nt, nh, page), 2)
        tq = lax.broadcasted_iota(jnp.int32, (nt, nh, page), 0)
        update(jnp.where(tk <= tq, s3, NEG_INF), [vn])

    lane = lax.broadcasted_iota(jnp.int32, (nh, page), 1)
    cn = cn_s[...]
    cq = jnp.stack([cn[:, t:t + 1] for t in range(nt)], axis=0)
    qbd = qbd_s[...]
    carry = carry_s[...]
    s_tiles, v_tiles = [], []
    for g in range(g_pages):
        x = f_pages[g][...]
        incl = x
        sh = 1
        while sh < page:
            incl = incl + jnp.where(lane + sh < page, pltpu.roll(incl, page - sh, 1), 0.0)
            sh *= 2
        r = carry + (incl - x)
        carry = carry + incl[:, 0:1]
        s3 = _dot(qbd, k_pages[g][...].reshape(w, page).astype(BF16)).reshape(nt, nh, page)
        s_tiles.append(s3 + (cq + r[None, :, :]))
        v_tiles.append(v_pages[g][...].reshape(w, page).astype(BF16))
    carry_s[...] = carry
    update(jnp.concatenate(s_tiles, axis=-1), v_tiles)

    @pl.when(step == pl.num_programs(1) - 1)
    def _():
        head_of_lane = lax.broadcasted_iota(jnp.int32, (nt, nh, w), 2) // hd
        head_of_row = lax.broadcasted_iota(jnp.int32, (nt, nh, w), 1)
        o3 = jnp.where(head_of_lane == head_of_row, acc_s[...] / l_s[...], 0.0)
        o2 = o3.reshape(rows, w).astype(BF16)
        sel_r = lax.broadcasted_iota(jnp.int32, (nt, rows), 0)
        sel_c = lax.broadcasted_iota(jnp.int32, (nt, rows), 1) // nh
        sel = jnp.where(sel_r == sel_c, 1.0, 0.0).astype(BF16)
        o_ref[...] = _dot(sel, o2).astype(o_ref.dtype)


def _fox_sample(q, kn, vn, lnt, cache_kt, cache_vt, cache_ft, page_table, layer):
    nseq, n_pages = page_table.shape
    n, w = q.shape
    nt = n // nseq
    _, _, nh, hd, page = cache_kt.shape
    g_pages = min(PAGES_PER_STEP, n_pages)
    n_steps = n_pages // g_pages

    def page_spec(shape, g):
        nd = len(shape)

        def idx(b, s, pt):
            return (layer, pt[b * n_pages + (n_pages - 1 - (s * g_pages + g))]) + (0,) * nd
        return pl.BlockSpec((None, None) + shape, idx)

    seq_rows = pl.BlockSpec((nt, w), lambda b, s, pt: (b, 0))
    in_specs = [seq_rows, seq_rows, seq_rows, pl.BlockSpec((None, nh, nt), lambda b, s, pt: (b, 0, 0))]
    in_specs += [page_spec((nh, hd, page), g) for g in range(g_pages)]
    in_specs += [page_spec((nh, hd, page), g) for g in range(g_pages)]
    in_specs += [page_spec((nh, page), g) for g in range(g_pages)]
    kern = functools.partial(_fox_sample_kernel, nh=nh, hd=hd, g_pages=g_pages)
    grid_spec = pltpu.PrefetchScalarGridSpec(
        num_scalar_prefetch=1, grid=(nseq, n_steps), in_specs=in_specs,
        out_specs=pl.BlockSpec((nt, w), lambda b, s, pt: (b, 0)),
        scratch_shapes=[pltpu.VMEM((nt * nh, w), BF16),
                        pltpu.VMEM((nt, nh, 1), F32), pltpu.VMEM((nt, nh, 1), F32),
                        pltpu.VMEM((nt, nh, w), F32),
                        pltpu.VMEM((nh, page), F32), pltpu.VMEM((nh, page), F32)])
    return pl.pallas_call(
        kern, grid_spec=grid_spec, out_shape=jax.ShapeDtypeStruct((n, w), F32),
        compiler_params=_params("parallel", "arbitrary"), name="fox_sample",
    )(page_table.reshape(-1), q, kn, vn, lnt,
      *([cache_kt] * g_pages), *([cache_vt] * g_pages), *([cache_ft] * g_pages))


def _hgrn_kernel(q_ref, g_ref, kf_ref, v_ref, gc_ref, nw_ref, s0_ref, y_ref, sn_ref, st_s, *, c, nsub, kd):
    ti = pl.program_id(1)
    nh = s0_ref.shape[1]

    @pl.when(ti == 0)
    def _():
        for h in range(nh):
            st_s[h] = s0_ref[0, h].T

    rowi = lax.broadcasted_iota(jnp.int32, (c, 1), 0)

    def head_step(rs, h):
        hs = slice(h * kd, (h + 1) * kd)
        q = q_ref[0, rs, hs]
        g = g_ref[0, rs, hs]
        kf = kf_ref[0, rs, hs]
        v = v_ref[0, rs, hs]
        st = st_s[h]

        b = jnp.zeros_like(g)
        b_rows = []
        run = None
        for s in range(c):
            gs = g[s:s + 1, :]
            b = b + jnp.where(rowi >= s, gs, 0.0)
            run = gs if run is None else run + gs
            b_rows.append(run)
        bl = b_rows[-1]

        o = _dot_nt((q * jnp.exp(b)).astype(BF16), st.astype(BF16))
        for s in range(c):
            xs = q * jnp.exp(jnp.minimum(b - b_rows[s], 0.0)) * kf[s:s + 1, :]
            a = jnp.sum(xs, axis=-1, keepdims=True)
            o = o + jnp.where(rowi >= s, a, 0.0) * v[s:s + 1, :]

        kd_ = kf * jnp.exp(bl - b)
        st_s[h] = st * jnp.exp(bl) + _dot_tn(v.astype(BF16), kd_.astype(BF16))

        y = _rms(o, nw_ref[:, hs]) * gc_ref[0, rs, hs]
        y_ref[0, rs, hs] = y.astype(y_ref.dtype)

    def body(i, carry):
        rs = pl.ds(0 if nsub == 1 else pl.multiple_of(i * c, c), c)
        for h in range(nh):
            head_step(rs, h)
        return carry

    if nsub == 1:
        body(0, 0)
    else:
        lax.fori_loop(0, nsub, body, 0)

    @pl.when(ti == pl.num_programs(1) - 1)
    def _():
        for h in range(nh):
            sn_ref[0, h] = st_s[h].T


def _hgrn(qs, gl, kf, iv, gcs, nw, s0):
    b, t, w = qs.shape
    _, nh, kd, vd = s0.shape
    c = HGRN_SUB if t % HGRN_SUB == 0 else t
    tt = HGRN_TILE if t % HGRN_TILE == 0 else t
    kern = functools.partial(_hgrn_kernel, c=c, nsub=tt // c, kd=kd)
    seq = pl.BlockSpec((1, tt, w), lambda i, j: (i, j, 0))
    state = pl.BlockSpec((1, nh, kd, vd), lambda i, j: (i, 0, 0, 0))
    return pl.pallas_call(
        kern, grid=(b, t // tt),
        in_specs=[seq, seq, seq, seq, seq, _const_spec(nw.shape), state],
        out_specs=(seq, state),
        out_shape=(jax.ShapeDtypeStruct((b, t, nh * vd), BF16), jax.ShapeDtypeStruct(s0.shape, F32)),
        scratch_shapes=[pltpu.VMEM((nh, vd, kd), F32)],
        compiler_params=_params("parallel", "arbitrary"), name="hgrn",
    )(qs, gl, kf, iv, gcs, nw, s0)


def _merge_kernel(x_ref, ya_ref, ub_ref, vb_ref, yc_ref, gt_ref, gm_ref, gb_ref,
                  woa_ref, wob_ref, woc_ref, wo_ref, o_ref, *, n_groups):
    tm = x_ref.shape[0]
    d = x_ref.shape[1]
    cn = gm_ref.shape[1]
    gw = ub_ref.shape[1] // n_groups
    yb_rows = []
    for ci in range(tm // cn):
        rs = slice(ci * cn, (ci + 1) * cn)
        vb = vb_ref[rs, :].astype(BF16)
        gb = gb_ref[...]
        parts = []
        for g in range(n_groups):
            parts.append(_dot(gm_ref[g], vb[:, g * gw:(g + 1) * gw]) + gb[:, g:g + 1])
        yb_rows.append(ub_ref[rs, :] * jnp.concatenate(parts, axis=1))
    yb = jnp.concatenate(yb_rows, axis=0) if len(yb_rows) > 1 else yb_rows[0]
    gt = gt_ref[...]
    merged = (gt[:, :d] * _dot(ya_ref[...].astype(BF16), woa_ref[...])
              + gt[:, d:2 * d] * _dot(yb.astype(BF16), wob_ref[...])
              + gt[:, 2 * d:] * _dot(yc_ref[...].astype(BF16), woc_ref[...]))
    o_ref[...] = x_ref[...] + _dot(merged.astype(BF16), wo_ref[...])


def _merge(x, ya, ub, vb, yc, gt, gm, gb, woa, wob, woc, wo):
    n, d = x.shape
    tm = min(ROW_TILE, n)

    def rows(w):
        return pl.BlockSpec((tm, w), lambda i: (i, 0))

    kern = functools.partial(_merge_kernel, n_groups=gm.shape[0])
    return pl.pallas_call(
        kern, grid=(n // tm,),
        in_specs=[rows(d), rows(ya.shape[1]), rows(ub.shape[1]), rows(vb.shape[1]), rows(yc.shape[1]),
                  rows(gt.shape[1]), _const_spec(gm.shape), _const_spec(gb.shape),
                  _const_spec(woa.shape), _const_spec(wob.shape), _const_spec(woc.shape),
                  _const_spec(wo.shape)],
        out_specs=rows(d), out_shape=jax.ShapeDtypeStruct((n, d), F32),
        compiler_params=_params("parallel"), name="merge",
    )(x, ya, ub, vb, yc, gt, gm, gb, woa, wob, woc, wo)


def _ffn_kernel(x_ref, n2_ref, wup_ref, wcv_ref, bcv_ref, wdn_ref, buf_ref, nf_ref,
                y_ref, nb_ref, carry_s, ext_s, *, shift, base, final_norm):
    ti = pl.program_id(1)
    tm = x_ref.shape[1]
    ncw = wup_ref.shape[0]
    cw = wdn_ref.shape[1]
    hist = 2 * shift

    @pl.when(ti == 0)
    def _():
        carry_s[...] = buf_ref[0]

    x = x_ref[0]
    h = _rms(x, n2_ref[...]).astype(BF16)
    acc = jnp.zeros(x.shape, F32)
    for c in range(ncw):
        up = _dot(h, wup_ref[c])
        ext_s[base - hist:base, :] = carry_s[c]
        ext_s[base:base + tm, :] = up
        wcv = wcv_ref[c]
        conv = bcv_ref[c] + (wcv[0:1, :] * ext_s[base - hist:base - hist + tm, :]
                             + wcv[1:2, :] * ext_s[base - shift:base - shift + tm, :]
                             + wcv[2:3, :] * up)
        carry_s[c] = up[tm - hist:, :]
        act = _silu(conv[:, :cw]) * conv[:, cw:]
        acc = acc + _dot(act.astype(BF16), wdn_ref[c])
    y = x + acc
    if final_norm:
        y = _rms(y, nf_ref[...])
    y_ref[0] = y
    nb_ref[0] = carry_s[...]


def _ffn(x, n2, wup, wcv, bcv, wdn, buf, nf, shift, final_norm):
    nseq, t, d = x.shape
    tm = min(ROW_TILE, t)
    ncw, _, cw2 = wup.shape
    hist = 2 * shift
    base = -(-hist // SUBLANES) * SUBLANES
    kern = functools.partial(_ffn_kernel, shift=shift, base=base, final_norm=final_norm)
    state = pl.BlockSpec((1, ncw, hist, cw2), lambda i, j: (i, 0, 0, 0))
    return pl.pallas_call(
        kern, grid=(nseq, t // tm),
        in_specs=[pl.BlockSpec((1, tm, d), lambda i, j: (i, j, 0)), _const_spec(n2.shape),
                  _const_spec(wup.shape), _const_spec(wcv.shape), _const_spec(bcv.shape),
                  _const_spec(wdn.shape), state, _const_spec(nf.shape)],
        out_specs=(pl.BlockSpec((1, tm, d), lambda i, j: (i, j, 0)), state),
        out_shape=(jax.ShapeDtypeStruct(x.shape, F32), jax.ShapeDtypeStruct(buf.shape, F32)),
        scratch_shapes=[pltpu.VMEM((ncw, hist, cw2), F32), pltpu.VMEM((base + tm, cw2), F32)],
        compiler_params=_params("parallel", "arbitrary"), name="conv_ffn",
    )(x, n2, wup, wcv, bcv, wdn, buf, nf)


def _ffn_cols_to_chunks(a, d_ff, cw):
    ncw = d_ff // cw
    lead = a.shape[:-1]
    a = a.reshape(lead + (2, ncw, cw))
    a = jnp.moveaxis(a, -2, 0)
    return a.reshape((ncw,) + lead + (2 * cw,))


def _ffn_chunks_to_cols(a, d_ff, cw):
    ncw = d_ff // cw
    lead = a.shape[1:-1]
    a = a.reshape((ncw,) + lead + (2, cw))
    a = jnp.moveaxis(a, 0, -2)
    return a.reshape(lead + (2 * d_ff,))


def kernel(x_prompt, x_sample, cache_k, cache_v, cache_logf, state_hgrn, state_conv, page_table,
           norm1, w_in, b_f, gmlp_norm, w_s, b_s, hgrn_norm, hgrn_lb, w_oa, w_ob, w_oc, w_o,
           norm2, w_up, w_conv, b_conv, w_down, norm_f):
    depth = w_in.shape[0]
    bp, tp, d = x_prompt.shape
    bs, ts, _ = x_sample.shape
    _, n_phys, page, nh_a, hd_a = cache_k.shape
    w_a = nh_a * hd_a
    n_grp, chunk_b, _ = w_s.shape[1:]
    w_b = gmlp_norm.shape[1]
    _, _, nh_c, k_c, v_c = state_hgrn.shape
    w_c = nh_c * v_c
    d_ff = w_down.shape[1]
    cw = FFN_COLS
    ncw = d_ff // cw

    sm = jax.nn.softmax(hgrn_lb.astype(F32), axis=0)
    lower = jnp.cumsum(sm, axis=0) - sm[:1]

    cache_kt = jnp.transpose(cache_k, (0, 1, 3, 4, 2))
    cache_vt = jnp.transpose(cache_v, (0, 1, 3, 4, 2))
    cache_ft = jnp.swapaxes(cache_logf, 2, 3)

    xp = x_prompt.reshape(bp * tp, d)
    xs = x_sample.reshape(bs * ts, d)
    st_p, st_s = [], []
    for l in range(depth):
        o = 0
        wl = w_in[l]
        wa = wl[:, o:o + 3 * w_a]; o += 3 * w_a
        wf = wl[:, o:o + nh_a]; o += nh_a
        wb = wl[:, o:o + 2 * w_b]; o += 2 * w_b
        wc = wl[:, o:o + 4 * w_c]; o += 4 * w_c
        wg = wl[:, o:]
        qscale = jnp.concatenate([jnp.full((w_a,), hd_a ** -0.5, F32), jnp.ones((2 * w_a,), F32)])
        wa = (wa * qscale).astype(BF16)
        wf_pad = jnp.pad(wf, ((0, 0), (0, LANES - nh_a))).astype(BF16)
        bf_row = jnp.pad(b_f[l], (0, LANES - nh_a)).reshape(1, LANES)
        inproj_w = (norm1[l].reshape(1, d), wa, wf_pad, wb.astype(BF16), wc.astype(BF16),
                    wg.astype(BF16), bf_row, gmlp_norm[l].reshape(1, w_b), lower[l].reshape(1, w_c))

        tril_w = jnp.where(jnp.tril(jnp.ones((chunk_b, chunk_b), bool)), w_s[l], 0)
        woa, wob, woc, wo = (w_oa[l].astype(BF16), w_ob[l].astype(BF16), w_oc[l].astype(BF16),
                             w_o[l].astype(BF16))
        nw = hgrn_norm[l].reshape(1, w_c)

        n2 = norm2[l].reshape(1, d)
        wup = _ffn_cols_to_chunks(w_up[l], d_ff, cw).astype(BF16)
        wcv = _ffn_cols_to_chunks(w_conv[l], d_ff, cw)
        bcv = _ffn_cols_to_chunks(b_conv[l].reshape(1, -1), d_ff, cw)
        wdn = w_down[l].reshape(ncw, cw, d).astype(BF16)
        nf = norm_f.reshape(1, d)
        last = l == depth - 1

        (q, k, v, lf, ub, vb, qs, gl, kf, iv, gcs, gt) = _inproj(xp, *inproj_w)
        ya = _fox_prompt(q.reshape(bp, tp, w_a), k.reshape(bp, tp, w_a), v.reshape(bp, tp, w_a),
                         lf.reshape(bp, tp, LANES), nh_a)
        s0 = jnp.zeros((bp, nh_c, k_c, v_c), F32)
        yc, s_new = _hgrn(*(a.reshape(bp, tp, w_c) for a in (qs, gl, kf, iv, gcs)), nw, s0)
        x1 = _merge(xp, ya.reshape(bp * tp, w_a), ub, vb, yc.reshape(bp * tp, w_c), gt,
                    tril_w.astype(BF16), b_s[l].T, woa, wob, woc, wo)
        buf0 = jnp.zeros((bp, ncw, 2, 2 * cw), F32)
        x2, nb = _ffn(x1.reshape(bp, tp, d), n2, wup, wcv, bcv, wdn, buf0, nf, 1, last)
        xp = x2.reshape(bp * tp, d)
        conv_p = _ffn_chunks_to_cols(jnp.moveaxis(nb, 1, 0), d_ff, cw)
        st_p.append((k.reshape(bp, tp, nh_a, hd_a), v.reshape(bp, tp, nh_a, hd_a),
                     lf[:, :nh_a].reshape(bp, tp, nh_a), s_new, conv_p))

        (q, k, v, lf, ub, vb, qs, gl, kf, iv, gcs, gt) = _inproj(xs, *inproj_w)
        k4 = k.reshape(bs, ts, nh_a, hd_a)
        v4 = v.reshape(bs, ts, nh_a, hd_a)
        lf3 = lf[:, :nh_a].reshape(bs, ts, nh_a)
        ya = _fox_sample(q.astype(F32), k, v, jnp.swapaxes(lf3, 1, 2),
                         cache_kt, cache_vt, cache_ft, page_table, l)
        yc, s_new = _hgrn(*(a.reshape(bs, ts, w_c) for a in (qs, gl, kf, iv, gcs)), nw, state_hgrn[l])
        gm = jnp.einsum("ab,gts->gatbs", jnp.eye(bs, dtype=F32), tril_w[:, :ts, :ts])
        gm = gm.reshape(n_grp, bs * ts, bs * ts).astype(BF16)
        gb = jnp.tile(b_s[l][:, :ts].T, (bs, 1))
        x1 = _merge(xs, ya, ub, vb, yc.reshape(bs * ts, w_c), gt, gm, gb, woa, wob, woc, wo)
        x1t = jnp.swapaxes(x1.reshape(bs, ts, d), 0, 1).reshape(1, ts * bs, d)
        buf_t = jnp.swapaxes(state_conv[l], 0, 1).reshape(2 * bs, 2 * d_ff)
        buf_t = _ffn_cols_to_chunks(buf_t, d_ff, cw)[None]
        x2t, nb = _ffn(x1t, n2, wup, wcv, bcv, wdn, buf_t, nf, bs, last)
        xs = jnp.swapaxes(x2t.reshape(ts, bs, d), 0, 1).reshape(bs * ts, d)
        conv_s = _ffn_chunks_to_cols(nb[0], d_ff, cw).reshape(2, bs, 2 * d_ff)
        st_s.append((k4, v4, lf3, vb.reshape(bs, ts, w_b), s_new, jnp.swapaxes(conv_s, 0, 1)))

    def stk(sts, i):
        return jnp.stack([st[i] for st in sts], axis=0)

    return (xp.reshape(bp, tp, d), xs.reshape(bs, ts, d),
            stk(st_p, 0), stk(st_p, 1), stk(st_p, 2), stk(st_p, 3), stk(st_p, 4),
            stk(st_s, 0), stk(st_s, 1), stk(st_s, 2), stk(st_s, 3), stk(st_s, 4), stk(st_s, 5))
```

```python
import functools

import numpy as np

import jax
import jax.numpy as jnp
from jax import lax
from jax.experimental import pallas as pl
from jax.experimental.pallas import tpu as pltpu

F32 = jnp.float32
BF16 = jnp.bfloat16
EPS = 1e-6
NEG_INF = float("-inf")
LOG2E = 1.4426950408889634

LANES = 128
SUBLANES = 8
VMEM_LIMIT_BYTES = 56 * 1024 * 1024

ROW_TILE = 256
ATTN_TILE = 256
PAGES_PER_STEP = 8
HGRN_SUB = 16
HGRN_TILE = 256
FFN_COLS = 256
FFN_EXT_SLOTS = 4
FFN_DOWN_CHUNKS = 4
N_SPLIT = 3


def _dot(a, b):
    return jnp.dot(a, b, preferred_element_type=F32)


def _dot_nt(a, b):
    return lax.dot_general(a, b, (((1,), (1,)), ((), ())), preferred_element_type=F32)


def _dot_tn(a, b):
    return lax.dot_general(a, b, (((0,), (0,)), ((), ())), preferred_element_type=F32)


def _rms(x, g):
    return x * lax.rsqrt(jnp.mean(x * x, axis=-1, keepdims=True) + EPS) * g


def _sigmoid(x):
    return 1.0 / (1.0 + jnp.exp(-x))


def _silu(x):
    return x * _sigmoid(x)


def _gelu_tanh(x):
    c = 0.7978845608028654
    return x * (0.5 * (1.0 + jnp.tanh(c * (x + 0.044715 * (x * x * x)))))


def _log_sigmoid(z):
    return jnp.minimum(z, 0.0) - jnp.log1p(jnp.exp(-jnp.abs(z)))


def _split3(x):
    hi = x.astype(BF16)
    r = x - hi.astype(F32)
    mid = r.astype(BF16)
    lo = (r - mid.astype(F32)).astype(BF16)
    return hi, mid, lo


def _ones_tri(n, upper):
    r = lax.broadcasted_iota(jnp.int32, (n, n), 0)
    c = lax.broadcasted_iota(jnp.int32, (n, n), 1)
    keep = (r <= c) if upper else (r >= c)
    return jnp.where(keep, 1.0, 0.0).astype(BF16)


def _const_spec(shape):
    nd = len(shape)
    return pl.BlockSpec(shape, lambda *_: (0,) * nd, pipeline_mode=pl.Buffered(1))


def _params(*sem):
    return pltpu.CompilerParams(dimension_semantics=sem, vmem_limit_bytes=VMEM_LIMIT_BYTES)


def _inproj_kernel(x_ref, n1_ref, wa_ref, wf_ref, wb_ref, wc_ref, wg_ref, bf_ref, gn_ref, lb_ref,
                   q_ref, k_ref, v_ref, lf_ref, ub_ref, vb_ref,
                   qs_ref, gl_ref, kf_ref, iv_ref, gc_ref, gt_ref):
    wa_w = k_ref.shape[-1]
    wb_w = ub_ref.shape[-1]
    wc_w = qs_ref.shape[-1]
    h = _rms(x_ref[...], n1_ref[...]).astype(BF16)

    pa = _dot_nt(h, wa_ref[...])
    q_ref[...] = pa[:, :wa_w].astype(q_ref.dtype)
    k_ref[...] = pa[:, wa_w:2 * wa_w]
    v_ref[...] = pa[:, 2 * wa_w:]

    lf_ref[...] = _log_sigmoid(_dot_nt(h, wf_ref[...]) + bf_ref[...])

    pb = _dot_nt(h, wb_ref[...])
    ub_ref[...] = _gelu_tanh(pb[:, :wb_w])
    vb_ref[...] = _rms(_gelu_tanh(pb[:, wb_w:]), gn_ref[...])

    pc = _dot_nt(h, wc_ref[...])
    lb = lb_ref[...]
    qs_ref[...] = _silu(pc[:, :wc_w])
    f = lb + (1.0 - lb) * _sigmoid(pc[:, wc_w:2 * wc_w])
    gl_ref[...] = jnp.log(f)
    kf_ref[...] = 1.0 - f
    iv_ref[...] = pc[:, 2 * wc_w:3 * wc_w]
    gc_ref[...] = _silu(pc[:, 3 * wc_w:])

    gt_ref[...] = _sigmoid(_dot_nt(h, wg_ref[...]))


def _inproj(x, n1, wa, wf, wb, wc, wg, bf_row, gn, lb):
    n, d = x.shape
    tm = min(ROW_TILE, n)
    wa_w, wb_w, wc_w = wa.shape[0] // 3, wb.shape[0] // 2, wc.shape[0] // 4

    def rows(w):
        return pl.BlockSpec((tm, w), lambda i: (i, 0))

    out_shape = (
        jax.ShapeDtypeStruct((n, wa_w), BF16),
        jax.ShapeDtypeStruct((n, wa_w), F32),
        jax.ShapeDtypeStruct((n, wa_w), F32),
        jax.ShapeDtypeStruct((n, LANES), F32),
        jax.ShapeDtypeStruct((n, wb_w), F32),
        jax.ShapeDtypeStruct((n, wb_w), F32),
        jax.ShapeDtypeStruct((n, wc_w), F32),
        jax.ShapeDtypeStruct((n, wc_w), F32),
        jax.ShapeDtypeStruct((n, wc_w), F32),
        jax.ShapeDtypeStruct((n, wc_w), F32),
        jax.ShapeDtypeStruct((n, wc_w), F32),
        jax.ShapeDtypeStruct((n, wg.shape[0]), F32),
    )
    out_specs = (rows(wa_w), rows(wa_w), rows(wa_w), rows(LANES),
                 rows(wb_w), rows(wb_w), rows(wc_w), rows(wc_w), rows(wc_w), rows(wc_w), rows(wc_w),
                 rows(wg.shape[0]))
    in_specs = [rows(d), _const_spec(n1.shape), _const_spec(wa.shape), _const_spec(wf.shape),
                _const_spec(wb.shape), _const_spec(wc.shape), _const_spec(wg.shape),
                _const_spec(bf_row.shape), _const_spec(gn.shape), _const_spec(lb.shape)]
    return pl.pallas_call(
        _inproj_kernel, grid=(n // tm,), in_specs=in_specs, out_specs=out_specs, out_shape=out_shape,
        compiler_params=_params("parallel"), name="inproj",
    )(x, n1, wa, wf, wb, wc, wg, bf_row, gn, lb)


def _decay_selectors(nh, hd):
    sel_q = np.zeros((N_SPLIT * LANES, nh * hd), np.float32)
    sel_k = np.zeros((N_SPLIT * LANES, nh * hd), np.float32)
    one_q = np.zeros((1, nh * hd), np.float32)
    one_k = np.zeros((1, nh * hd), np.float32)
    for h in range(nh):
        for p in range(N_SPLIT):
            sel_k[p * LANES + h, h * hd + p] = -1.0
            one_k[0, h * hd + N_SPLIT + p] = 1.0
            one_q[0, h * hd + p] = 1.0
            sel_q[p * LANES + h, h * hd + N_SPLIT + p] = 1.0
    return (jnp.asarray(sel_q, BF16), jnp.asarray(sel_k, BF16), jnp.asarray(one_q), jnp.asarray(one_k))


def _fox_prep_kernel(q_ref, k_ref, v_ref, lf_ref, selq_ref, selk_ref, oneq_ref, onek_ref,
                     qa_ref, ka_ref, vt_ref, carry_s):
    ti = pl.program_id(1)
    tm = q_ref.shape[1]
    w = q_ref.shape[2]

    @pl.when(ti == 0)
    def _():
        carry_s[...] = jnp.zeros_like(carry_s)

    lower = _ones_tri(tm, upper=False)
    hi, mid, lo = _split3(lf_ref[0])
    c = _dot(lower, hi) + _dot(lower, mid) + _dot(lower, lo) + carry_s[...]
    carry_s[...] = c[tm - 1:tm, :]
    c3 = jnp.concatenate(_split3(c * LOG2E), axis=1)
    eq = _dot(c3, selq_ref[...]) + oneq_ref[...]
    ek = _dot(c3, selk_ref[...]) + onek_ref[...]
    q = q_ref[0].astype(F32)
    k = k_ref[0]
    qa_parts, ka_parts = [], []
    for p in range(w // LANES):
        ls = slice(p * LANES, (p + 1) * LANES)
        qa_parts += [q[:, ls], eq[:, ls]]
        ka_parts += [k[:, ls], ek[:, ls]]
    qa_ref[0] = jnp.concatenate(qa_parts, axis=1).astype(BF16)
    ka_ref[0] = jnp.concatenate(ka_parts, axis=1).astype(BF16)
    vt_ref[0, 0] = v_ref[0].T.astype(BF16)


def _fox_prompt_kernel(qa_ref, ka_ref, vt_ref, o_ref, qh_s, m_s, l_s, acc_s, *, hd, tq):
    qi = pl.program_id(1)
    heads_per_slab = LANES // hd
    n_slab = vt_ref.shape[2] // LANES
    heads = [(sl, hh) for sl in range(n_slab) for hh in range(heads_per_slab)]
    lane2 = lax.broadcasted_iota(jnp.int32, (tq, 2 * LANES), 1) % LANES
    key = lax.broadcasted_iota(jnp.int32, (tq, tq), 0)
    qry = lax.broadcasted_iota(jnp.int32, (tq, tq), 1)

    for n, (sl, hh) in enumerate(heads):
        qa = qa_ref[0, :, sl * 2 * LANES:(sl + 1) * 2 * LANES].astype(F32)
        qh_s[n] = jnp.where(lane2 // hd == hh, qa, 0.0).astype(BF16)
        m_s[n] = jnp.full((1, tq), NEG_INF, F32)
        l_s[n] = jnp.zeros((1, tq), F32)
        acc_s[n] = jnp.zeros((LANES, tq), F32)

    def tile(j, masked):
        k0 = pl.multiple_of(j * tq, tq)
        s_all = []
        for n, (sl, _) in enumerate(heads):
            s = _dot_nt(ka_ref[0, pl.ds(k0, tq), sl * 2 * LANES:(sl + 1) * 2 * LANES], qh_s[n])
            s_all.append(jnp.where(key <= qry, s, NEG_INF) if masked else s)
        m_old = [m_s[n] for n in range(len(heads))]
        m_new = [jnp.maximum(m, jnp.max(s, axis=0, keepdims=True)) for m, s in zip(m_old, s_all)]
        p_all = [jnp.exp2(s - m) for s, m in zip(s_all, m_new)]
        for n, (sl, _) in enumerate(heads):
            alpha = jnp.exp2(m_old[n] - m_new[n])
            l_s[n] = alpha * l_s[n] + jnp.sum(p_all[n], axis=0, keepdims=True)
            pv = _dot(vt_ref[0, j, sl * LANES:(sl + 1) * LANES, :], p_all[n].astype(BF16))
            acc_s[n] = alpha * acc_s[n] + pv
            m_s[n] = m_new[n]

    def body(j, carry):
        tile(j, False)
        return carry

    lax.fori_loop(0, qi, body, 0)
    tile(qi, True)
    for sl in range(n_slab):
        parts = [(acc_s[n] / l_s[n])[hh * hd:(hh + 1) * hd, :] for n, (s2, hh) in enumerate(heads) if s2 == sl]
        o_ref[0, :, sl * LANES:(sl + 1) * LANES] = jnp.concatenate(parts, axis=0).T.astype(o_ref.dtype)


def _fox_prompt(q, k, v, lf, nh):
    b, t, w = k.shape
    hd = w // nh
    tq = min(ATTN_TILE, t)
    nkv = t // tq
    sel_q, sel_k, one_q, one_k = _decay_selectors(nh, hd)
    blk = lambda width: pl.BlockSpec((1, tq, width), lambda i, j: (i, j, 0))
    qa, ka, vt = pl.pallas_call(
        _fox_prep_kernel, grid=(b, t // tq),
        in_specs=[blk(w), blk(w), blk(w), blk(LANES), _const_spec(sel_q.shape), _const_spec(sel_k.shape),
                  _const_spec(one_q.shape), _const_spec(one_k.shape)],
        out_specs=(blk(2 * w), blk(2 * w), pl.BlockSpec((1, 1, w, tq), lambda i, j: (i, j, 0, 0))),
        out_shape=(jax.ShapeDtypeStruct((b, t, 2 * w), BF16), jax.ShapeDtypeStruct((b, t, 2 * w), BF16),
                   jax.ShapeDtypeStruct((b, nkv, w, tq), BF16)),
        scratch_shapes=[pltpu.VMEM((1, LANES), F32)],
        compiler_params=_params("parallel", "arbitrary"), name="fox_prep",
    )(q, k, v, lf, sel_q, sel_k, one_q, one_k)
    kern = functools.partial(_fox_prompt_kernel, hd=hd, tq=tq)
    return pl.pallas_call(
        kern, grid=(b, nkv),
        in_specs=[pl.BlockSpec((1, tq, 2 * w), lambda i, j: (i, j, 0)),
                  pl.BlockSpec((1, t, 2 * w), lambda i, j: (i, 0, 0)),
                  pl.BlockSpec((1, nkv, w, tq), lambda i, j: (i, 0, 0, 0))],
        out_specs=pl.BlockSpec((1, tq, w), lambda i, j: (i, j, 0)),
        out_shape=jax.ShapeDtypeStruct((b, t, w), BF16),
        scratch_shapes=[pltpu.VMEM((nh, tq, 2 * LANES), BF16), pltpu.VMEM((nh, 1, tq), F32),
                        pltpu.VMEM((nh, 1, tq), F32), pltpu.VMEM((nh, LANES, tq), F32)],
        compiler_params=_params("parallel", "arbitrary"), name="fox_prompt",
    )(qa, ka, vt)


def _fox_sample_kernel(pt_ref, q_ref, kn_ref, vn_ref, lnt_ref, *rest, nh, hd, g_pages):
    k_pages = rest[:g_pages]
    v_pages = rest[g_pages:2 * g_pages]
    f_pages = rest[2 * g_pages:3 * g_pages]
    o_ref, qbd_s, m_s, l_s, acc_s, carry_s, cn_s = rest[3 * g_pages:]
    del pt_ref
    step = pl.program_id(1)
    nt = q_ref.shape[0]
    w = q_ref.shape[1]
    page = k_pages[0].shape[-1]
    rows = nt * nh

    def update(s3, vs):
        m_old = m_s[...]
        m_new = jnp.maximum(m_old, jnp.max(s3, axis=-1, keepdims=True))
        alpha = jnp.exp2(m_old - m_new)
        p3 = jnp.exp2(s3 - m_new)
        l_s[...] = alpha * l_s[...] + jnp.sum(p3, axis=-1, keepdims=True)
        p = p3.reshape(rows, p3.shape[-1]).astype(BF16)
        pv = None
        off = 0
        for vt in vs:
            part = _dot_nt(p[:, off:off + vt.shape[1]], vt)
            pv = part if pv is None else pv + part
            off += vt.shape[1]
        acc_s[...] = alpha * acc_s[...] + pv.reshape(nt, nh, w)
        m_s[...] = m_new

    @pl.when(step == 0)
    def _():
        rep_r = lax.broadcasted_iota(jnp.int32, (rows, nt), 0) // nh
        rep_c = lax.broadcasted_iota(jnp.int32, (rows, nt), 1)
        q_rep = _dot(jnp.where(rep_r == rep_c, 1.0, 0.0), q_ref[...])
        head_of_lane = lax.broadcasted_iota(jnp.int32, (rows, w), 1) // hd
        head_of_row = lax.broadcasted_iota(jnp.int32, (rows, w), 0) % nh
        qbd = jnp.where(head_of_lane == head_of_row, q_rep, 0.0).astype(BF16)
        qbd_s[...] = qbd

        lane = lax.broadcasted_iota(jnp.int32, (nh, page), 1)
        cn = jnp.concatenate([lnt_ref[...] * LOG2E, jnp.zeros((nh, page - nt), F32)], axis=1)
        sh = 1
        while sh < nt:
            cn = cn + jnp.where(lane >= sh, pltpu.roll(cn, sh, 1), 0.0)
            sh *= 2
        cn_s[...] = cn
        carry_s[...] = jnp.zeros_like(carry_s)
        m_s[...] = jnp.full(m_s.shape, NEG_INF, F32)
        l_s[...] = jnp.zeros_like(l_s)
        acc_s[...] = jnp.zeros_like(acc_s)

        kn = jnp.concatenate([kn_ref[...], jnp.zeros((page - nt, w), F32)], axis=0).astype(BF16)
        vn = jnp.concatenate([vn_ref[...], jnp.zeros((page - nt, w), F32)], axis=0).T.astype(BF16)
        s3 = _dot_nt(qbd, kn).reshape(nt, nh, page)
        cq = jnp.stack([cn[:, t:t + 1] for t in range(nt)], axis=0)
        s3 = s3 + (cq - cn[None, :, :])
        tk = lax.broadcasted_iota(jnp.int32, (nt, nh, page), 2)
        tq = lax.broadcasted_iota(jnp.int32, (nt, nh, page), 0)
        update(jnp.where(tk <= tq, s3, NEG_INF), [vn])

    lane = lax.broadcasted_iota(jnp.int32, (nh, page), 1)
    cn = cn_s[...]
    cq = jnp.stack([cn[:, t:t + 1] for t in range(nt)], axis=0)
    qbd = qbd_s[...]
    carry = carry_s[...]
    s_tiles, v_tiles = [], []
    for g in range(g_pages):
        x = f_pages[g][...] * LOG2E
        incl = x
        sh = 1
        while sh < page:
            incl = incl + jnp.where(lane + sh < page, pltpu.roll(incl, page - sh, 1), 0.0)
            sh *= 2
        r = carry + (incl - x)
        carry = carry + incl[:, 0:1]
        s3 = _dot(qbd, k_pages[g][...].reshape(w, page).astype(BF16)).reshape(nt, nh, page)
        s_tiles.append(s3 + (cq + r[None, :, :]))
        v_tiles.append(v_pages[g][...].reshape(w, page).astype(BF16))
    carry_s[...] = carry
    update(jnp.concatenate(s_tiles, axis=-1), v_tiles)

    @pl.when(step == pl.num_programs(1) - 1)
    def _():
        head_of_lane = lax.broadcasted_iota(jnp.int32, (nt, nh, w), 2) // hd
        head_of_row = lax.broadcasted_iota(jnp.int32, (nt, nh, w), 1)
        o3 = jnp.where(head_of_lane == head_of_row, acc_s[...] / l_s[...], 0.0)
        o2 = o3.reshape(rows, w).astype(BF16)
        sel_r = lax.broadcasted_iota(jnp.int32, (nt, rows), 0)
        sel_c = lax.broadcasted_iota(jnp.int32, (nt, rows), 1) // nh
        sel = jnp.where(sel_r == sel_c, 1.0, 0.0).astype(BF16)
        o_ref[...] = _dot(sel, o2).astype(o_ref.dtype)


def _fox_sample(q, kn, vn, lnt, cache_kt, cache_vt, cache_ft, page_table, layer):
    nseq, n_pages = page_table.shape
    n, w = q.shape
    nt = n // nseq
    _, _, nh, hd, page = cache_kt.shape
    g_pages = min(PAGES_PER_STEP, n_pages)
    n_steps = n_pages // g_pages

    def page_spec(shape, g):
        nd = len(shape)

        def idx(b, s, pt):
            return (layer, pt[b * n_pages + (n_pages - 1 - (s * g_pages + g))]) + (0,) * nd
        return pl.BlockSpec((None, None) + shape, idx)

    seq_rows = pl.BlockSpec((nt, w), lambda b, s, pt: (b, 0))
    in_specs = [seq_rows, seq_rows, seq_rows, pl.BlockSpec((None, nh, nt), lambda b, s, pt: (b, 0, 0))]
    in_specs += [page_spec((nh, hd, page), g) for g in range(g_pages)]
    in_specs += [page_spec((nh, hd, page), g) for g in range(g_pages)]
    in_specs += [page_spec((nh, page), g) for g in range(g_pages)]
    kern = functools.partial(_fox_sample_kernel, nh=nh, hd=hd, g_pages=g_pages)
    grid_spec = pltpu.PrefetchScalarGridSpec(
        num_scalar_prefetch=1, grid=(nseq, n_steps), in_specs=in_specs,
        out_specs=pl.BlockSpec((nt, w), lambda b, s, pt: (b, 0)),
        scratch_shapes=[pltpu.VMEM((nt * nh, w), BF16),
                        pltpu.VMEM((nt, nh, 1), F32), pltpu.VMEM((nt, nh, 1), F32),
                        pltpu.VMEM((nt, nh, w), F32),
                        pltpu.VMEM((nh, page), F32), pltpu.VMEM((nh, page), F32)])
    return pl.pallas_call(
        kern, grid_spec=grid_spec, out_shape=jax.ShapeDtypeStruct((n, w), F32),
        compiler_params=_params("parallel", "arbitrary"), name="fox_sample",
    )(page_table.reshape(-1), q, kn, vn, lnt,
      *([cache_kt] * g_pages), *([cache_vt] * g_pages), *([cache_ft] * g_pages))


def _hgrn_kernel(q_ref, g_ref, kf_ref, v_ref, gc_ref, nw_ref, s0_ref, y_ref, sn_ref, st_s, *, c, nsub, kd):
    ti = pl.program_id(1)
    nh = s0_ref.shape[1]

    @pl.when(ti == 0)
    def _():
        for h in range(nh):
            st_s[h] = s0_ref[0, h].T

    rowi = lax.broadcasted_iota(jnp.int32, (c, 1), 0)

    def head_step(rs, h):
        hs = slice(h * kd, (h + 1) * kd)
        q = q_ref[0, rs, hs]
        g = g_ref[0, rs, hs]
        kf = kf_ref[0, rs, hs]
        v = v_ref[0, rs, hs]
        st = st_s[h]

        b = jnp.zeros_like(g)
        b_rows = []
        run = None
        for s in range(c):
            gs = g[s:s + 1, :]
            b = b + jnp.where(rowi >= s, gs, 0.0)
            run = gs if run is None else run + gs
            b_rows.append(run)
        bl = b_rows[-1]

        o = _dot_nt((q * jnp.exp(b)).astype(BF16), st.astype(BF16))
        for s in range(c):
            xs = q * jnp.exp(jnp.minimum(b - b_rows[s], 0.0)) * kf[s:s + 1, :]
            a = jnp.sum(xs, axis=-1, keepdims=True)
            o = o + jnp.where(rowi >= s, a, 0.0) * v[s:s + 1, :]

        kd_ = kf * jnp.exp(bl - b)
        st_s[h] = st * jnp.exp(bl) + _dot_tn(v.astype(BF16), kd_.astype(BF16))

        y = _rms(o, nw_ref[:, hs]) * gc_ref[0, rs, hs]
        y_ref[0, rs, hs] = y.astype(y_ref.dtype)

    def body(i, carry):
        rs = pl.ds(0 if nsub == 1 else pl.multiple_of(i * c, c), c)
        for h in range(nh):
            head_step(rs, h)
        return carry

    if nsub == 1:
        body(0, 0)
    else:
        lax.fori_loop(0, nsub, body, 0)

    @pl.when(ti == pl.num_programs(1) - 1)
    def _():
        for h in range(nh):
            sn_ref[0, h] = st_s[h].T


def _hgrn(qs, gl, kf, iv, gcs, nw, s0):
    b, t, w = qs.shape
    _, nh, kd, vd = s0.shape
    c = HGRN_SUB if t % HGRN_SUB == 0 else t
    tt = HGRN_TILE if t % HGRN_TILE == 0 else t
    kern = functools.partial(_hgrn_kernel, c=c, nsub=tt // c, kd=kd)
    seq = pl.BlockSpec((1, tt, w), lambda i, j: (i, j, 0))
    state = pl.BlockSpec((1, nh, kd, vd), lambda i, j: (i, 0, 0, 0))
    return pl.pallas_call(
        kern, grid=(b, t // tt),
        in_specs=[seq, seq, seq, seq, seq, _const_spec(nw.shape), state],
        out_specs=(seq, state),
        out_shape=(jax.ShapeDtypeStruct((b, t, nh * vd), BF16), jax.ShapeDtypeStruct(s0.shape, F32)),
        scratch_shapes=[pltpu.VMEM((nh, vd, kd), F32)],
        compiler_params=_params("parallel", "arbitrary"), name="hgrn",
    )(qs, gl, kf, iv, gcs, nw, s0)


def _merge_kernel(x_ref, ya_ref, ub_ref, vb_ref, yc_ref, gt_ref, gm_ref, gb_ref,
                  woa_ref, wob_ref, woc_ref, wo_ref, o_ref, *, n_groups):
    tm = x_ref.shape[0]
    d = x_ref.shape[1]
    cn = gm_ref.shape[1]
    gw = ub_ref.shape[1] // n_groups
    yb_rows = []
    for ci in range(tm // cn):
        rs = slice(ci * cn, (ci + 1) * cn)
        vb = vb_ref[rs, :].astype(BF16)
        gb = gb_ref[...]
        parts = []
        for g in range(n_groups):
            parts.append(_dot(gm_ref[g], vb[:, g * gw:(g + 1) * gw]) + gb[:, g:g + 1])
        yb_rows.append(ub_ref[rs, :] * jnp.concatenate(parts, axis=1))
    yb = jnp.concatenate(yb_rows, axis=0) if len(yb_rows) > 1 else yb_rows[0]
    gt = gt_ref[...]
    merged = (gt[:, :d] * _dot(ya_ref[...].astype(BF16), woa_ref[...])
              + gt[:, d:2 * d] * _dot(yb.astype(BF16), wob_ref[...])
              + gt[:, 2 * d:] * _dot(yc_ref[...].astype(BF16), woc_ref[...]))
    o_ref[...] = x_ref[...] + _dot(merged.astype(BF16), wo_ref[...])


def _merge(x, ya, ub, vb, yc, gt, gm, gb, woa, wob, woc, wo):
    n, d = x.shape
    tm = min(ROW_TILE, n)

    def rows(w):
        return pl.BlockSpec((tm, w), lambda i: (i, 0))

    kern = functools.partial(_merge_kernel, n_groups=gm.shape[0])
    return pl.pallas_call(
        kern, grid=(n // tm,),
        in_specs=[rows(d), rows(ya.shape[1]), rows(ub.shape[1]), rows(vb.shape[1]), rows(yc.shape[1]),
                  rows(gt.shape[1]), _const_spec(gm.shape), _const_spec(gb.shape),
                  _const_spec(woa.shape), _const_spec(wob.shape), _const_spec(woc.shape),
                  _const_spec(wo.shape)],
        out_specs=rows(d), out_shape=jax.ShapeDtypeStruct((n, d), F32),
        compiler_params=_params("parallel"), name="merge",
    )(x, ya, ub, vb, yc, gt, gm, gb, woa, wob, woc, wo)


def _ffn_kernel(x_ref, n2_ref, wup_ref, wcv_ref, bcv_ref, wdn_ref, buf_ref, nf_ref,
                y_ref, nb_ref, carry_s, ext_s, h_s, act_s, *, shift, base, cw, final_norm):
    ti = pl.program_id(1)
    tm = x_ref.shape[1]
    d_ff = wdn_ref.shape[0]
    hist = 2 * shift

    @pl.when(ti == 0)
    def _():
        carry_s[...] = buf_ref[0]

    n_chunks = d_ff // cw
    n_slots = ext_s.shape[0]

    def col0(c, half):
        return half * d_ff + c * cw

    def stage(c):
        for half in range(2):
            cols = slice(col0(c, half), col0(c, half) + cw)
            ext = ext_s.at[(2 * c + half) % n_slots]
            ext[base - hist:base, :] = carry_s[:, cols]
            ext[base:base + tm, :] = _dot(h_s[...], wup_ref[:, cols])

    def conv(c, half):
        cols = slice(col0(c, half), col0(c, half) + cw)
        ext = ext_s.at[(2 * c + half) % n_slots]
        wcv = wcv_ref[:, cols]
        out = bcv_ref[:, cols] + (wcv[0:1, :] * ext[base - hist:base - hist + tm, :]
                                  + wcv[1:2, :] * ext[base - shift:base - shift + tm, :]
                                  + wcv[2:3, :] * ext[base:base + tm, :])
        carry_s[:, cols] = ext[base + tm - hist:base + tm, :]
        return out

    x = x_ref[0]
    h_s[...] = _rms(x, n2_ref[...]).astype(BF16)
    y = x
    stage(0)
    for c in range(n_chunks):
        if c + 1 < n_chunks:
            stage(c + 1)
        act_s[:, c * cw:(c + 1) * cw] = (_silu(conv(c, 0)) * conv(c, 1)).astype(BF16)
        if (c + 1) % FFN_DOWN_CHUNKS == 0 or c + 1 == n_chunks:
            r0 = (c // FFN_DOWN_CHUNKS) * FFN_DOWN_CHUNKS * cw
            y = y + _dot(act_s[:, r0:(c + 1) * cw], wdn_ref[r0:(c + 1) * cw, :])
    if final_norm:
        y = _rms(y, nf_ref[...])
    y_ref[0] = y
    nb_ref[0] = carry_s[...]


def _ffn(x, n2, wup, wcv, bcv, wdn, buf, nf, shift, final_norm):
    nseq, t, d = x.shape
    tm = min(ROW_TILE, t)
    hist = 2 * shift
    base = -(-hist // SUBLANES) * SUBLANES
    kern = functools.partial(_ffn_kernel, shift=shift, base=base, cw=FFN_COLS, final_norm=final_norm)
    state = pl.BlockSpec((1, hist, buf.shape[2]), lambda i, j: (i, 0, 0))
    return pl.pallas_call(
        kern, grid=(nseq, t // tm),
        in_specs=[pl.BlockSpec((1, tm, d), lambda i, j: (i, j, 0)), _const_spec(n2.shape),
                  _const_spec(wup.shape), _const_spec(wcv.shape), _const_spec(bcv.shape),
                  _const_spec(wdn.shape), state, _const_spec(nf.shape)],
        out_specs=(pl.BlockSpec((1, tm, d), lambda i, j: (i, j, 0)), state),
        out_shape=(jax.ShapeDtypeStruct(x.shape, F32), jax.ShapeDtypeStruct(buf.shape, F32)),
        scratch_shapes=[pltpu.VMEM((hist, buf.shape[2]), F32),
                        pltpu.VMEM((FFN_EXT_SLOTS, base + tm, FFN_COLS), F32),
                        pltpu.VMEM((tm, d), BF16), pltpu.VMEM((tm, wdn.shape[0]), BF16)],
        compiler_params=_params("parallel", "arbitrary"), name="conv_ffn",
    )(x, n2, wup, wcv, bcv, wdn, buf, nf)


def kernel(x_prompt, x_sample, cache_k, cache_v, cache_logf, state_hgrn, state_conv, page_table,
           norm1, w_in, b_f, gmlp_norm, w_s, b_s, hgrn_norm, hgrn_lb, w_oa, w_ob, w_oc, w_o,
           norm2, w_up, w_conv, b_conv, w_down, norm_f):
    depth = w_in.shape[0]
    bp, tp, d = x_prompt.shape
    bs, ts, _ = x_sample.shape
    _, n_phys, page, nh_a, hd_a = cache_k.shape
    w_a = nh_a * hd_a
    n_grp, chunk_b, _ = w_s.shape[1:]
    w_b = gmlp_norm.shape[1]
    _, _, nh_c, k_c, v_c = state_hgrn.shape
    w_c = nh_c * v_c
    d_ff = w_down.shape[1]

    sm = jax.nn.softmax(hgrn_lb.astype(F32), axis=0)
    lower = jnp.cumsum(sm, axis=0) - sm[:1]

    cache_kt = jnp.transpose(cache_k, (0, 1, 3, 4, 2))
    cache_vt = jnp.transpose(cache_v, (0, 1, 3, 4, 2))
    cache_ft = jnp.swapaxes(cache_logf, 2, 3)

    xp = x_prompt.reshape(bp * tp, d)
    xs = x_sample.reshape(bs * ts, d)
    st_p, st_s = [], []
    for l in range(depth):
        o = 0
        wl = jnp.swapaxes(w_in[l], 0, 1)
        wa = wl[o:o + 3 * w_a]; o += 3 * w_a
        wf = wl[o:o + nh_a]; o += nh_a
        wb = wl[o:o + 2 * w_b]; o += 2 * w_b
        wc = wl[o:o + 4 * w_c]; o += 4 * w_c
        wg = wl[o:]
        qscale = jnp.concatenate([jnp.full((w_a, 1), hd_a ** -0.5 * LOG2E, F32), jnp.ones((2 * w_a, 1), F32)])
        wa = (wa * qscale).astype(BF16)
        wf_pad = jnp.pad(wf, ((0, LANES - nh_a), (0, 0))).astype(BF16)
        bf_row = jnp.pad(b_f[l], (0, LANES - nh_a)).reshape(1, LANES)
        inproj_w = (norm1[l].reshape(1, d), wa, wf_pad, wb.astype(BF16), wc.astype(BF16),
                    wg.astype(BF16), bf_row, gmlp_norm[l].reshape(1, w_b), lower[l].reshape(1, w_c))

        tril_w = jnp.where(jnp.tril(jnp.ones((chunk_b, chunk_b), bool)), w_s[l], 0)
        woa, wob, woc, wo = (w_oa[l].astype(BF16), w_ob[l].astype(BF16), w_oc[l].astype(BF16),
                             w_o[l].astype(BF16))
        nw = hgrn_norm[l].reshape(1, w_c)

        n2 = norm2[l].reshape(1, d)
        wup = w_up[l].astype(BF16)
        wcv = w_conv[l]
        bcv = b_conv[l].reshape(1, 2 * d_ff)
        wdn = w_down[l].astype(BF16)
        nf = norm_f.reshape(1, d)
        last = l == depth - 1

        (q, k, v, lf, ub, vb, qs, gl, kf, iv, gcs, gt) = _inproj(xp, *inproj_w)
        ya = _fox_prompt(q.reshape(bp, tp, w_a), k.reshape(bp, tp, w_a), v.reshape(bp, tp, w_a),
                         lf.reshape(bp, tp, LANES), nh_a)
        s0 = jnp.zeros((bp, nh_c, k_c, v_c), F32)
        yc, s_new = _hgrn(*(a.reshape(bp, tp, w_c) for a in (qs, gl, kf, iv, gcs)), nw, s0)
        x1 = _merge(xp, ya.reshape(bp * tp, w_a), ub, vb, yc.reshape(bp * tp, w_c), gt,
                    tril_w.astype(BF16), b_s[l].T, woa, wob, woc, wo)
        buf0 = jnp.zeros((bp, 2, 2 * d_ff), F32)
        x2, conv_p = _ffn(x1.reshape(bp, tp, d), n2, wup, wcv, bcv, wdn, buf0, nf, 1, last)
        xp = x2.reshape(bp * tp, d)
        st_p.append((k.reshape(bp, tp, nh_a, hd_a), v.reshape(bp, tp, nh_a, hd_a),
                     lf[:, :nh_a].reshape(bp, tp, nh_a), s_new, conv_p))

        (q, k, v, lf, ub, vb, qs, gl, kf, iv, gcs, gt) = _inproj(xs, *inproj_w)
        k4 = k.reshape(bs, ts, nh_a, hd_a)
        v4 = v.reshape(bs, ts, nh_a, hd_a)
        lf3 = lf[:, :nh_a].reshape(bs, ts, nh_a)
        ya = _fox_sample(q.astype(F32), k, v, jnp.swapaxes(lf3, 1, 2),
                         cache_kt, cache_vt, cache_ft, page_table, l)
        yc, s_new = _hgrn(*(a.reshape(bs, ts, w_c) for a in (qs, gl, kf, iv, gcs)), nw, state_hgrn[l])
        same_seq = np.kron(np.eye(bs, dtype=np.float32), np.ones((ts, ts), np.float32))
        gm = (jnp.tile(tril_w[:, :ts, :ts], (1, bs, bs)) * same_seq).astype(BF16)
        gb = jnp.tile(b_s[l][:, :ts].T, (bs, 1))
        x1 = _merge(xs, ya, ub, vb, yc.reshape(bs * ts, w_c), gt, gm, gb, woa, wob, woc, wo)
        x1t = jnp.swapaxes(x1.reshape(bs, ts, d), 0, 1).reshape(1, ts * bs, d)
        buf_t = jnp.swapaxes(state_conv[l], 0, 1).reshape(1, 2 * bs, 2 * d_ff)
        x2t, nb = _ffn(x1t, n2, wup, wcv, bcv, wdn, buf_t, nf, bs, last)
        xs = jnp.swapaxes(x2t.reshape(ts, bs, d), 0, 1).reshape(bs * ts, d)
        conv_s = nb.reshape(2, bs, 2 * d_ff)
        st_s.append((k4, v4, lf3, vb.reshape(bs, ts, w_b), s_new, jnp.swapaxes(conv_s, 0, 1)))

    def stk(sts, i):
        return jnp.stack([st[i] for st in sts], axis=0)

    return (xp.reshape(bp, tp, d), xs.reshape(bs, ts, d),
            stk(st_p, 0), stk(st_p, 1), stk(st_p, 2), stk(st_p, 3), stk(st_p, 4),
            stk(st_s, 0), stk(st_s, 1), stk(st_s, 2), stk(st_s, 3), stk(st_s, 4), stk(st_s, 5))
```

```python
import functools

import numpy as np

import jax
import jax.numpy as jnp
from jax import lax
from jax.experimental import pallas as pl
from jax.experimental.pallas import tpu as pltpu

F32 = jnp.float32
BF16 = jnp.bfloat16
EPS = 1e-6
NEG_INF = float("-inf")
LOG2E = 1.4426950408889634

LANES = 128
SUBLANES = 8
VMEM_LIMIT_BYTES = 56 * 1024 * 1024

ROW_TILE = 256
ATTN_TILE = 256
PAGES_PER_STEP = 8
HGRN_SUB = 16
HGRN_TILE = 256
HGRN_GROUP = 8
FFN_COLS = 256
FFN_EXT_SLOTS = 4
FFN_DOWN_CHUNKS = 4
N_SPLIT = 3


def _dot(a, b):
    return jnp.dot(a, b, preferred_element_type=F32)


def _dot_nt(a, b):
    return lax.dot_general(a, b, (((1,), (1,)), ((), ())), preferred_element_type=F32)


def _dot_tn(a, b):
    return lax.dot_general(a, b, (((0,), (0,)), ((), ())), preferred_element_type=F32)


def _rms(x, g):
    return x * lax.rsqrt(jnp.mean(x * x, axis=-1, keepdims=True) + EPS) * g


def _sigmoid(x):
    return 1.0 / (1.0 + jnp.exp(-x))


def _silu(x):
    return x * _sigmoid(x)


def _gelu_tanh(x):
    c = 0.7978845608028654
    return x * (0.5 * (1.0 + jnp.tanh(c * (x + 0.044715 * (x * x * x)))))


def _log_sigmoid(z):
    return jnp.minimum(z, 0.0) - jnp.log1p(jnp.exp(-jnp.abs(z)))


def _split3(x):
    hi = x.astype(BF16)
    r = x - hi.astype(F32)
    mid = r.astype(BF16)
    lo = (r - mid.astype(F32)).astype(BF16)
    return hi, mid, lo


def _ones_tri(n, upper):
    r = lax.broadcasted_iota(jnp.int32, (n, n), 0)
    c = lax.broadcasted_iota(jnp.int32, (n, n), 1)
    keep = (r <= c) if upper else (r >= c)
    return jnp.where(keep, 1.0, 0.0).astype(BF16)


def _const_spec(shape):
    nd = len(shape)
    return pl.BlockSpec(shape, lambda *_: (0,) * nd, pipeline_mode=pl.Buffered(1))


def _params(*sem):
    return pltpu.CompilerParams(dimension_semantics=sem, vmem_limit_bytes=VMEM_LIMIT_BYTES)


def _inproj_kernel(x_ref, n1_ref, wa_ref, wf_ref, wb_ref, wc_ref, wg_ref, bf_ref, bfc_ref, gn_ref, lb_ref,
                   *rest, n_alias, seq_major):
    outs = rest[n_alias:]
    if seq_major:
        q_ref, k_ref, kt_ref, vt_ref, lf_ref, lft_ref = outs[:6]
        ub_ref, vb_ref, qs_ref, gl_ref, kf_ref, iv_ref, gc_ref, gt_ref = outs[6:]
    else:
        q_ref, k_ref, v_ref, lf_ref = outs[:4]
        ub_ref, vb_ref, qs_ref, gl_ref, kf_ref, iv_ref, gc_ref, gt_ref = outs[4:]
    wa_w = k_ref.shape[-1]
    wb_w = ub_ref.shape[-1]
    wc_w = qs_ref.shape[-1]
    h = _rms(x_ref[...], n1_ref[...]).astype(BF16)

    if seq_major:
        pa = _dot_nt(h, wa_ref[:2 * wa_w, :])
        k_ref[...] = pa[:, wa_w:].astype(k_ref.dtype)
        kv_t = _dot_nt(wa_ref[wa_w:, :], h)
        kt_ref[...] = kv_t[:wa_w, :]
        vt_ref[...] = kv_t[wa_w:, :]
        lft = _log_sigmoid(_dot_nt(wf_ref[...], h)[:bfc_ref.shape[0], :] + bfc_ref[...])
        lft_ref[...] = lft
    else:
        pa = _dot_nt(h, wa_ref[...])
        k_ref[...] = pa[:, wa_w:2 * wa_w]
        v_ref[...] = pa[:, 2 * wa_w:]
    q_ref[...] = pa[:, :wa_w].astype(q_ref.dtype)

    lf_ref[...] = _log_sigmoid(_dot_nt(h, wf_ref[...]) + bf_ref[...])

    pb = _dot_nt(h, wb_ref[...])
    ub_ref[...] = _gelu_tanh(pb[:, :wb_w])
    vb_ref[...] = _rms(_gelu_tanh(pb[:, wb_w:]), gn_ref[...])

    pc = _dot_nt(h, wc_ref[...])
    lb = lb_ref[...]
    qs_ref[...] = _silu(pc[:, :wc_w])
    f = lb + (1.0 - lb) * _sigmoid(pc[:, wc_w:2 * wc_w])
    gl_ref[...] = jnp.log(f)
    kf_ref[...] = 1.0 - f
    iv_ref[...] = pc[:, 2 * wc_w:3 * wc_w]
    gc_ref[...] = _silu(pc[:, 3 * wc_w:])

    gt_ref[...] = _sigmoid(_dot_nt(h, wg_ref[...]))


def _inproj(x, n1, wa, wf, wb, wc, wg, bf_row, bf_col, gn, lb, stacked=None):
    n, d = x.shape
    tm = min(ROW_TILE, n)
    wa_w, wb_w, wc_w = wa.shape[0] // 3, wb.shape[0] // 2, wc.shape[0] // 4
    nh = bf_col.shape[0]

    def rows(w):
        return pl.BlockSpec((tm, w), lambda i: (i, 0))

    q_shape = jax.ShapeDtypeStruct((n, wa_w), BF16)
    lf_shape = jax.ShapeDtypeStruct((n, LANES), F32)
    extra_in, extra_specs, aliases = [], [], {}
    if stacked is None:
        head_shape = (q_shape, jax.ShapeDtypeStruct((n, wa_w), F32), jax.ShapeDtypeStruct((n, wa_w), F32),
                      lf_shape)
        head_specs = (rows(wa_w), rows(wa_w), rows(wa_w), rows(LANES))
    else:
        layer, depth, b, prev = stacked
        t = n // b
        tps = t // tm

        def feat_major(f):
            return (jax.ShapeDtypeStruct((depth, b, f, t), F32),
                    pl.BlockSpec((None, None, f, tm), lambda i: (layer, i // tps, 0, i % tps)))

        (kt_shape, kt_spec), (lft_shape, lft_spec) = feat_major(wa_w), feat_major(nh)
        head_shape = (q_shape, jax.ShapeDtypeStruct((n, wa_w), BF16), kt_shape, kt_shape, lf_shape, lft_shape)
        head_specs = (rows(wa_w), rows(wa_w), kt_spec, kt_spec, rows(LANES), lft_spec)
        if prev is not None:
            extra_in = list(prev)
            extra_specs = [pl.BlockSpec(memory_space=pl.ANY)] * len(prev)
            aliases = {11: 2, 12: 3, 13: 5}

    out_shape = head_shape + (
        jax.ShapeDtypeStruct((n, wb_w), F32),
        jax.ShapeDtypeStruct((n, wb_w), F32),
        jax.ShapeDtypeStruct((n, wc_w), F32),
        jax.ShapeDtypeStruct((n, wc_w), F32),
        jax.ShapeDtypeStruct((n, wc_w), F32),
        jax.ShapeDtypeStruct((n, wc_w), F32),
        jax.ShapeDtypeStruct((n, wc_w), F32),
        jax.ShapeDtypeStruct((n, wg.shape[0]), F32),
    )
    out_specs = head_specs + (rows(wb_w), rows(wb_w), rows(wc_w), rows(wc_w), rows(wc_w), rows(wc_w),
                              rows(wc_w), rows(wg.shape[0]))
    in_specs = [rows(d), _const_spec(n1.shape), _const_spec(wa.shape), _const_spec(wf.shape),
                _const_spec(wb.shape), _const_spec(wc.shape), _const_spec(wg.shape),
                _const_spec(bf_row.shape), _const_spec(bf_col.shape), _const_spec(gn.shape),
                _const_spec(lb.shape)] + extra_specs
    kern = functools.partial(_inproj_kernel, n_alias=len(extra_in), seq_major=stacked is not None)
    return pl.pallas_call(
        kern, grid=(n // tm,), in_specs=in_specs, out_specs=out_specs, out_shape=out_shape,
        input_output_aliases=aliases, compiler_params=_params("parallel"), name="inproj",
    )(x, n1, wa, wf, wb, wc, wg, bf_row, bf_col, gn, lb, *extra_in)


def _decay_selectors(nh, hd):
    sel_q = np.zeros((N_SPLIT * LANES, nh * hd), np.float32)
    sel_k = np.zeros((N_SPLIT * LANES, nh * hd), np.float32)
    one_q = np.zeros((1, nh * hd), np.float32)
    one_k = np.zeros((1, nh * hd), np.float32)
    for h in range(nh):
        for p in range(N_SPLIT):
            sel_k[p * LANES + h, h * hd + p] = -1.0
            one_k[0, h * hd + N_SPLIT + p] = 1.0
            one_q[0, h * hd + p] = 1.0
            sel_q[p * LANES + h, h * hd + N_SPLIT + p] = 1.0
    return (jnp.asarray(sel_q, BF16), jnp.asarray(sel_k, BF16), jnp.asarray(one_q), jnp.asarray(one_k))


def _fox_prep_kernel(q_ref, k_ref, vin_ref, lf_ref, selq_ref, selk_ref, oneq_ref, onek_ref,
                     qa_ref, ka_ref, vt_ref, carry_s):
    ti = pl.program_id(1)
    tm = q_ref.shape[1]
    w = q_ref.shape[2]

    @pl.when(ti == 0)
    def _():
        carry_s[...] = jnp.zeros_like(carry_s)

    lower = _ones_tri(tm, upper=False)
    hi, mid, lo = _split3(lf_ref[0])
    c = _dot(lower, hi) + _dot(lower, mid) + _dot(lower, lo) + carry_s[...]
    carry_s[...] = c[tm - 1:tm, :]
    c3 = jnp.concatenate(_split3(c * LOG2E), axis=1)
    eq = _dot(c3, selq_ref[...]) + oneq_ref[...]
    ek = _dot(c3, selk_ref[...]) + onek_ref[...]
    q = q_ref[0].astype(F32)
    k = k_ref[0].astype(F32)
    qa_parts, ka_parts = [], []
    for p in range(w // LANES):
        ls = slice(p * LANES, (p + 1) * LANES)
        qa_parts += [q[:, ls], eq[:, ls]]
        ka_parts += [k[:, ls], ek[:, ls]]
    qa_ref[0] = jnp.concatenate(qa_parts, axis=1).astype(BF16)
    ka_ref[0] = jnp.concatenate(ka_parts, axis=1).astype(BF16)
    vt_ref[0, 0] = vin_ref[...].astype(BF16)


def _fox_prompt_kernel(qa_ref, ka_ref, vt_ref, o_ref, qh_s, m_s, l_s, acc_s, *, hd, tq):
    qi = pl.program_id(1)
    heads_per_slab = LANES // hd
    n_slab = vt_ref.shape[2] // LANES
    heads = [(sl, hh) for sl in range(n_slab) for hh in range(heads_per_slab)]
    lane2 = lax.broadcasted_iota(jnp.int32, (tq, 2 * LANES), 1) % LANES
    key = lax.broadcasted_iota(jnp.int32, (tq, tq), 0)
    qry = lax.broadcasted_iota(jnp.int32, (tq, tq), 1)

    for n, (sl, hh) in enumerate(heads):
        qa = qa_ref[0, :, sl * 2 * LANES:(sl + 1) * 2 * LANES].astype(F32)
        qh_s[n] = jnp.where(lane2 // hd == hh, qa, 0.0).astype(BF16)
        m_s[n] = jnp.full((1, tq), NEG_INF, F32)
        l_s[n] = jnp.zeros((1, tq), F32)
        acc_s[n] = jnp.zeros((LANES, tq), F32)

    def tile(j, masked):
        k0 = pl.multiple_of(j * tq, tq)
        s_all = []
        for n, (sl, _) in enumerate(heads):
            s = _dot_nt(ka_ref[0, pl.ds(k0, tq), sl * 2 * LANES:(sl + 1) * 2 * LANES], qh_s[n])
            s_all.append(jnp.where(key <= qry, s, NEG_INF) if masked else s)
        m_old = [m_s[n] for n in range(len(heads))]
        m_new = [jnp.maximum(m, jnp.max(s, axis=0, keepdims=True)) for m, s in zip(m_old, s_all)]
        p_all = [jnp.exp2(s - m) for s, m in zip(s_all, m_new)]
        for n, (sl, _) in enumerate(heads):
            alpha = jnp.exp2(m_old[n] - m_new[n])
            l_s[n] = alpha * l_s[n] + jnp.sum(p_all[n], axis=0, keepdims=True)
            pv = _dot(vt_ref[0, j, sl * LANES:(sl + 1) * LANES, :], p_all[n].astype(BF16))
            acc_s[n] = alpha * acc_s[n] + pv
            m_s[n] = m_new[n]

    def body(j, carry):
        tile(j, False)
        return carry

    lax.fori_loop(0, qi, body, 0)
    tile(qi, True)
    for sl in range(n_slab):
        parts = [(acc_s[n] / l_s[n])[hh * hd:(hh + 1) * hd, :] for n, (s2, hh) in enumerate(heads) if s2 == sl]
        o_ref[0, :, sl * LANES:(sl + 1) * LANES] = jnp.concatenate(parts, axis=0).T.astype(o_ref.dtype)


def _fox_prompt(q, k, vt_all, layer, lf, nh):
    b, t, w = k.shape
    hd = w // nh
    tq = min(ATTN_TILE, t)
    nkv = t // tq
    sel_q, sel_k, one_q, one_k = _decay_selectors(nh, hd)
    blk = lambda width: pl.BlockSpec((1, tq, width), lambda i, j: (i, j, 0))
    qa, ka, vt = pl.pallas_call(
        _fox_prep_kernel, grid=(b, t // tq),
        in_specs=[blk(w), blk(w), pl.BlockSpec((None, None, w, tq), lambda i, j: (layer, i, 0, j)),
                  blk(LANES), _const_spec(sel_q.shape), _const_spec(sel_k.shape),
                  _const_spec(one_q.shape), _const_spec(one_k.shape)],
        out_specs=(blk(2 * w), blk(2 * w), pl.BlockSpec((1, 1, w, tq), lambda i, j: (i, j, 0, 0))),
        out_shape=(jax.ShapeDtypeStruct((b, t, 2 * w), BF16), jax.ShapeDtypeStruct((b, t, 2 * w), BF16),
                   jax.ShapeDtypeStruct((b, nkv, w, tq), BF16)),
        scratch_shapes=[pltpu.VMEM((1, LANES), F32)],
        compiler_params=_params("parallel", "arbitrary"), name="fox_prep",
    )(q, k, vt_all, lf, sel_q, sel_k, one_q, one_k)
    kern = functools.partial(_fox_prompt_kernel, hd=hd, tq=tq)
    return pl.pallas_call(
        kern, grid=(b, nkv),
        in_specs=[pl.BlockSpec((1, tq, 2 * w), lambda i, j: (i, j, 0)),
                  pl.BlockSpec((1, t, 2 * w), lambda i, j: (i, 0, 0)),
                  pl.BlockSpec((1, nkv, w, tq), lambda i, j: (i, 0, 0, 0))],
        out_specs=pl.BlockSpec((1, tq, w), lambda i, j: (i, j, 0)),
        out_shape=jax.ShapeDtypeStruct((b, t, w), BF16),
        scratch_shapes=[pltpu.VMEM((nh, tq, 2 * LANES), BF16), pltpu.VMEM((nh, 1, tq), F32),
                        pltpu.VMEM((nh, 1, tq), F32), pltpu.VMEM((nh, LANES, tq), F32)],
        compiler_params=_params("parallel", "arbitrary"), name="fox_prompt",
    )(qa, ka, vt)


def _fox_sample_kernel(pt_ref, q_ref, kn_ref, vn_ref, lnt_ref, *rest, nh, hd, g_pages):
    k_pages = rest[:g_pages]
    v_pages = rest[g_pages:2 * g_pages]
    f_pages = rest[2 * g_pages:3 * g_pages]
    o_ref, qbd_s, m_s, l_s, acc_s, carry_s, cn_s = rest[3 * g_pages:]
    del pt_ref
    step = pl.program_id(1)
    nt = q_ref.shape[0]
    w = q_ref.shape[1]
    page = k_pages[0].shape[-1]
    rows = nt * nh

    def update(s3, vs):
        m_old = m_s[...]
        m_new = jnp.maximum(m_old, jnp.max(s3, axis=-1, keepdims=True))
        alpha = jnp.exp2(m_old - m_new)
        p3 = jnp.exp2(s3 - m_new)
        l_s[...] = alpha * l_s[...] + jnp.sum(p3, axis=-1, keepdims=True)
        p = p3.reshape(rows, p3.shape[-1]).astype(BF16)
        pv = None
        off = 0
        for vt in vs:
            part = _dot_nt(p[:, off:off + vt.shape[1]], vt)
            pv = part if pv is None else pv + part
            off += vt.shape[1]
        acc_s[...] = alpha * acc_s[...] + pv.reshape(nt, nh, w)
        m_s[...] = m_new

    @pl.when(step == 0)
    def _():
        rep_r = lax.broadcasted_iota(jnp.int32, (rows, nt), 0) // nh
        rep_c = lax.broadcasted_iota(jnp.int32, (rows, nt), 1)
        q_rep = _dot(jnp.where(rep_r == rep_c, 1.0, 0.0), q_ref[...])
        head_of_lane = lax.broadcasted_iota(jnp.int32, (rows, w), 1) // hd
        head_of_row = lax.broadcasted_iota(jnp.int32, (rows, w), 0) % nh
        qbd = jnp.where(head_of_lane == head_of_row, q_rep, 0.0).astype(BF16)
        qbd_s[...] = qbd

        lane = lax.broadcasted_iota(jnp.int32, (nh, page), 1)
        cn = jnp.concatenate([lnt_ref[...] * LOG2E, jnp.zeros((nh, page - nt), F32)], axis=1)
        sh = 1
        while sh < nt:
            cn = cn + jnp.where(lane >= sh, pltpu.roll(cn, sh, 1), 0.0)
            sh *= 2
        cn_s[...] = cn
        carry_s[...] = jnp.zeros_like(carry_s)
        m_s[...] = jnp.full(m_s.shape, NEG_INF, F32)
        l_s[...] = jnp.zeros_like(l_s)
        acc_s[...] = jnp.zeros_like(acc_s)

        kn = jnp.concatenate([kn_ref[...], jnp.zeros((page - nt, w), F32)], axis=0).astype(BF16)
        vn = jnp.concatenate([vn_ref[...], jnp.zeros((page - nt, w), F32)], axis=0).T.astype(BF16)
        s3 = _dot_nt(qbd, kn).reshape(nt, nh, page)
        cq = jnp.stack([cn[:, t:t + 1] for t in range(nt)], axis=0)
        s3 = s3 + (cq - cn[None, :, :])
        tk = lax.broadcasted_iota(jnp.int32, (nt, nh, page), 2)
        tq = lax.broadcasted_iota(jnp.int32, (nt, nh, page), 0)
        update(jnp.where(tk <= tq, s3, NEG_INF), [vn])

    lane = lax.broadcasted_iota(jnp.int32, (nh, page), 1)
    cn = cn_s[...]
    cq = jnp.stack([cn[:, t:t + 1] for t in range(nt)], axis=0)
    qbd = qbd_s[...]
    carry = carry_s[...]
    s_tiles, v_tiles = [], []
    for g in range(g_pages):
        x = f_pages[g][...] * LOG2E
        incl = x
        sh = 1
        while sh < page:
            incl = incl + jnp.where(lane + sh < page, pltpu.roll(incl, page - sh, 1), 0.0)
            sh *= 2
        r = carry + (incl - x)
        carry = carry + incl[:, 0:1]
        s3 = _dot(qbd, k_pages[g][...].reshape(w, page).astype(BF16)).reshape(nt, nh, page)
        s_tiles.append(s3 + (cq + r[None, :, :]))
        v_tiles.append(v_pages[g][...].reshape(w, page).astype(BF16))
    carry_s[...] = carry
    update(jnp.concatenate(s_tiles, axis=-1), v_tiles)

    @pl.when(step == pl.num_programs(1) - 1)
    def _():
        head_of_lane = lax.broadcasted_iota(jnp.int32, (nt, nh, w), 2) // hd
        head_of_row = lax.broadcasted_iota(jnp.int32, (nt, nh, w), 1)
        o3 = jnp.where(head_of_lane == head_of_row, acc_s[...] / l_s[...], 0.0)
        o2 = o3.reshape(rows, w).astype(BF16)
        sel_r = lax.broadcasted_iota(jnp.int32, (nt, rows), 0)
        sel_c = lax.broadcasted_iota(jnp.int32, (nt, rows), 1) // nh
        sel = jnp.where(sel_r == sel_c, 1.0, 0.0).astype(BF16)
        o_ref[...] = _dot(sel, o2).astype(o_ref.dtype)


def _fox_sample(q, kn, vn, lnt, cache_kt, cache_vt, cache_ft, page_table, layer):
    nseq, n_pages = page_table.shape
    n, w = q.shape
    nt = n // nseq
    _, _, nh, hd, page = cache_kt.shape
    g_pages = min(PAGES_PER_STEP, n_pages)
    n_steps = n_pages // g_pages

    def page_spec(shape, g):
        nd = len(shape)

        def idx(b, s, pt):
            return (layer, pt[b * n_pages + (n_pages - 1 - (s * g_pages + g))]) + (0,) * nd
        return pl.BlockSpec((None, None) + shape, idx)

    seq_rows = pl.BlockSpec((nt, w), lambda b, s, pt: (b, 0))
    in_specs = [seq_rows, seq_rows, seq_rows, pl.BlockSpec((None, nh, nt), lambda b, s, pt: (b, 0, 0))]
    in_specs += [page_spec((nh, hd, page), g) for g in range(g_pages)]
    in_specs += [page_spec((nh, hd, page), g) for g in range(g_pages)]
    in_specs += [page_spec((nh, page), g) for g in range(g_pages)]
    kern = functools.partial(_fox_sample_kernel, nh=nh, hd=hd, g_pages=g_pages)
    grid_spec = pltpu.PrefetchScalarGridSpec(
        num_scalar_prefetch=1, grid=(nseq, n_steps), in_specs=in_specs,
        out_specs=pl.BlockSpec((nt, w), lambda b, s, pt: (b, 0)),
        scratch_shapes=[pltpu.VMEM((nt * nh, w), BF16),
                        pltpu.VMEM((nt, nh, 1), F32), pltpu.VMEM((nt, nh, 1), F32),
                        pltpu.VMEM((nt, nh, w), F32),
                        pltpu.VMEM((nh, page), F32), pltpu.VMEM((nh, page), F32)])
    return pl.pallas_call(
        kern, grid_spec=grid_spec, out_shape=jax.ShapeDtypeStruct((n, w), F32),
        compiler_params=_params("parallel", "arbitrary"), name="fox_sample",
    )(page_table.reshape(-1), q, kn, vn, lnt,
      *([cache_kt] * g_pages), *([cache_vt] * g_pages), *([cache_ft] * g_pages))


def _hgrn_kernel(q_ref, g_ref, kf_ref, v_ref, gc_ref, nw_ref, s0_ref, y_ref, sn_ref, st_s, *, c, nsub, kd):
    ti = pl.program_id(1)
    nh = s0_ref.shape[1]

    @pl.when(ti == 0)
    def _():
        for h in range(nh):
            st_s[h] = s0_ref[0, h].T

    blk = SUBLANES
    group = HGRN_GROUP if nsub % HGRN_GROUP == 0 else 1
    rowi = lax.broadcasted_iota(jnp.int32, (blk, 1), 0)
    subi = lax.broadcasted_iota(jnp.int32, (blk, kd), 0)

    def cumsum_blocks(g):
        b_blocks = []
        off = None
        for i in range(c // blk):
            x = g[i * blk:(i + 1) * blk, :]
            sh = 1
            while sh < blk:
                x = x + jnp.where(subi >= sh, pltpu.roll(x, sh, 0), 0.0)
                sh *= 2
            x = x if off is None else x + off
            off = x[blk - 1:blk, :]
            b_blocks.append(x)
        return b_blocks

    def cat(blocks):
        return jnp.concatenate(blocks, axis=0) if len(blocks) > 1 else blocks[0]

    def body(i, carry):
        streams = []
        for u in range(group):
            rs = pl.ds(0 if nsub == 1 else pl.multiple_of((i * group + u) * c, c), c)
            for h in range(nh):
                hs = slice(h * kd, (h + 1) * kd)
                q, kf, v = q_ref[0, rs, hs], kf_ref[0, rs, hs], v_ref[0, rs, hs]
                b_blocks = cumsum_blocks(g_ref[0, rs, hs])
                b = cat(b_blocks)
                a_prev = [None]
                for n in range(1, len(b_blocks)):
                    e = b_blocks[n - 1][blk - 1:blk, :]
                    a_prev.append(_dot_nt((q[n * blk:(n + 1) * blk, :] * jnp.exp(b_blocks[n] - e)).astype(BF16),
                                          (kf[:n * blk, :] * jnp.exp(e - b[:n * blk, :])).astype(BF16)))
                streams.append((rs, h, hs, q, kf, v, b_blocks, b, a_prev))

        diag = []
        for rs, h, hs, q, kf, v, b_blocks, b, a_prev in streams:
            o_blocks = []
            for n, bn in enumerate(b_blocks):
                r0 = n * blk
                qn = q[r0:r0 + blk, :]
                on = jnp.zeros((blk, kd), F32)
                for s in range(blk):
                    xs = qn * jnp.exp(jnp.minimum(bn - bn[s:s + 1, :], 0.0)) * kf[r0 + s:r0 + s + 1, :]
                    a = jnp.sum(xs, axis=-1, keepdims=True)
                    on = on + jnp.where(rowi >= s, a, 0.0) * v[r0 + s:r0 + s + 1, :]
                o_blocks.append(on)
            diag.append(o_blocks)

        for (rs, h, hs, q, kf, v, b_blocks, b, a_prev), o_blocks in zip(streams, diag):
            st = st_s[h]
            for n in range(1, len(b_blocks)):
                o_blocks[n] = o_blocks[n] + _dot(a_prev[n].astype(BF16), v[:n * blk, :].astype(BF16))
            o = cat(o_blocks) + _dot_nt((q * jnp.exp(b)).astype(BF16), st.astype(BF16))
            bl = b_blocks[-1][blk - 1:blk, :]
            kd_ = kf * jnp.exp(bl - b)
            st_s[h] = st * jnp.exp(bl) + _dot_tn(v.astype(BF16), kd_.astype(BF16))
            y = _rms(o, nw_ref[:, hs]) * gc_ref[0, rs, hs]
            y_ref[0, rs, hs] = y.astype(y_ref.dtype)
        return carry

    if nsub == 1:
        body(0, 0)
    else:
        lax.fori_loop(0, nsub // group, body, 0)

    @pl.when(ti == pl.num_programs(1) - 1)
    def _():
        for h in range(nh):
            sn_ref[0, h] = st_s[h].T


def _hgrn(qs, gl, kf, iv, gcs, nw, s0_all, layer):
    b, t, w = qs.shape
    _, _, nh, kd, vd = s0_all.shape
    c = HGRN_SUB if t % HGRN_SUB == 0 else t
    tt = HGRN_TILE if t % HGRN_TILE == 0 else t
    kern = functools.partial(_hgrn_kernel, c=c, nsub=tt // c, kd=kd)
    seq = pl.BlockSpec((1, tt, w), lambda i, j: (i, j, 0))
    state = pl.BlockSpec((1, nh, kd, vd), lambda i, j: (i, 0, 0, 0))
    state_in = pl.BlockSpec((None, 1, nh, kd, vd), lambda i, j: (layer, i, 0, 0, 0))
    return pl.pallas_call(
        kern, grid=(b, t // tt),
        in_specs=[seq, seq, seq, seq, seq, _const_spec(nw.shape), state_in],
        out_specs=(seq, state),
        out_shape=(jax.ShapeDtypeStruct((b, t, nh * vd), BF16), jax.ShapeDtypeStruct((b, nh, kd, vd), F32)),
        scratch_shapes=[pltpu.VMEM((nh, vd, kd), F32)],
        compiler_params=_params("parallel", "arbitrary"), name="hgrn",
    )(qs, gl, kf, iv, gcs, nw, s0_all)


def _merge_kernel(x_ref, ya_ref, ub_ref, vb_ref, yc_ref, gt_ref, gm_ref, gb_ref,
                  woa_ref, wob_ref, woc_ref, wo_ref, o_ref, *, n_groups):
    tm = x_ref.shape[0]
    d = x_ref.shape[1]
    cn = gm_ref.shape[1]
    gw = ub_ref.shape[1] // n_groups
    yb_rows = []
    for ci in range(tm // cn):
        rs = slice(ci * cn, (ci + 1) * cn)
        vb = vb_ref[rs, :].astype(BF16)
        gb = gb_ref[...]
        parts = []
        for g in range(n_groups):
            parts.append(_dot(gm_ref[g], vb[:, g * gw:(g + 1) * gw]) + gb[:, g:g + 1])
        yb_rows.append(ub_ref[rs, :] * jnp.concatenate(parts, axis=1))
    yb = jnp.concatenate(yb_rows, axis=0) if len(yb_rows) > 1 else yb_rows[0]
    gt = gt_ref[...]
    merged = (gt[:, :d] * _dot(ya_ref[...].astype(BF16), woa_ref[...])
              + gt[:, d:2 * d] * _dot(yb.astype(BF16), wob_ref[...])
              + gt[:, 2 * d:] * _dot(yc_ref[...].astype(BF16), woc_ref[...]))
    o_ref[...] = x_ref[...] + _dot(merged.astype(BF16), wo_ref[...])


def _merge(x, ya, ub, vb, yc, gt, gm, gb, woa, wob, woc, wo):
    n, d = x.shape
    tm = min(ROW_TILE, n)

    def rows(w):
        return pl.BlockSpec((tm, w), lambda i: (i, 0))

    kern = functools.partial(_merge_kernel, n_groups=gm.shape[0])
    return pl.pallas_call(
        kern, grid=(n // tm,),
        in_specs=[rows(d), rows(ya.shape[1]), rows(ub.shape[1]), rows(vb.shape[1]), rows(yc.shape[1]),
                  rows(gt.shape[1]), _const_spec(gm.shape), _const_spec(gb.shape),
                  _const_spec(woa.shape), _const_spec(wob.shape), _const_spec(woc.shape),
                  _const_spec(wo.shape)],
        out_specs=rows(d), out_shape=jax.ShapeDtypeStruct((n, d), F32),
        compiler_params=_params("parallel"), name="merge",
    )(x, ya, ub, vb, yc, gt, gm, gb, woa, wob, woc, wo)


def _ffn_kernel(x_ref, n2_ref, wup_ref, wcv_ref, bcv_ref, wdn_ref, buf_ref, nf_ref,
                y_ref, nb_ref, carry_s, ext_s, h_s, act_s, *, shift, base, cw, final_norm):
    ti = pl.program_id(1)
    tm = x_ref.shape[1]
    d_ff = wdn_ref.shape[0]
    hist = 2 * shift

    @pl.when(ti == 0)
    def _():
        carry_s[...] = buf_ref[0]

    n_chunks = d_ff // cw
    n_slots = ext_s.shape[0]

    def col0(c, half):
        return half * d_ff + c * cw

    def stage(c):
        for half in range(2):
            cols = slice(col0(c, half), col0(c, half) + cw)
            ext = ext_s.at[(2 * c + half) % n_slots]
            ext[base - hist:base, :] = carry_s[:, cols]
            ext[base:base + tm, :] = _dot(h_s[...], wup_ref[:, cols])

    def conv(c, half):
        cols = slice(col0(c, half), col0(c, half) + cw)
        ext = ext_s.at[(2 * c + half) % n_slots]
        wcv = wcv_ref[:, cols]
        out = bcv_ref[:, cols] + (wcv[0:1, :] * ext[base - hist:base - hist + tm, :]
                                  + wcv[1:2, :] * ext[base - shift:base - shift + tm, :]
                                  + wcv[2:3, :] * ext[base:base + tm, :])
        carry_s[:, cols] = ext[base + tm - hist:base + tm, :]
        return out

    x = x_ref[0]
    h_s[...] = _rms(x, n2_ref[...]).astype(BF16)
    y = x
    stage(0)
    for c in range(n_chunks):
        if c + 1 < n_chunks:
            stage(c + 1)
        act_s[:, c * cw:(c + 1) * cw] = (_silu(conv(c, 0)) * conv(c, 1)).astype(BF16)
        if (c + 1) % FFN_DOWN_CHUNKS == 0 or c + 1 == n_chunks:
            r0 = (c // FFN_DOWN_CHUNKS) * FFN_DOWN_CHUNKS * cw
            y = y + _dot(act_s[:, r0:(c + 1) * cw], wdn_ref[r0:(c + 1) * cw, :])
    if final_norm:
        y = _rms(y, nf_ref[...])
    y_ref[0] = y
    nb_ref[0] = carry_s[...]


def _ffn(x, n2, wup, wcv, bcv, wdn, buf, nf, shift, final_norm):
    nseq, t, d = x.shape
    tm = min(ROW_TILE, t)
    hist = 2 * shift
    base = -(-hist // SUBLANES) * SUBLANES
    kern = functools.partial(_ffn_kernel, shift=shift, base=base, cw=FFN_COLS, final_norm=final_norm)
    state = pl.BlockSpec((1, hist, buf.shape[2]), lambda i, j: (i, 0, 0))
    return pl.pallas_call(
        kern, grid=(nseq, t // tm),
        in_specs=[pl.BlockSpec((1, tm, d), lambda i, j: (i, j, 0)), _const_spec(n2.shape),
                  _const_spec(wup.shape), _const_spec(wcv.shape), _const_spec(bcv.shape),
                  _const_spec(wdn.shape), state, _const_spec(nf.shape)],
        out_specs=(pl.BlockSpec((1, tm, d), lambda i, j: (i, j, 0)), state),
        out_shape=(jax.ShapeDtypeStruct(x.shape, F32), jax.ShapeDtypeStruct(buf.shape, F32)),
        scratch_shapes=[pltpu.VMEM((hist, buf.shape[2]), F32),
                        pltpu.VMEM((FFN_EXT_SLOTS, base + tm, FFN_COLS), F32),
                        pltpu.VMEM((tm, d), BF16), pltpu.VMEM((tm, wdn.shape[0]), BF16)],
        compiler_params=_params("parallel", "arbitrary"), name="conv_ffn",
    )(x, n2, wup, wcv, bcv, wdn, buf, nf)


def kernel(x_prompt, x_sample, cache_k, cache_v, cache_logf, state_hgrn, state_conv, page_table,
           norm1, w_in, b_f, gmlp_norm, w_s, b_s, hgrn_norm, hgrn_lb, w_oa, w_ob, w_oc, w_o,
           norm2, w_up, w_conv, b_conv, w_down, norm_f):
    depth = w_in.shape[0]
    bp, tp, d = x_prompt.shape
    bs, ts, _ = x_sample.shape
    _, n_phys, page, nh_a, hd_a = cache_k.shape
    w_a = nh_a * hd_a
    n_grp, chunk_b, _ = w_s.shape[1:]
    w_b = gmlp_norm.shape[1]
    _, _, nh_c, k_c, v_c = state_hgrn.shape
    w_c = nh_c * v_c
    d_ff = w_down.shape[1]

    sm = jax.nn.softmax(hgrn_lb.astype(F32), axis=0)
    lower = jnp.cumsum(sm, axis=0) - sm[:1]

    cache_kt = jnp.transpose(cache_k, (0, 1, 3, 4, 2))
    cache_vt = jnp.transpose(cache_v, (0, 1, 3, 4, 2))
    cache_ft = jnp.swapaxes(cache_logf, 2, 3)

    xp = x_prompt.reshape(bp * tp, d)
    xs = x_sample.reshape(bs * ts, d)
    st_p, st_s = [], []
    kv_bufs = None
    for l in range(depth):
        o = 0
        wl = jnp.swapaxes(w_in[l], 0, 1)
        wa = wl[o:o + 3 * w_a]; o += 3 * w_a
        wf = wl[o:o + nh_a]; o += nh_a
        wb = wl[o:o + 2 * w_b]; o += 2 * w_b
        wc = wl[o:o + 4 * w_c]; o += 4 * w_c
        wg = wl[o:]
        qscale = jnp.concatenate([jnp.full((w_a, 1), hd_a ** -0.5 * LOG2E, F32), jnp.ones((2 * w_a, 1), F32)])
        wa = (wa * qscale).astype(BF16)
        wf_pad = jnp.pad(wf, ((0, LANES - nh_a), (0, 0))).astype(BF16)
        bf_row = jnp.pad(b_f[l], (0, LANES - nh_a)).reshape(1, LANES)
        inproj_w = (norm1[l].reshape(1, d), wa, wf_pad, wb.astype(BF16), wc.astype(BF16),
                    wg.astype(BF16), bf_row, b_f[l].reshape(nh_a, 1), gmlp_norm[l].reshape(1, w_b),
                    lower[l].reshape(1, w_c))

        tril_w = jnp.where(jnp.tril(jnp.ones((chunk_b, chunk_b), bool)), w_s[l], 0)
        woa, wob, woc, wo = (w_oa[l].astype(BF16), w_ob[l].astype(BF16), w_oc[l].astype(BF16),
                             w_o[l].astype(BF16))
        nw = hgrn_norm[l].reshape(1, w_c)

        n2 = norm2[l].reshape(1, d)
        wup = w_up[l].astype(BF16)
        wcv = w_conv[l]
        bcv = b_conv[l].reshape(1, 2 * d_ff)
        wdn = w_down[l].astype(BF16)
        nf = norm_f.reshape(1, d)
        last = l == depth - 1

        (q, k, kt_all, vt_all, lf, lft_all, ub, vb, qs, gl, kf, iv, gcs, gt) = _inproj(
            xp, *inproj_w, stacked=(l, depth, bp, kv_bufs))
        kv_bufs = (kt_all, vt_all, lft_all)
        ya = _fox_prompt(q.reshape(bp, tp, w_a), k.reshape(bp, tp, w_a), vt_all, l,
                         lf.reshape(bp, tp, LANES), nh_a)
        s0 = jnp.zeros((1, bp, nh_c, k_c, v_c), F32)
        yc, s_new = _hgrn(*(a.reshape(bp, tp, w_c) for a in (qs, gl, kf, iv, gcs)), nw, s0, 0)
        x1 = _merge(xp, ya.reshape(bp * tp, w_a), ub, vb, yc.reshape(bp * tp, w_c), gt,
                    tril_w.astype(BF16), b_s[l].T, woa, wob, woc, wo)
        buf0 = jnp.zeros((bp, 2, 2 * d_ff), F32)
        x2, conv_p = _ffn(x1.reshape(bp, tp, d), n2, wup, wcv, bcv, wdn, buf0, nf, 1, last)
        xp = x2.reshape(bp * tp, d)
        st_p.append((s_new, conv_p))

        (q, k, v, lf, ub, vb, qs, gl, kf, iv, gcs, gt) = _inproj(xs, *inproj_w)
        k4 = k.reshape(bs, ts, nh_a, hd_a)
        v4 = v.reshape(bs, ts, nh_a, hd_a)
        lf3 = lf[:, :nh_a].reshape(bs, ts, nh_a)
        ya = _fox_sample(q.astype(F32), k, v, jnp.swapaxes(lf3, 1, 2),
                         cache_kt, cache_vt, cache_ft, page_table, l)
        yc, s_new = _hgrn(*(a.reshape(bs, ts, w_c) for a in (qs, gl, kf, iv, gcs)), nw, state_hgrn, l)
        same_seq = np.kron(np.eye(bs, dtype=np.float32), np.ones((ts, ts), np.float32))
        rep = np.tile(np.eye(ts, dtype=np.float32), (bs, 1))
        gm = (jnp.einsum("rt,gts,cs->grc", rep, tril_w[:, :ts, :ts], rep) * same_seq).astype(BF16)
        gb = jnp.tile(b_s[l][:, :ts].T, (bs, 1))
        x1 = _merge(xs, ya, ub, vb, yc.reshape(bs * ts, w_c), gt, gm, gb, woa, wob, woc, wo)
        x1t = jnp.swapaxes(x1.reshape(bs, ts, d), 0, 1).reshape(1, ts * bs, d)
        buf_t = jnp.swapaxes(state_conv[l], 0, 1).reshape(1, 2 * bs, 2 * d_ff)
        x2t, nb = _ffn(x1t, n2, wup, wcv, bcv, wdn, buf_t, nf, bs, last)
        xs = jnp.swapaxes(x2t.reshape(ts, bs, d), 0, 1).reshape(bs * ts, d)
        conv_s = nb.reshape(2, bs, 2 * d_ff)
        st_s.append((k4, v4, lf3, vb.reshape(bs, ts, w_b), s_new, jnp.swapaxes(conv_s, 0, 1)))

    def stk(sts, i):
        return jnp.stack([st[i] for st in sts], axis=0)

    kt_all, vt_all, lft_all = kv_bufs
    k_prompt = jnp.transpose(kt_all.reshape(depth, bp, nh_a, hd_a, tp), (0, 1, 4, 2, 3))
    v_prompt = jnp.transpose(vt_all.reshape(depth, bp, nh_a, hd_a, tp), (0, 1, 4, 2, 3))
    return (xp.reshape(bp, tp, d), xs.reshape(bs, ts, d),
            k_prompt, v_prompt, jnp.swapaxes(lft_all, 2, 3), stk(st_p, 0), stk(st_p, 1),
            stk(st_s, 0), stk(st_s, 1), stk(st_s, 2), stk(st_s, 3), stk(st_s, 4), stk(st_s, 5))
```

```python
import functools

import numpy as np

import jax
import jax.numpy as jnp
from jax import lax
from jax.experimental import pallas as pl
from jax.experimental.pallas import tpu as pltpu

F32 = jnp.float32
BF16 = jnp.bfloat16
EPS = 1e-6
NEG_INF = float("-inf")
LOG2E = 1.4426950408889634

LANES = 128
SUBLANES = 8
VMEM_LIMIT_BYTES = 56 * 1024 * 1024

ROW_TILE = 256
MERGE_TILE = 512
FFN_TILE = 256
ATTN_TILE = 256
ATTN_HEADS_PER_PHASE = 8
PAGES_PER_STEP = 16
HGRN_SUB = 16
HGRN_TILE = 256
HGRN_GROUP = 8
FFN_COLS = 256
FFN_EXT_SLOTS = 4
FFN_DOWN_CHUNKS = 11
N_SPLIT = 3


def _dot(a, b):
    return jnp.dot(a, b, preferred_element_type=F32)


def _dot_nt(a, b):
    return lax.dot_general(a, b, (((1,), (1,)), ((), ())), preferred_element_type=F32)


def _dot_tn(a, b):
    return lax.dot_general(a, b, (((0,), (0,)), ((), ())), preferred_element_type=F32)


def _rms(x, g):
    return x * lax.rsqrt(jnp.mean(x * x, axis=-1, keepdims=True) + EPS) * g


def _sigmoid(x):
    return 1.0 / (1.0 + jnp.exp(-x))


def _silu(x):
    return x * _sigmoid(x)


def _gelu_tanh(x):
    c = 0.7978845608028654
    return x * (0.5 * (1.0 + jnp.tanh(c * (x + 0.044715 * (x * x * x)))))


def _log_sigmoid(z):
    return jnp.minimum(z, 0.0) - jnp.log1p(jnp.exp(-jnp.abs(z)))


def _split3(x):
    hi = x.astype(BF16)
    r = x - hi.astype(F32)
    mid = r.astype(BF16)
    lo = (r - mid.astype(F32)).astype(BF16)
    return hi, mid, lo


def _ones_tri(n, upper):
    r = lax.broadcasted_iota(jnp.int32, (n, n), 0)
    c = lax.broadcasted_iota(jnp.int32, (n, n), 1)
    keep = (r <= c) if upper else (r >= c)
    return jnp.where(keep, 1.0, 0.0).astype(BF16)


def _const_spec(shape):
    nd = len(shape)
    return pl.BlockSpec(shape, lambda *_: (0,) * nd, pipeline_mode=pl.Buffered(1))


def _params(*sem):
    return pltpu.CompilerParams(dimension_semantics=sem, vmem_limit_bytes=VMEM_LIMIT_BYTES)


def _inproj_kernel(x_ref, n1_ref, wa_ref, wf_ref, wb_ref, wc_ref, wg_ref, bf_ref, bfc_ref, gn_ref, lb_ref,
                   *rest, n_alias, seq_major):
    outs = rest[n_alias:]
    if seq_major:
        q_ref, k_ref, kt_ref, vt_ref, lf_ref, lft_ref = outs[:6]
        ub_ref, vb_ref, qs_ref, gl_ref, kf_ref, iv_ref, gc_ref, gt_ref = outs[6:]
    else:
        q_ref, k_ref, v_ref, lf_ref = outs[:4]
        ub_ref, vb_ref, qs_ref, gl_ref, kf_ref, iv_ref, gc_ref, gt_ref = outs[4:]
    wa_w = k_ref.shape[-1]
    wb_w = ub_ref.shape[-1]
    wc_w = qs_ref.shape[-1]
    h = _rms(x_ref[...], n1_ref[...]).astype(BF16)

    if seq_major:
        pa = _dot_nt(h, wa_ref[:2 * wa_w, :])
        k_ref[...] = pa[:, wa_w:].astype(k_ref.dtype)
        kv_t = _dot_nt(wa_ref[wa_w:, :], h)
        kt_ref[...] = kv_t[:wa_w, :]
        vt_ref[...] = kv_t[wa_w:, :]
        lft = _log_sigmoid(_dot_nt(wf_ref[...], h)[:bfc_ref.shape[0], :] + bfc_ref[...])
        lft_ref[...] = lft
    else:
        pa = _dot_nt(h, wa_ref[...])
        k_ref[...] = pa[:, wa_w:2 * wa_w]
        v_ref[...] = pa[:, 2 * wa_w:]
    q_ref[...] = pa[:, :wa_w].astype(q_ref.dtype)

    lf_ref[...] = _log_sigmoid(_dot_nt(h, wf_ref[...]) + bf_ref[...])

    pb = _dot_nt(h, wb_ref[...])
    ub_ref[...] = _gelu_tanh(pb[:, :wb_w])
    vb_ref[...] = _rms(_gelu_tanh(pb[:, wb_w:]), gn_ref[...])

    pc = _dot_nt(h, wc_ref[...])
    lb = lb_ref[...]
    qs_ref[...] = _silu(pc[:, :wc_w])
    f = lb + (1.0 - lb) * _sigmoid(pc[:, wc_w:2 * wc_w])
    gl_ref[...] = jnp.log(f)
    kf_ref[...] = 1.0 - f
    iv_ref[...] = pc[:, 2 * wc_w:3 * wc_w]
    gc_ref[...] = _silu(pc[:, 3 * wc_w:])

    gt_ref[...] = _sigmoid(_dot_nt(h, wg_ref[...]))


def _inproj(x, n1, wa, wf, wb, wc, wg, bf_row, bf_col, gn, lb, stacked=None):
    n, d = x.shape
    tm = min(ROW_TILE, n)
    wa_w, wb_w, wc_w = wa.shape[0] // 3, wb.shape[0] // 2, wc.shape[0] // 4
    nh = bf_col.shape[0]

    def rows(w):
        return pl.BlockSpec((tm, w), lambda i: (i, 0))

    q_shape = jax.ShapeDtypeStruct((n, wa_w), BF16)
    lf_shape = jax.ShapeDtypeStruct((n, LANES), F32)
    extra_in, extra_specs, aliases = [], [], {}
    if stacked is None:
        head_shape = (q_shape, jax.ShapeDtypeStruct((n, wa_w), F32), jax.ShapeDtypeStruct((n, wa_w), F32),
                      lf_shape)
        head_specs = (rows(wa_w), rows(wa_w), rows(wa_w), rows(LANES))
    else:
        layer, depth, b, prev = stacked
        t = n // b
        tps = t // tm

        def feat_major(f):
            return (jax.ShapeDtypeStruct((depth, b, f, t), F32),
                    pl.BlockSpec((None, None, f, tm), lambda i: (layer, i // tps, 0, i % tps)))

        (kt_shape, kt_spec), (lft_shape, lft_spec) = feat_major(wa_w), feat_major(nh)
        head_shape = (q_shape, jax.ShapeDtypeStruct((n, wa_w), BF16), kt_shape, kt_shape, lf_shape, lft_shape)
        head_specs = (rows(wa_w), rows(wa_w), kt_spec, kt_spec, rows(LANES), lft_spec)
        if prev is not None:
            extra_in = list(prev)
            extra_specs = [pl.BlockSpec(memory_space=pl.ANY)] * len(prev)
            aliases = {11: 2, 12: 3, 13: 5}

    out_shape = head_shape + (
        jax.ShapeDtypeStruct((n, wb_w), F32),
        jax.ShapeDtypeStruct((n, wb_w), F32),
        jax.ShapeDtypeStruct((n, wc_w), F32),
        jax.ShapeDtypeStruct((n, wc_w), F32),
        jax.ShapeDtypeStruct((n, wc_w), F32),
        jax.ShapeDtypeStruct((n, wc_w), F32),
        jax.ShapeDtypeStruct((n, wc_w), F32),
        jax.ShapeDtypeStruct((n, wg.shape[0]), F32),
    )
    out_specs = head_specs + (rows(wb_w), rows(wb_w), rows(wc_w), rows(wc_w), rows(wc_w), rows(wc_w),
                              rows(wc_w), rows(wg.shape[0]))
    in_specs = [rows(d), _const_spec(n1.shape), _const_spec(wa.shape), _const_spec(wf.shape),
                _const_spec(wb.shape), _const_spec(wc.shape), _const_spec(wg.shape),
                _const_spec(bf_row.shape), _const_spec(bf_col.shape), _const_spec(gn.shape),
                _const_spec(lb.shape)] + extra_specs
    kern = functools.partial(_inproj_kernel, n_alias=len(extra_in), seq_major=stacked is not None)
    return pl.pallas_call(
        kern, grid=(n // tm,), in_specs=in_specs, out_specs=out_specs, out_shape=out_shape,
        input_output_aliases=aliases, compiler_params=_params("parallel"), name="inproj",
    )(x, n1, wa, wf, wb, wc, wg, bf_row, bf_col, gn, lb, *extra_in)


def _decay_selectors(nh, hd):
    sel_q = np.zeros((N_SPLIT * LANES, nh * hd), np.float32)
    sel_k = np.zeros((N_SPLIT * LANES, nh * hd), np.float32)
    one_q = np.zeros((1, nh * hd), np.float32)
    one_k = np.zeros((1, nh * hd), np.float32)
    for h in range(nh):
        for p in range(N_SPLIT):
            sel_k[p * LANES + h, h * hd + p] = -1.0
            one_k[0, h * hd + N_SPLIT + p] = 1.0
            one_q[0, h * hd + p] = 1.0
            sel_q[p * LANES + h, h * hd + N_SPLIT + p] = 1.0
    return (jnp.asarray(sel_q, BF16), jnp.asarray(sel_k, BF16), jnp.asarray(one_q), jnp.asarray(one_k))


def _fox_prep_kernel(q_ref, k_ref, vin_ref, lf_ref, selq_ref, selk_ref, oneq_ref, onek_ref,
                     qa_ref, ka_ref, vt_ref, carry_s):
    ti = pl.program_id(1)
    tm = q_ref.shape[1]
    w = q_ref.shape[2]

    @pl.when(ti == 0)
    def _():
        carry_s[...] = jnp.zeros_like(carry_s)

    lower = _ones_tri(tm, upper=False)
    hi, mid, lo = _split3(lf_ref[0])
    c = _dot(lower, hi) + _dot(lower, mid) + _dot(lower, lo) + carry_s[...]
    carry_s[...] = c[tm - 1:tm, :]
    c3 = jnp.concatenate(_split3(c * LOG2E), axis=1)
    eq = _dot(c3, selq_ref[...]) + oneq_ref[...]
    ek = _dot(c3, selk_ref[...]) + onek_ref[...]
    q = q_ref[0].astype(F32)
    k = k_ref[0].astype(F32)
    qa_parts, ka_parts = [], []
    for p in range(w // LANES):
        ls = slice(p * LANES, (p + 1) * LANES)
        qa_parts += [q[:, ls], eq[:, ls]]
        ka_parts += [k[:, ls], ek[:, ls]]
    qa_ref[0] = jnp.concatenate(qa_parts, axis=1).astype(BF16)
    ka_ref[0] = jnp.concatenate(ka_parts, axis=1).astype(BF16)
    vt_ref[0, 0] = vin_ref[...].astype(BF16)


def _fox_prompt_kernel(qa_ref, ka_ref, vt_ref, o_ref, qh_s, m_s, l_s, acc_s, *, hd, tq):
    qi = pl.program_id(1)
    heads_per_slab = LANES // hd
    n_slab = vt_ref.shape[2] // LANES
    heads = [(sl, hh) for sl in range(n_slab) for hh in range(heads_per_slab)]
    lane2 = lax.broadcasted_iota(jnp.int32, (tq, 2 * LANES), 1) % LANES
    key = lax.broadcasted_iota(jnp.int32, (tq, tq), 0)
    qry = lax.broadcasted_iota(jnp.int32, (tq, tq), 1)

    for n, (sl, hh) in enumerate(heads):
        qa = qa_ref[0, :, sl * 2 * LANES:(sl + 1) * 2 * LANES].astype(F32)
        qh_s[n] = jnp.where(lane2 // hd == hh, qa, 0.0).astype(BF16)
        m_s[n] = jnp.full((1, tq), NEG_INF, F32)
        l_s[n] = jnp.zeros((1, tq), F32)
        acc_s[n] = jnp.zeros((LANES, tq), F32)

    def tile(j, masked):
        k0 = pl.multiple_of(j * tq, tq)
        for g0 in range(0, len(heads), ATTN_HEADS_PER_PHASE):
            grp = list(enumerate(heads))[g0:g0 + ATTN_HEADS_PER_PHASE]
            s_all = []
            for n, (sl, _) in grp:
                s = _dot_nt(ka_ref[0, pl.ds(k0, tq), sl * 2 * LANES:(sl + 1) * 2 * LANES], qh_s[n])
                s_all.append(jnp.where(key <= qry, s, NEG_INF) if masked else s)
            m_old = [m_s[n] for n, _ in grp]
            m_new = [jnp.maximum(m, jnp.max(s, axis=0, keepdims=True)) for m, s in zip(m_old, s_all)]
            p_all = [jnp.exp2(s - m) for s, m in zip(s_all, m_new)]
            for i, (n, (sl, _)) in enumerate(grp):
                alpha = jnp.exp2(m_old[i] - m_new[i])
                l_s[n] = alpha * l_s[n] + jnp.sum(p_all[i], axis=0, keepdims=True)
                pv = _dot(vt_ref[0, j, sl * LANES:(sl + 1) * LANES, :], p_all[i].astype(BF16))
                acc_s[n] = alpha * acc_s[n] + pv
                m_s[n] = m_new[i]

    def body(j, carry):
        tile(j, False)
        return carry

    lax.fori_loop(0, qi, body, 0)
    tile(qi, True)
    for sl in range(n_slab):
        parts = [(acc_s[n] / l_s[n])[hh * hd:(hh + 1) * hd, :] for n, (s2, hh) in enumerate(heads) if s2 == sl]
        o_ref[0, :, sl * LANES:(sl + 1) * LANES] = jnp.concatenate(parts, axis=0).T.astype(o_ref.dtype)


def _fox_prompt(q, k, vt_all, layer, lf, nh):
    b, t, w = k.shape
    hd = w // nh
    tq = min(ATTN_TILE, t)
    nkv = t // tq
    sel_q, sel_k, one_q, one_k = _decay_selectors(nh, hd)
    blk = lambda width: pl.BlockSpec((1, tq, width), lambda i, j: (i, j, 0))
    qa, ka, vt = pl.pallas_call(
        _fox_prep_kernel, grid=(b, t // tq),
        in_specs=[blk(w), blk(w), pl.BlockSpec((None, None, w, tq), lambda i, j: (layer, i, 0, j)),
                  blk(LANES), _const_spec(sel_q.shape), _const_spec(sel_k.shape),
                  _const_spec(one_q.shape), _const_spec(one_k.shape)],
        out_specs=(blk(2 * w), blk(2 * w), pl.BlockSpec((1, 1, w, tq), lambda i, j: (i, j, 0, 0))),
        out_shape=(jax.ShapeDtypeStruct((b, t, 2 * w), BF16), jax.ShapeDtypeStruct((b, t, 2 * w), BF16),
                   jax.ShapeDtypeStruct((b, nkv, w, tq), BF16)),
        scratch_shapes=[pltpu.VMEM((1, LANES), F32)],
        compiler_params=_params("parallel", "arbitrary"), name="fox_prep",
    )(q, k, vt_all, lf, sel_q, sel_k, one_q, one_k)
    kern = functools.partial(_fox_prompt_kernel, hd=hd, tq=tq)
    return pl.pallas_call(
        kern, grid=(b, nkv),
        in_specs=[pl.BlockSpec((1, tq, 2 * w), lambda i, j: (i, j, 0)),
                  pl.BlockSpec((1, t, 2 * w), lambda i, j: (i, 0, 0)),
                  pl.BlockSpec((1, nkv, w, tq), lambda i, j: (i, 0, 0, 0))],
        out_specs=pl.BlockSpec((1, tq, w), lambda i, j: (i, j, 0)),
        out_shape=jax.ShapeDtypeStruct((b, t, w), BF16),
        scratch_shapes=[pltpu.VMEM((nh, tq, 2 * LANES), BF16), pltpu.VMEM((nh, 1, tq), F32),
                        pltpu.VMEM((nh, 1, tq), F32), pltpu.VMEM((nh, LANES, tq), F32)],
        compiler_params=_params("parallel", "arbitrary"), name="fox_prompt",
    )(qa, ka, vt)


def _fox_sample_kernel(pt_ref, q_ref, kn_ref, vn_ref, lnt_ref, *rest, nh, hd, g_pages):
    k_pages = rest[:g_pages]
    v_pages = rest[g_pages:2 * g_pages]
    f_pages = rest[2 * g_pages:3 * g_pages]
    o_ref, qbd_s, m_s, l_s, acc_s, carry_s, cn_s = rest[3 * g_pages:]
    del pt_ref
    step = pl.program_id(1)
    nt = q_ref.shape[0]
    w = q_ref.shape[1]
    page = k_pages[0].shape[-1]
    rows = nt * nh

    def update(s3, vs):
        m_old = m_s[...]
        m_new = jnp.maximum(m_old, jnp.max(s3, axis=-1, keepdims=True))
        alpha = jnp.exp2(m_old - m_new)
        p3 = jnp.exp2(s3 - m_new)
        l_s[...] = alpha * l_s[...] + jnp.sum(p3, axis=-1, keepdims=True)
        p = p3.reshape(rows, p3.shape[-1]).astype(BF16)
        pv = None
        off = 0
        for vt in vs:
            part = _dot_nt(p[:, off:off + vt.shape[1]], vt)
            pv = part if pv is None else pv + part
            off += vt.shape[1]
        acc_s[...] = alpha * acc_s[...] + pv.reshape(nt, nh, w)
        m_s[...] = m_new

    @pl.when(step == 0)
    def _():
        rep_r = lax.broadcasted_iota(jnp.int32, (rows, nt), 0) // nh
        rep_c = lax.broadcasted_iota(jnp.int32, (rows, nt), 1)
        q_rep = _dot(jnp.where(rep_r == rep_c, 1.0, 0.0), q_ref[...])
        head_of_lane = lax.broadcasted_iota(jnp.int32, (rows, w), 1) // hd
        head_of_row = lax.broadcasted_iota(jnp.int32, (rows, w), 0) % nh
        qbd = jnp.where(head_of_lane == head_of_row, q_rep, 0.0).astype(BF16)
        qbd_s[...] = qbd

        lane = lax.broadcasted_iota(jnp.int32, (nh, page), 1)
        cn = jnp.concatenate([lnt_ref[...] * LOG2E, jnp.zeros((nh, page - nt), F32)], axis=1)
        sh = 1
        while sh < nt:
            cn = cn + jnp.where(lane >= sh, pltpu.roll(cn, sh, 1), 0.0)
            sh *= 2
        cn_s[...] = cn
        carry_s[...] = jnp.zeros_like(carry_s)
        m_s[...] = jnp.full(m_s.shape, NEG_INF, F32)
        l_s[...] = jnp.zeros_like(l_s)
        acc_s[...] = jnp.zeros_like(acc_s)

        kn = jnp.concatenate([kn_ref[...], jnp.zeros((page - nt, w), F32)], axis=0).astype(BF16)
        vn = jnp.concatenate([vn_ref[...], jnp.zeros((page - nt, w), F32)], axis=0).T.astype(BF16)
        s3 = _dot_nt(qbd, kn).reshape(nt, nh, page)
        cq = jnp.stack([cn[:, t:t + 1] for t in range(nt)], axis=0)
        s3 = s3 + (cq - cn[None, :, :])
        tk = lax.broadcasted_iota(jnp.int32, (nt, nh, page), 2)
        tq = lax.broadcasted_iota(jnp.int32, (nt, nh, page), 0)
        update(jnp.where(tk <= tq, s3, NEG_INF), [vn])

    lane = lax.broadcasted_iota(jnp.int32, (nh, page), 1)
    cn = cn_s[...]
    cq = jnp.stack([cn[:, t:t + 1] for t in range(nt)], axis=0)
    qbd = qbd_s[...]
    carry = carry_s[...]
    s_tiles, v_tiles = [], []
    for g in range(g_pages):
        x = f_pages[g][...] * LOG2E
        incl = x
        sh = 1
        while sh < page:
            incl = incl + jnp.where(lane + sh < page, pltpu.roll(incl, page - sh, 1), 0.0)
            sh *= 2
        r = carry + (incl - x)
        carry = carry + incl[:, 0:1]
        s3 = _dot(qbd, k_pages[g][...].reshape(w, page).astype(BF16)).reshape(nt, nh, page)
        s_tiles.append(s3 + (cq + r[None, :, :]))
        v_tiles.append(v_pages[g][...].reshape(w, page).astype(BF16))
    carry_s[...] = carry
    update(jnp.concatenate(s_tiles, axis=-1), v_tiles)

    @pl.when(step == pl.num_programs(1) - 1)
    def _():
        head_of_lane = lax.broadcasted_iota(jnp.int32, (nt, nh, w), 2) // hd
        head_of_row = lax.broadcasted_iota(jnp.int32, (nt, nh, w), 1)
        o3 = jnp.where(head_of_lane == head_of_row, acc_s[...] / l_s[...], 0.0)
        o2 = o3.reshape(rows, w).astype(BF16)
        sel_r = lax.broadcasted_iota(jnp.int32, (nt, rows), 0)
        sel_c = lax.broadcasted_iota(jnp.int32, (nt, rows), 1) // nh
        sel = jnp.where(sel_r == sel_c, 1.0, 0.0).astype(BF16)
        o_ref[...] = _dot(sel, o2).astype(o_ref.dtype)


def _fox_sample(q, kn, vn, lnt, cache_kt, cache_vt, cache_ft, page_table, layer):
    nseq, n_pages = page_table.shape
    n, w = q.shape
    nt = n // nseq
    _, _, nh, hd, page = cache_kt.shape
    g_pages = min(PAGES_PER_STEP, n_pages)
    n_steps = n_pages // g_pages

    def page_spec(shape, g):
        nd = len(shape)

        def idx(b, s, pt):
            return (layer, pt[b * n_pages + (n_pages - 1 - (s * g_pages + g))]) + (0,) * nd
        return pl.BlockSpec((None, None) + shape, idx)

    seq_rows = pl.BlockSpec((nt, w), lambda b, s, pt: (b, 0))
    in_specs = [seq_rows, seq_rows, seq_rows, pl.BlockSpec((None, nh, nt), lambda b, s, pt: (b, 0, 0))]
    in_specs += [page_spec((nh, hd, page), g) for g in range(g_pages)]
    in_specs += [page_spec((nh, hd, page), g) for g in range(g_pages)]
    in_specs += [page_spec((nh, page), g) for g in range(g_pages)]
    kern = functools.partial(_fox_sample_kernel, nh=nh, hd=hd, g_pages=g_pages)
    grid_spec = pltpu.PrefetchScalarGridSpec(
        num_scalar_prefetch=1, grid=(nseq, n_steps), in_specs=in_specs,
        out_specs=pl.BlockSpec((nt, w), lambda b, s, pt: (b, 0)),
        scratch_shapes=[pltpu.VMEM((nt * nh, w), BF16),
                        pltpu.VMEM((nt, nh, 1), F32), pltpu.VMEM((nt, nh, 1), F32),
                        pltpu.VMEM((nt, nh, w), F32),
                        pltpu.VMEM((nh, page), F32), pltpu.VMEM((nh, page), F32)])
    return pl.pallas_call(
        kern, grid_spec=grid_spec, out_shape=jax.ShapeDtypeStruct((n, w), F32),
        compiler_params=_params("parallel", "arbitrary"), name="fox_sample",
    )(page_table.reshape(-1), q, kn, vn, lnt,
      *([cache_kt] * g_pages), *([cache_vt] * g_pages), *([cache_ft] * g_pages))


def _hgrn_kernel(q_ref, g_ref, kf_ref, v_ref, gc_ref, nw_ref, s0_ref, y_ref, sn_ref, st_s, *, c, nsub, kd):
    ti = pl.program_id(1)
    nh = s0_ref.shape[1]

    @pl.when(ti == 0)
    def _():
        for h in range(nh):
            st_s[h] = s0_ref[0, h].T

    blk = SUBLANES
    group = HGRN_GROUP if nsub % HGRN_GROUP == 0 else 1
    rowi = lax.broadcasted_iota(jnp.int32, (blk, 1), 0)
    subi = lax.broadcasted_iota(jnp.int32, (blk, kd), 0)

    def cumsum_blocks(g):
        b_blocks = []
        off = None
        for i in range(c // blk):
            x = g[i * blk:(i + 1) * blk, :]
            sh = 1
            while sh < blk:
                x = x + jnp.where(subi >= sh, pltpu.roll(x, sh, 0), 0.0)
                sh *= 2
            x = x if off is None else x + off
            off = x[blk - 1:blk, :]
            b_blocks.append(x)
        return b_blocks

    def cat(blocks):
        return jnp.concatenate(blocks, axis=0) if len(blocks) > 1 else blocks[0]

    def body(i, carry):
        streams = []
        for u in range(group):
            rs = pl.ds(0 if nsub == 1 else pl.multiple_of((i * group + u) * c, c), c)
            for h in range(nh):
                hs = slice(h * kd, (h + 1) * kd)
                q, kf, v = q_ref[0, rs, hs], kf_ref[0, rs, hs], v_ref[0, rs, hs]
                b_blocks = cumsum_blocks(g_ref[0, rs, hs])
                b = cat(b_blocks)
                a_prev = [None]
                for n in range(1, len(b_blocks)):
                    e = b_blocks[n - 1][blk - 1:blk, :]
                    a_prev.append(_dot_nt((q[n * blk:(n + 1) * blk, :] * jnp.exp(b_blocks[n] - e)).astype(BF16),
                                          (kf[:n * blk, :] * jnp.exp(e - b[:n * blk, :])).astype(BF16)))
                streams.append((rs, h, hs, q, kf, v, b_blocks, b, a_prev))

        diag = []
        for rs, h, hs, q, kf, v, b_blocks, b, a_prev in streams:
            o_blocks = []
            for n, bn in enumerate(b_blocks):
                r0 = n * blk
                qn = q[r0:r0 + blk, :]
                on = jnp.zeros((blk, kd), F32)
                for s in range(blk):
                    xs = qn * jnp.exp(jnp.minimum(bn - bn[s:s + 1, :], 0.0)) * kf[r0 + s:r0 + s + 1, :]
                    a = jnp.sum(xs, axis=-1, keepdims=True)
                    on = on + jnp.where(rowi >= s, a, 0.0) * v[r0 + s:r0 + s + 1, :]
                o_blocks.append(on)
            diag.append(o_blocks)

        for (rs, h, hs, q, kf, v, b_blocks, b, a_prev), o_blocks in zip(streams, diag):
            st = st_s[h]
            for n in range(1, len(b_blocks)):
                o_blocks[n] = o_blocks[n] + _dot(a_prev[n].astype(BF16), v[:n * blk, :].astype(BF16))
            o = cat(o_blocks) + _dot_nt((q * jnp.exp(b)).astype(BF16), st.astype(BF16))
            bl = b_blocks[-1][blk - 1:blk, :]
            kd_ = kf * jnp.exp(bl - b)
            st_s[h] = st * jnp.exp(bl) + _dot_tn(v.astype(BF16), kd_.astype(BF16))
            y = _rms(o, nw_ref[:, hs]) * gc_ref[0, rs, hs]
            y_ref[0, rs, hs] = y.astype(y_ref.dtype)
        return carry

    if nsub == 1:
        body(0, 0)
    else:
        lax.fori_loop(0, nsub // group, body, 0)

    @pl.when(ti == pl.num_programs(1) - 1)
    def _():
        for h in range(nh):
            sn_ref[0, h] = st_s[h].T


def _hgrn(qs, gl, kf, iv, gcs, nw, s0_all, layer):
    b, t, w = qs.shape
    _, _, nh, kd, vd = s0_all.shape
    c = HGRN_SUB if t % HGRN_SUB == 0 else t
    tt = HGRN_TILE if t % HGRN_TILE == 0 else t
    kern = functools.partial(_hgrn_kernel, c=c, nsub=tt // c, kd=kd)
    seq = pl.BlockSpec((1, tt, w), lambda i, j: (i, j, 0))
    state = pl.BlockSpec((1, nh, kd, vd), lambda i, j: (i, 0, 0, 0))
    state_in = pl.BlockSpec((None, 1, nh, kd, vd), lambda i, j: (layer, i, 0, 0, 0))
    return pl.pallas_call(
        kern, grid=(b, t // tt),
        in_specs=[seq, seq, seq, seq, seq, _const_spec(nw.shape), state_in],
        out_specs=(seq, state),
        out_shape=(jax.ShapeDtypeStruct((b, t, nh * vd), BF16), jax.ShapeDtypeStruct((b, nh, kd, vd), F32)),
        scratch_shapes=[pltpu.VMEM((nh, vd, kd), F32)],
        compiler_params=_params("parallel", "arbitrary"), name="hgrn",
    )(qs, gl, kf, iv, gcs, nw, s0_all)


def _merge_kernel(x_ref, ya_ref, ub_ref, vb_ref, yc_ref, gt_ref, gm_ref, gb_ref,
                  woa_ref, wob_ref, woc_ref, wo_ref, o_ref, *, n_groups):
    tm = x_ref.shape[0]
    d = x_ref.shape[1]
    cn = gm_ref.shape[1]
    gw = ub_ref.shape[1] // n_groups
    yb_rows = []
    for ci in range(tm // cn):
        rs = slice(ci * cn, (ci + 1) * cn)
        vb = vb_ref[rs, :].astype(BF16)
        gb = gb_ref[...]
        parts = []
        for g in range(n_groups):
            parts.append(_dot(gm_ref[g], vb[:, g * gw:(g + 1) * gw]) + gb[:, g:g + 1])
        yb_rows.append(ub_ref[rs, :] * jnp.concatenate(parts, axis=1))
    yb = jnp.concatenate(yb_rows, axis=0) if len(yb_rows) > 1 else yb_rows[0]
    gt = gt_ref[...]
    merged = (gt[:, :d] * _dot(ya_ref[...].astype(BF16), woa_ref[...])
              + gt[:, d:2 * d] * _dot(yb.astype(BF16), wob_ref[...])
              + gt[:, 2 * d:] * _dot(yc_ref[...].astype(BF16), woc_ref[...]))
    o_ref[...] = x_ref[...] + _dot(merged.astype(BF16), wo_ref[...])


def _merge(x, ya, ub, vb, yc, gt, gm, gb, woa, wob, woc, wo):
    n, d = x.shape
    tm = min(MERGE_TILE, n)

    def rows(w):
        return pl.BlockSpec((tm, w), lambda i: (i, 0))

    kern = functools.partial(_merge_kernel, n_groups=gm.shape[0])
    return pl.pallas_call(
        kern, grid=(n // tm,),
        in_specs=[rows(d), rows(ya.shape[1]), rows(ub.shape[1]), rows(vb.shape[1]), rows(yc.shape[1]),
                  rows(gt.shape[1]), _const_spec(gm.shape), _const_spec(gb.shape),
                  _const_spec(woa.shape), _const_spec(wob.shape), _const_spec(woc.shape),
                  _const_spec(wo.shape)],
        out_specs=rows(d), out_shape=jax.ShapeDtypeStruct((n, d), F32),
        compiler_params=_params("parallel"), name="merge",
    )(x, ya, ub, vb, yc, gt, gm, gb, woa, wob, woc, wo)


def _ffn_kernel(x_ref, n2_ref, wup_ref, wcv_ref, bcv_ref, wdn_ref, buf_ref, nf_ref,
                y_ref, nb_ref, carry_s, ext_s, h_s, act_s, *, shift, base, cw, final_norm):
    ti = pl.program_id(1)
    tm = x_ref.shape[1]
    d_ff = wdn_ref.shape[0]
    hist = 2 * shift

    @pl.when(ti == 0)
    def _():
        carry_s[...] = buf_ref[0]

    n_chunks = d_ff // cw
    n_slots = ext_s.shape[0]

    def col0(c, half):
        return half * d_ff + c * cw

    def stage(c):
        for half in range(2):
            cols = slice(col0(c, half), col0(c, half) + cw)
            ext = ext_s.at[(2 * c + half) % n_slots]
            ext[base - hist:base, :] = carry_s[:, cols]
            ext[base:base + tm, :] = _dot(h_s[...], wup_ref[:, cols])

    def conv(c, half):
        cols = slice(col0(c, half), col0(c, half) + cw)
        ext = ext_s.at[(2 * c + half) % n_slots]
        wcv = wcv_ref[:, cols]
        out = bcv_ref[:, cols] + (wcv[0:1, :] * ext[base - hist:base - hist + tm, :]
                                  + wcv[1:2, :] * ext[base - shift:base - shift + tm, :]
                                  + wcv[2:3, :] * ext[base:base + tm, :])
        carry_s[:, cols] = ext[base + tm - hist:base + tm, :]
        return out

    x = x_ref[0]
    h_s[...] = _rms(x, n2_ref[...]).astype(BF16)
    y = x
    stage(0)
    for c in range(n_chunks):
        if c + 1 < n_chunks:
            stage(c + 1)
        act_s[:, c * cw:(c + 1) * cw] = (_silu(conv(c, 0)) * conv(c, 1)).astype(BF16)
        if (c + 1) % FFN_DOWN_CHUNKS == 0 or c + 1 == n_chunks:
            r0 = (c // FFN_DOWN_CHUNKS) * FFN_DOWN_CHUNKS * cw
            y = y + _dot(act_s[:, r0:(c + 1) * cw], wdn_ref[r0:(c + 1) * cw, :])
    if final_norm:
        y = _rms(y, nf_ref[...])
    y_ref[0] = y
    nb_ref[0] = carry_s[...]


def _ffn(x, n2, wup, wcv, bcv, wdn, buf, nf, shift, final_norm):
    nseq, t, d = x.shape
    tm = min(FFN_TILE, t)
    hist = 2 * shift
    base = -(-hist // SUBLANES) * SUBLANES
    kern = functools.partial(_ffn_kernel, shift=shift, base=base, cw=FFN_COLS, final_norm=final_norm)
    state = pl.BlockSpec((1, hist, buf.shape[2]), lambda i, j: (i, 0, 0))
    return pl.pallas_call(
        kern, grid=(nseq, t // tm),
        in_specs=[pl.BlockSpec((1, tm, d), lambda i, j: (i, j, 0)), _const_spec(n2.shape),
                  _const_spec(wup.shape), _const_spec(wcv.shape), _const_spec(bcv.shape),
                  _const_spec(wdn.shape), state, _const_spec(nf.shape)],
        out_specs=(pl.BlockSpec((1, tm, d), lambda i, j: (i, j, 0)), state),
        out_shape=(jax.ShapeDtypeStruct(x.shape, F32), jax.ShapeDtypeStruct(buf.shape, F32)),
        scratch_shapes=[pltpu.VMEM((hist, buf.shape[2]), F32),
                        pltpu.VMEM((FFN_EXT_SLOTS, base + tm, FFN_COLS), F32),
                        pltpu.VMEM((tm, d), BF16), pltpu.VMEM((tm, wdn.shape[0]), BF16)],
        compiler_params=_params("parallel", "arbitrary"), name="conv_ffn",
    )(x, n2, wup, wcv, bcv, wdn, buf, nf)


def kernel(x_prompt, x_sample, cache_k, cache_v, cache_logf, state_hgrn, state_conv, page_table,
           norm1, w_in, b_f, gmlp_norm, w_s, b_s, hgrn_norm, hgrn_lb, w_oa, w_ob, w_oc, w_o,
           norm2, w_up, w_conv, b_conv, w_down, norm_f):
    depth = w_in.shape[0]
    bp, tp, d = x_prompt.shape
    bs, ts, _ = x_sample.shape
    _, n_phys, page, nh_a, hd_a = cache_k.shape
    w_a = nh_a * hd_a
    n_grp, chunk_b, _ = w_s.shape[1:]
    w_b = gmlp_norm.shape[1]
    _, _, nh_c, k_c, v_c = state_hgrn.shape
    w_c = nh_c * v_c
    d_ff = w_down.shape[1]

    sm = jax.nn.softmax(hgrn_lb.astype(F32), axis=0)
    lower = jnp.cumsum(sm, axis=0) - sm[:1]

    cache_kt = jnp.transpose(cache_k, (0, 1, 3, 4, 2))
    cache_vt = jnp.transpose(cache_v, (0, 1, 3, 4, 2))
    cache_ft = jnp.swapaxes(cache_logf, 2, 3)

    xp = x_prompt.reshape(bp * tp, d)
    xs = x_sample.reshape(bs * ts, d)
    st_p, st_s = [], []
    kv_bufs = None
    for l in range(depth):
        o = 0
        wl = jnp.swapaxes(w_in[l], 0, 1)
        wa = wl[o:o + 3 * w_a]; o += 3 * w_a
        wf = wl[o:o + nh_a]; o += nh_a
        wb = wl[o:o + 2 * w_b]; o += 2 * w_b
        wc = wl[o:o + 4 * w_c]; o += 4 * w_c
        wg = wl[o:]
        qscale = jnp.concatenate([jnp.full((w_a, 1), hd_a ** -0.5 * LOG2E, F32), jnp.ones((2 * w_a, 1), F32)])
        wa = (wa * qscale).astype(BF16)
        wf_pad = jnp.pad(wf, ((0, LANES - nh_a), (0, 0))).astype(BF16)
        bf_row = jnp.pad(b_f[l], (0, LANES - nh_a)).reshape(1, LANES)
        inproj_w = (norm1[l].reshape(1, d), wa, wf_pad, wb.astype(BF16), wc.astype(BF16),
                    wg.astype(BF16), bf_row, b_f[l].reshape(nh_a, 1), gmlp_norm[l].reshape(1, w_b),
                    lower[l].reshape(1, w_c))

        tril_w = jnp.where(jnp.tril(jnp.ones((chunk_b, chunk_b), bool)), w_s[l], 0)
        woa, wob, woc, wo = (w_oa[l].astype(BF16), w_ob[l].astype(BF16), w_oc[l].astype(BF16),
                             w_o[l].astype(BF16))
        nw = hgrn_norm[l].reshape(1, w_c)

        n2 = norm2[l].reshape(1, d)
        wup = w_up[l].astype(BF16)
        wcv = w_conv[l]
        bcv = b_conv[l].reshape(1, 2 * d_ff)
        wdn = w_down[l].astype(BF16)
        nf = norm_f.reshape(1, d)
        last = l == depth - 1

        (q, k, kt_all, vt_all, lf, lft_all, ub, vb, qs, gl, kf, iv, gcs, gt) = _inproj(
            xp, *inproj_w, stacked=(l, depth, bp, kv_bufs))
        kv_bufs = (kt_all, vt_all, lft_all)
        ya = _fox_prompt(q.reshape(bp, tp, w_a), k.reshape(bp, tp, w_a), vt_all, l,
                         lf.reshape(bp, tp, LANES), nh_a)
        s0 = jnp.zeros((1, bp, nh_c, k_c, v_c), F32)
        yc, s_new = _hgrn(*(a.reshape(bp, tp, w_c) for a in (qs, gl, kf, iv, gcs)), nw, s0, 0)
        x1 = _merge(xp, ya.reshape(bp * tp, w_a), ub, vb, yc.reshape(bp * tp, w_c), gt,
                    tril_w.astype(BF16), b_s[l].T, woa, wob, woc, wo)
        buf0 = jnp.zeros((bp, 2, 2 * d_ff), F32)
        x2, conv_p = _ffn(x1.reshape(bp, tp, d), n2, wup, wcv, bcv, wdn, buf0, nf, 1, last)
        xp = x2.reshape(bp * tp, d)
        st_p.append((s_new, conv_p))

        (q, k, v, lf, ub, vb, qs, gl, kf, iv, gcs, gt) = _inproj(xs, *inproj_w)
        k4 = k.reshape(bs, ts, nh_a, hd_a)
        v4 = v.reshape(bs, ts, nh_a, hd_a)
        lf3 = lf[:, :nh_a].reshape(bs, ts, nh_a)
        ya = _fox_sample(q.astype(F32), k, v, jnp.swapaxes(lf3, 1, 2),
                         cache_kt, cache_vt, cache_ft, page_table, l)
        yc, s_new = _hgrn(*(a.reshape(bs, ts, w_c) for a in (qs, gl, kf, iv, gcs)), nw, state_hgrn, l)
        same_seq = np.kron(np.eye(bs, dtype=np.float32), np.ones((ts, ts), np.float32))
        rep = np.tile(np.eye(ts, dtype=np.float32), (bs, 1))
        gm = (jnp.einsum("rt,gts,cs->grc", rep, tril_w[:, :ts, :ts], rep) * same_seq).astype(BF16)
        gb = jnp.tile(b_s[l][:, :ts].T, (bs, 1))
        x1 = _merge(xs, ya, ub, vb, yc.reshape(bs * ts, w_c), gt, gm, gb, woa, wob, woc, wo)
        x1t = jnp.swapaxes(x1.reshape(bs, ts, d), 0, 1).reshape(1, ts * bs, d)
        buf_t = jnp.swapaxes(state_conv[l], 0, 1).reshape(1, 2 * bs, 2 * d_ff)
        x2t, nb = _ffn(x1t, n2, wup, wcv, bcv, wdn, buf_t, nf, bs, last)
        xs = jnp.swapaxes(x2t.reshape(ts, bs, d), 0, 1).reshape(bs * ts, d)
        conv_s = nb.reshape(2, bs, 2 * d_ff)
        st_s.append((k4, v4, lf3, vb.reshape(bs, ts, w_b), s_new, jnp.swapaxes(conv_s, 0, 1)))

    def stk(sts, i):
        return jnp.stack([st[i] for st in sts], axis=0)

    kt_all, vt_all, lft_all = kv_bufs
    k_prompt = jnp.transpose(kt_all.reshape(depth, bp, nh_a, hd_a, tp), (0, 1, 4, 2, 3))
    v_prompt = jnp.transpose(vt_all.reshape(depth, bp, nh_a, hd_a, tp), (0, 1, 4, 2, 3))
    return (xp.reshape(bp, tp, d), xs.reshape(bs, ts, d),
            k_prompt, v_prompt, jnp.swapaxes(lft_all, 2, 3), stk(st_p, 0), stk(st_p, 1),
            stk(st_s, 0), stk(st_s, 1), stk(st_s, 2), stk(st_s, 3), stk(st_s, 4), stk(st_s, 5))
```

```python
import functools

import numpy as np

import jax
import jax.numpy as jnp
from jax import lax
from jax.experimental import pallas as pl
from jax.experimental.pallas import tpu as pltpu

F32 = jnp.float32
BF16 = jnp.bfloat16
EPS = 1e-6
NEG_INF = float("-inf")
LOG2E = 1.4426950408889634

LANES = 128
SUBLANES = 8
VMEM_LIMIT_BYTES = 56 * 1024 * 1024

ROW_TILE = 256
MERGE_TILE = 512
FFN_TILE = 256
ATTN_TILE = 256
ATTN_HEADS_PER_PHASE = 8
PAGES_PER_STEP = 32
HGRN_SUB = 16
HGRN_TILE = 256
HGRN_GROUP = 8
FFN_COLS = 256
FFN_EXT_SLOTS = 4
FFN_DOWN_CHUNKS = 11
N_SPLIT = 3


def _dot(a, b):
    return jnp.dot(a, b, preferred_element_type=F32)


def _dot_nt(a, b):
    return lax.dot_general(a, b, (((1,), (1,)), ((), ())), preferred_element_type=F32)


def _dot_tn(a, b):
    return lax.dot_general(a, b, (((0,), (0,)), ((), ())), preferred_element_type=F32)


def _rms(x, g):
    return x * lax.rsqrt(jnp.mean(x * x, axis=-1, keepdims=True) + EPS) * g


def _sigmoid(x):
    return 1.0 / (1.0 + jnp.exp(-x))


def _silu(x):
    return x * _sigmoid(x)


def _gelu_tanh(x):
    c = 0.7978845608028654
    return x * (0.5 * (1.0 + jnp.tanh(c * (x + 0.044715 * (x * x * x)))))


def _log_sigmoid(z):
    return jnp.minimum(z, 0.0) - jnp.log1p(jnp.exp(-jnp.abs(z)))


def _split3(x):
    hi = x.astype(BF16)
    r = x - hi.astype(F32)
    mid = r.astype(BF16)
    lo = (r - mid.astype(F32)).astype(BF16)
    return hi, mid, lo


def _ones_tri(n, upper):
    r = lax.broadcasted_iota(jnp.int32, (n, n), 0)
    c = lax.broadcasted_iota(jnp.int32, (n, n), 1)
    keep = (r <= c) if upper else (r >= c)
    return jnp.where(keep, 1.0, 0.0).astype(BF16)


def _const_spec(shape):
    nd = len(shape)
    return pl.BlockSpec(shape, lambda *_: (0,) * nd, pipeline_mode=pl.Buffered(1))


def _layer_spec(arr, layer):
    nd = arr.ndim - 1
    return pl.BlockSpec((None,) + arr.shape[1:], lambda *_: (layer,) + (0,) * nd, pipeline_mode=pl.Buffered(1))


def _params(*sem):
    return pltpu.CompilerParams(dimension_semantics=sem, vmem_limit_bytes=VMEM_LIMIT_BYTES)


def _inproj_kernel(x_ref, n1_ref, wa_ref, wf_ref, wr_ref, bf_ref, bfc_ref, gn_ref, lb_ref,
                   *rest, n_alias, seq_major, qscale):
    outs = rest[n_alias:]
    if seq_major:
        q_ref, k_ref, kt_ref, vt_ref, lf_ref, lft_ref = outs[:6]
        ub_ref, vb_ref, qs_ref, gl_ref, kf_ref, iv_ref, gc_ref, gt_ref = outs[6:]
    else:
        q_ref, k_ref, v_ref, lf_ref = outs[:4]
        ub_ref, vb_ref, qs_ref, gl_ref, kf_ref, iv_ref, gc_ref, gt_ref = outs[4:]
    wa_w = k_ref.shape[-1]
    wb_w = ub_ref.shape[-1]
    wc_w = qs_ref.shape[-1]
    h = _rms(x_ref[...], n1_ref[...]).astype(BF16)

    if seq_major:
        pa = _dot_nt(h, wa_ref[:2 * wa_w, :])
        k_ref[...] = pa[:, wa_w:].astype(k_ref.dtype)
        kv_t = _dot_nt(wa_ref[wa_w:, :], h)
        kt_ref[...] = kv_t[:wa_w, :]
        vt_ref[...] = kv_t[wa_w:, :]
        lft = _log_sigmoid(_dot_nt(wf_ref[...], h)[:bfc_ref.shape[0], :] + bfc_ref[...])
        lft_ref[...] = lft
    else:
        pa = _dot_nt(h, wa_ref[...])
        k_ref[...] = pa[:, wa_w:2 * wa_w]
        v_ref[...] = pa[:, 2 * wa_w:]
    q_ref[...] = (pa[:, :wa_w] * qscale).astype(q_ref.dtype)

    lf_ref[...] = _log_sigmoid(_dot_nt(h, wf_ref[...]) + bf_ref[...])

    r_b, r_c = 2 * wb_w, 2 * wb_w + 4 * wc_w
    pb = _dot_nt(h, wr_ref[:r_b, :])
    ub_ref[...] = _gelu_tanh(pb[:, :wb_w])
    vb_ref[...] = _rms(_gelu_tanh(pb[:, wb_w:]), gn_ref[...])

    pc = _dot_nt(h, wr_ref[r_b:r_c, :])
    lb = lb_ref[...]
    qs_ref[...] = _silu(pc[:, :wc_w])
    f = lb + (1.0 - lb) * _sigmoid(pc[:, wc_w:2 * wc_w])
    gl_ref[...] = jnp.log(f)
    kf_ref[...] = 1.0 - f
    iv_ref[...] = pc[:, 2 * wc_w:3 * wc_w]
    gc_ref[...] = _silu(pc[:, 3 * wc_w:])

    gt_ref[...] = _sigmoid(_dot_nt(h, wr_ref[r_c:, :]))


def _inproj(x, layer, qscale, n1, wa, wf, wr, bf_row, bf_col, gn, lb, stacked=None):
    n, d = x.shape
    tm = min(ROW_TILE, n)
    wa_w, wb_w, wc_w = wa.shape[1] // 3, gn.shape[-1], lb.shape[-1]
    wg_w = wr.shape[1] - 2 * wb_w - 4 * wc_w
    nh = bf_col.shape[1]

    def rows(w):
        return pl.BlockSpec((tm, w), lambda i: (i, 0))

    q_shape = jax.ShapeDtypeStruct((n, wa_w), BF16)
    lf_shape = jax.ShapeDtypeStruct((n, LANES), F32)
    extra_in, extra_specs, aliases = [], [], {}
    if stacked is None:
        head_shape = (q_shape, jax.ShapeDtypeStruct((n, wa_w), F32), jax.ShapeDtypeStruct((n, wa_w), F32),
                      lf_shape)
        head_specs = (rows(wa_w), rows(wa_w), rows(wa_w), rows(LANES))
    else:
        depth, b, prev = stacked
        t = n // b
        tps = t // tm

        def feat_major(f):
            return (jax.ShapeDtypeStruct((depth, b, f, t), F32),
                    pl.BlockSpec((None, None, f, tm), lambda i: (layer, i // tps, 0, i % tps)))

        (kt_shape, kt_spec), (lft_shape, lft_spec) = feat_major(wa_w), feat_major(nh)
        head_shape = (q_shape, jax.ShapeDtypeStruct((n, wa_w), BF16), kt_shape, kt_shape, lf_shape, lft_shape)
        head_specs = (rows(wa_w), rows(wa_w), kt_spec, kt_spec, rows(LANES), lft_spec)
        if prev is not None:
            extra_in = list(prev)
            extra_specs = [pl.BlockSpec(memory_space=pl.ANY)] * len(prev)
            aliases = {9: 2, 10: 3, 11: 5}

    out_shape = head_shape + (
        jax.ShapeDtypeStruct((n, wb_w), F32),
        jax.ShapeDtypeStruct((n, wb_w), F32),
        jax.ShapeDtypeStruct((n, wc_w), F32),
        jax.ShapeDtypeStruct((n, wc_w), F32),
        jax.ShapeDtypeStruct((n, wc_w), F32),
        jax.ShapeDtypeStruct((n, wc_w), F32),
        jax.ShapeDtypeStruct((n, wc_w), F32),
        jax.ShapeDtypeStruct((n, wg_w), F32),
    )
    out_specs = head_specs + (rows(wb_w), rows(wb_w), rows(wc_w), rows(wc_w), rows(wc_w), rows(wc_w),
                              rows(wc_w), rows(wg_w))
    params = (n1, wa, wf, wr, bf_row, bf_col, gn, lb)
    in_specs = [rows(d)] + [_layer_spec(p, layer) for p in params] + extra_specs
    kern = functools.partial(_inproj_kernel, n_alias=len(extra_in), seq_major=stacked is not None,
                             qscale=qscale)
    return pl.pallas_call(
        kern, grid=(n // tm,), in_specs=in_specs, out_specs=out_specs, out_shape=out_shape,
        input_output_aliases=aliases, compiler_params=_params("parallel"), name="inproj",
    )(x, *params, *extra_in)


def _decay_selectors(nh, hd):
    sel_q = np.zeros((N_SPLIT * LANES, nh * hd), np.float32)
    sel_k = np.zeros((N_SPLIT * LANES, nh * hd), np.float32)
    one_q = np.zeros((1, nh * hd), np.float32)
    one_k = np.zeros((1, nh * hd), np.float32)
    for h in range(nh):
        for p in range(N_SPLIT):
            sel_k[p * LANES + h, h * hd + p] = -1.0
            one_k[0, h * hd + N_SPLIT + p] = 1.0
            one_q[0, h * hd + p] = 1.0
            sel_q[p * LANES + h, h * hd + N_SPLIT + p] = 1.0
    return (jnp.asarray(sel_q, BF16), jnp.asarray(sel_k, BF16), jnp.asarray(one_q), jnp.asarray(one_k))


def _fox_prep_kernel(q_ref, k_ref, vin_ref, lf_ref, selq_ref, selk_ref, oneq_ref, onek_ref,
                     qa_ref, ka_ref, vt_ref, carry_s):
    ti = pl.program_id(1)
    tm = q_ref.shape[1]
    w = q_ref.shape[2]

    @pl.when(ti == 0)
    def _():
        carry_s[...] = jnp.zeros_like(carry_s)

    lower = _ones_tri(tm, upper=False)
    hi, mid, lo = _split3(lf_ref[0])
    c = _dot(lower, hi) + _dot(lower, mid) + _dot(lower, lo) + carry_s[...]
    carry_s[...] = c[tm - 1:tm, :]
    c3 = jnp.concatenate(_split3(c * LOG2E), axis=1)
    eq = _dot(c3, selq_ref[...]) + oneq_ref[...]
    ek = _dot(c3, selk_ref[...]) + onek_ref[...]
    q = q_ref[0].astype(F32)
    k = k_ref[0].astype(F32)
    qa_parts, ka_parts = [], []
    for p in range(w // LANES):
        ls = slice(p * LANES, (p + 1) * LANES)
        qa_parts += [q[:, ls], eq[:, ls]]
        ka_parts += [k[:, ls], ek[:, ls]]
    qa_ref[0] = jnp.concatenate(qa_parts, axis=1).astype(BF16)
    ka_ref[0] = jnp.concatenate(ka_parts, axis=1).astype(BF16)
    vt_ref[0, 0] = vin_ref[...].astype(BF16)


def _fox_prompt_kernel(qa_ref, ka_ref, vt_ref, o_ref, qh_s, m_s, l_s, acc_s, *, hd, tq):
    qi = pl.program_id(1)
    heads_per_slab = LANES // hd
    n_slab = vt_ref.shape[2] // LANES
    heads = [(sl, hh) for sl in range(n_slab) for hh in range(heads_per_slab)]
    lane2 = lax.broadcasted_iota(jnp.int32, (tq, 2 * LANES), 1) % LANES
    key = lax.broadcasted_iota(jnp.int32, (tq, tq), 0)
    qry = lax.broadcasted_iota(jnp.int32, (tq, tq), 1)

    for n, (sl, hh) in enumerate(heads):
        qa = qa_ref[0, :, sl * 2 * LANES:(sl + 1) * 2 * LANES].astype(F32)
        qh_s[n] = jnp.where(lane2 // hd == hh, qa, 0.0).astype(BF16)
        m_s[n] = jnp.full((1, tq), NEG_INF, F32)
        l_s[n] = jnp.zeros((1, tq), F32)
        acc_s[n] = jnp.zeros((LANES, tq), F32)

    def tile(j, masked):
        k0 = pl.multiple_of(j * tq, tq)
        for g0 in range(0, len(heads), ATTN_HEADS_PER_PHASE):
            grp = list(enumerate(heads))[g0:g0 + ATTN_HEADS_PER_PHASE]
            s_all = []
            for n, (sl, _) in grp:
                s = _dot_nt(ka_ref[0, pl.ds(k0, tq), sl * 2 * LANES:(sl + 1) * 2 * LANES], qh_s[n])
                s_all.append(jnp.where(key <= qry, s, NEG_INF) if masked else s)
            m_old = [m_s[n] for n, _ in grp]
            m_new = [jnp.maximum(m, jnp.max(s, axis=0, keepdims=True)) for m, s in zip(m_old, s_all)]
            p_all = [jnp.exp2(s - m) for s, m in zip(s_all, m_new)]
            for i, (n, (sl, _)) in enumerate(grp):
                alpha = jnp.exp2(m_old[i] - m_new[i])
                l_s[n] = alpha * l_s[n] + jnp.sum(p_all[i], axis=0, keepdims=True)
                pv = _dot(vt_ref[0, j, sl * LANES:(sl + 1) * LANES, :], p_all[i].astype(BF16))
                acc_s[n] = alpha * acc_s[n] + pv
                m_s[n] = m_new[i]

    def body(j, carry):
        tile(j, False)
        return carry

    lax.fori_loop(0, qi, body, 0)
    tile(qi, True)
    for sl in range(n_slab):
        parts = [(acc_s[n] / l_s[n])[hh * hd:(hh + 1) * hd, :] for n, (s2, hh) in enumerate(heads) if s2 == sl]
        o_ref[0, :, sl * LANES:(sl + 1) * LANES] = jnp.concatenate(parts, axis=0).T.astype(o_ref.dtype)


def _fox_prompt(q, k, vt_all, layer, lf, nh):
    b, t, w = k.shape
    hd = w // nh
    tq = min(ATTN_TILE, t)
    nkv = t // tq
    sel_q, sel_k, one_q, one_k = _decay_selectors(nh, hd)
    blk = lambda width: pl.BlockSpec((1, tq, width), lambda i, j: (i, j, 0))
    qa, ka, vt = pl.pallas_call(
        _fox_prep_kernel, grid=(b, t // tq),
        in_specs=[blk(w), blk(w), pl.BlockSpec((None, None, w, tq), lambda i, j: (layer, i, 0, j)),
                  blk(LANES), _const_spec(sel_q.shape), _const_spec(sel_k.shape),
                  _const_spec(one_q.shape), _const_spec(one_k.shape)],
        out_specs=(blk(2 * w), blk(2 * w), pl.BlockSpec((1, 1, w, tq), lambda i, j: (i, j, 0, 0))),
        out_shape=(jax.ShapeDtypeStruct((b, t, 2 * w), BF16), jax.ShapeDtypeStruct((b, t, 2 * w), BF16),
                   jax.ShapeDtypeStruct((b, nkv, w, tq), BF16)),
        scratch_shapes=[pltpu.VMEM((1, LANES), F32)],
        compiler_params=_params("parallel", "arbitrary"), name="fox_prep",
    )(q, k, vt_all, lf, sel_q, sel_k, one_q, one_k)
    kern = functools.partial(_fox_prompt_kernel, hd=hd, tq=tq)
    return pl.pallas_call(
        kern, grid=(b, nkv),
        in_specs=[pl.BlockSpec((1, tq, 2 * w), lambda i, j: (i, j, 0)),
                  pl.BlockSpec((1, t, 2 * w), lambda i, j: (i, 0, 0)),
                  pl.BlockSpec((1, nkv, w, tq), lambda i, j: (i, 0, 0, 0))],
        out_specs=pl.BlockSpec((1, tq, w), lambda i, j: (i, j, 0)),
        out_shape=jax.ShapeDtypeStruct((b, t, w), BF16),
        scratch_shapes=[pltpu.VMEM((nh, tq, 2 * LANES), BF16), pltpu.VMEM((nh, 1, tq), F32),
                        pltpu.VMEM((nh, 1, tq), F32), pltpu.VMEM((nh, LANES, tq), F32)],
        compiler_params=_params("parallel", "arbitrary"), name="fox_prompt",
    )(qa, ka, vt)


def _fox_sample_kernel(pt_ref, q_ref, kn_ref, vn_ref, lnt_ref, *rest, nh, hd, g_pages):
    k_pages = rest[:g_pages]
    v_pages = rest[g_pages:2 * g_pages]
    f_pages = rest[2 * g_pages:3 * g_pages]
    o_ref, qbd_s, m_s, l_s, acc_s, carry_s, cn_s = rest[3 * g_pages:]
    del pt_ref
    step = pl.program_id(1)
    nt = q_ref.shape[0]
    w = q_ref.shape[1]
    page = k_pages[0].shape[-1]
    rows = nt * nh

    def update(s3, vs):
        m_old = m_s[...]
        m_new = jnp.maximum(m_old, jnp.max(s3, axis=-1, keepdims=True))
        alpha = jnp.exp2(m_old - m_new)
        p3 = jnp.exp2(s3 - m_new)
        l_s[...] = alpha * l_s[...] + jnp.sum(p3, axis=-1, keepdims=True)
        p = p3.reshape(rows, p3.shape[-1]).astype(BF16)
        pv = None
        off = 0
        for vt in vs:
            part = _dot_nt(p[:, off:off + vt.shape[1]], vt)
            pv = part if pv is None else pv + part
            off += vt.shape[1]
        acc_s[...] = alpha * acc_s[...] + pv.reshape(nt, nh, w)
        m_s[...] = m_new

    @pl.when(step == 0)
    def _():
        rep_r = lax.broadcasted_iota(jnp.int32, (rows, nt), 0) // nh
        rep_c = lax.broadcasted_iota(jnp.int32, (rows, nt), 1)
        q_rep = _dot(jnp.where(rep_r == rep_c, 1.0, 0.0), q_ref[...])
        head_of_lane = lax.broadcasted_iota(jnp.int32, (rows, w), 1) // hd
        head_of_row = lax.broadcasted_iota(jnp.int32, (rows, w), 0) % nh
        qbd = jnp.where(head_of_lane == head_of_row, q_rep, 0.0).astype(BF16)
        qbd_s[...] = qbd

        lane = lax.broadcasted_iota(jnp.int32, (nh, page), 1)
        cn = jnp.concatenate([lnt_ref[...] * LOG2E, jnp.zeros((nh, page - nt), F32)], axis=1)
        sh = 1
        while sh < nt:
            cn = cn + jnp.where(lane >= sh, pltpu.roll(cn, sh, 1), 0.0)
            sh *= 2
        cn_s[...] = cn
        carry_s[...] = jnp.zeros_like(carry_s)
        m_s[...] = jnp.full(m_s.shape, NEG_INF, F32)
        l_s[...] = jnp.zeros_like(l_s)
        acc_s[...] = jnp.zeros_like(acc_s)

        kn = jnp.concatenate([kn_ref[...], jnp.zeros((page - nt, w), F32)], axis=0).astype(BF16)
        vn = jnp.concatenate([vn_ref[...], jnp.zeros((page - nt, w), F32)], axis=0).T.astype(BF16)
        s3 = _dot_nt(qbd, kn).reshape(nt, nh, page)
        cq = jnp.stack([cn[:, t:t + 1] for t in range(nt)], axis=0)
        s3 = s3 + (cq - cn[None, :, :])
        tk = lax.broadcasted_iota(jnp.int32, (nt, nh, page), 2)
        tq = lax.broadcasted_iota(jnp.int32, (nt, nh, page), 0)
        update(jnp.where(tk <= tq, s3, NEG_INF), [vn])

    lane = lax.broadcasted_iota(jnp.int32, (nh, page), 1)
    cn = cn_s[...]
    cq = jnp.stack([cn[:, t:t + 1] for t in range(nt)], axis=0)
    qbd = qbd_s[...]
    carry = carry_s[...]
    s_tiles, v_tiles = [], []
    for g in range(g_pages):
        x = f_pages[g][...] * LOG2E
        incl = x
        sh = 1
        while sh < page:
            incl = incl + jnp.where(lane + sh < page, pltpu.roll(incl, page - sh, 1), 0.0)
            sh *= 2
        r = carry + (incl - x)
        carry = carry + incl[:, 0:1]
        s3 = _dot(qbd, k_pages[g][...].reshape(w, page).astype(BF16)).reshape(nt, nh, page)
        s_tiles.append(s3 + (cq + r[None, :, :]))
        v_tiles.append(v_pages[g][...].reshape(w, page).astype(BF16))
    carry_s[...] = carry
    update(jnp.concatenate(s_tiles, axis=-1), v_tiles)

    @pl.when(step == pl.num_programs(1) - 1)
    def _():
        head_of_lane = lax.broadcasted_iota(jnp.int32, (nt, nh, w), 2) // hd
        head_of_row = lax.broadcasted_iota(jnp.int32, (nt, nh, w), 1)
        o3 = jnp.where(head_of_lane == head_of_row, acc_s[...] / l_s[...], 0.0)
        o2 = o3.reshape(rows, w).astype(BF16)
        sel_r = lax.broadcasted_iota(jnp.int32, (nt, rows), 0)
        sel_c = lax.broadcasted_iota(jnp.int32, (nt, rows), 1) // nh
        sel = jnp.where(sel_r == sel_c, 1.0, 0.0).astype(BF16)
        o_ref[...] = _dot(sel, o2).astype(o_ref.dtype)


def _fox_sample(q, kn, vn, lnt, cache_kt, cache_vt, cache_ft, page_table, layer):
    nseq, n_pages = page_table.shape
    n, w = q.shape
    nt = n // nseq
    _, _, nh, hd, page = cache_kt.shape
    g_pages = min(PAGES_PER_STEP, n_pages)
    n_steps = n_pages // g_pages

    def page_spec(shape, g):
        nd = len(shape)

        def idx(b, s, pt):
            return (layer, pt[b * n_pages + (n_pages - 1 - (s * g_pages + g))]) + (0,) * nd
        return pl.BlockSpec((None, None) + shape, idx)

    seq_rows = pl.BlockSpec((nt, w), lambda b, s, pt: (b, 0))
    in_specs = [seq_rows, seq_rows, seq_rows, pl.BlockSpec((None, nh, nt), lambda b, s, pt: (b, 0, 0))]
    in_specs += [page_spec((nh, hd, page), g) for g in range(g_pages)]
    in_specs += [page_spec((nh, hd, page), g) for g in range(g_pages)]
    in_specs += [page_spec((nh, page), g) for g in range(g_pages)]
    kern = functools.partial(_fox_sample_kernel, nh=nh, hd=hd, g_pages=g_pages)
    grid_spec = pltpu.PrefetchScalarGridSpec(
        num_scalar_prefetch=1, grid=(nseq, n_steps), in_specs=in_specs,
        out_specs=pl.BlockSpec((nt, w), lambda b, s, pt: (b, 0)),
        scratch_shapes=[pltpu.VMEM((nt * nh, w), BF16),
                        pltpu.VMEM((nt, nh, 1), F32), pltpu.VMEM((nt, nh, 1), F32),
                        pltpu.VMEM((nt, nh, w), F32),
                        pltpu.VMEM((nh, page), F32), pltpu.VMEM((nh, page), F32)])
    return pl.pallas_call(
        kern, grid_spec=grid_spec, out_shape=jax.ShapeDtypeStruct((n, w), F32),
        compiler_params=_params("parallel", "arbitrary"), name="fox_sample",
    )(page_table.reshape(-1), q, kn, vn, lnt,
      *([cache_kt] * g_pages), *([cache_vt] * g_pages), *([cache_ft] * g_pages))


def _hgrn_kernel(q_ref, g_ref, kf_ref, v_ref, gc_ref, nw_ref, s0_ref, y_ref, sn_ref, st_s, *, c, nsub, kd):
    ti = pl.program_id(1)
    nh = s0_ref.shape[1]

    @pl.when(ti == 0)
    def _():
        for h in range(nh):
            st_s[h] = s0_ref[0, h].T

    blk = SUBLANES
    group = HGRN_GROUP if nsub % HGRN_GROUP == 0 else 1
    rowi = lax.broadcasted_iota(jnp.int32, (blk, 1), 0)
    subi = lax.broadcasted_iota(jnp.int32, (blk, kd), 0)

    def cumsum_blocks(g):
        b_blocks = []
        off = None
        for i in range(c // blk):
            x = g[i * blk:(i + 1) * blk, :]
            sh = 1
            while sh < blk:
                x = x + jnp.where(subi >= sh, pltpu.roll(x, sh, 0), 0.0)
                sh *= 2
            x = x if off is None else x + off
            off = x[blk - 1:blk, :]
            b_blocks.append(x)
        return b_blocks

    def cat(blocks):
        return jnp.concatenate(blocks, axis=0) if len(blocks) > 1 else blocks[0]

    def body(i, carry):
        streams = []
        for u in range(group):
            rs = pl.ds(0 if nsub == 1 else pl.multiple_of((i * group + u) * c, c), c)
            for h in range(nh):
                hs = slice(h * kd, (h + 1) * kd)
                q, kf, v = q_ref[0, rs, hs], kf_ref[0, rs, hs], v_ref[0, rs, hs]
                b_blocks = cumsum_blocks(g_ref[0, rs, hs])
                b = cat(b_blocks)
                a_prev = [None]
                for n in range(1, len(b_blocks)):
                    e = b_blocks[n - 1][blk - 1:blk, :]
                    a_prev.append(_dot_nt((q[n * blk:(n + 1) * blk, :] * jnp.exp(b_blocks[n] - e)).astype(BF16),
                                          (kf[:n * blk, :] * jnp.exp(e - b[:n * blk, :])).astype(BF16)))
                streams.append((rs, h, hs, q, kf, v, b_blocks, b, a_prev))

        diag = []
        for rs, h, hs, q, kf, v, b_blocks, b, a_prev in streams:
            o_blocks = []
            for n, bn in enumerate(b_blocks):
                r0 = n * blk
                qn = q[r0:r0 + blk, :]
                on = jnp.zeros((blk, kd), F32)
                for s in range(blk):
                    xs = qn * jnp.exp(jnp.minimum(bn - bn[s:s + 1, :], 0.0)) * kf[r0 + s:r0 + s + 1, :]
                    a = jnp.sum(xs, axis=-1, keepdims=True)
                    on = on + jnp.where(rowi >= s, a, 0.0) * v[r0 + s:r0 + s + 1, :]
                o_blocks.append(on)
            diag.append(o_blocks)

        for (rs, h, hs, q, kf, v, b_blocks, b, a_prev), o_blocks in zip(streams, diag):
            st = st_s[h]
            for n in range(1, len(b_blocks)):
                o_blocks[n] = o_blocks[n] + _dot(a_prev[n].astype(BF16), v[:n * blk, :].astype(BF16))
            o = cat(o_blocks) + _dot_nt((q * jnp.exp(b)).astype(BF16), st.astype(BF16))
            bl = b_blocks[-1][blk - 1:blk, :]
            kd_ = kf * jnp.exp(bl - b)
            st_s[h] = st * jnp.exp(bl) + _dot_tn(v.astype(BF16), kd_.astype(BF16))
            y = _rms(o, nw_ref[:, hs]) * gc_ref[0, rs, hs]
            y_ref[0, rs, hs] = y.astype(y_ref.dtype)
        return carry

    if nsub == 1:
        body(0, 0)
    else:
        lax.fori_loop(0, nsub // group, body, 0)

    @pl.when(ti == pl.num_programs(1) - 1)
    def _():
        for h in range(nh):
            sn_ref[0, h] = st_s[h].T


def _hgrn(qs, gl, kf, iv, gcs, nw, nw_layer, s0_all, layer):
    b, t, w = qs.shape
    _, _, nh, kd, vd = s0_all.shape
    c = HGRN_SUB if t % HGRN_SUB == 0 else t
    tt = HGRN_TILE if t % HGRN_TILE == 0 else t
    kern = functools.partial(_hgrn_kernel, c=c, nsub=tt // c, kd=kd)
    seq = pl.BlockSpec((1, tt, w), lambda i, j: (i, j, 0))
    state = pl.BlockSpec((1, nh, kd, vd), lambda i, j: (i, 0, 0, 0))
    state_in = pl.BlockSpec((None, 1, nh, kd, vd), lambda i, j: (layer, i, 0, 0, 0))
    return pl.pallas_call(
        kern, grid=(b, t // tt),
        in_specs=[seq, seq, seq, seq, seq, _layer_spec(nw, nw_layer), state_in],
        out_specs=(seq, state),
        out_shape=(jax.ShapeDtypeStruct((b, t, nh * vd), BF16), jax.ShapeDtypeStruct((b, nh, kd, vd), F32)),
        scratch_shapes=[pltpu.VMEM((nh, vd, kd), F32)],
        compiler_params=_params("parallel", "arbitrary"), name="hgrn",
    )(qs, gl, kf, iv, gcs, nw, s0_all)


def _merge_kernel(x_ref, ya_ref, ub_ref, vb_ref, yc_ref, gt_ref, gm_ref, gb_ref,
                  woa_ref, wob_ref, woc_ref, wo_ref, o_ref, *, n_groups):
    tm = x_ref.shape[0]
    d = x_ref.shape[1]
    cn = gm_ref.shape[1]
    gw = ub_ref.shape[1] // n_groups
    yb_rows = []
    for ci in range(tm // cn):
        rs = slice(ci * cn, (ci + 1) * cn)
        vb = vb_ref[rs, :].astype(BF16)
        gb = gb_ref[...]
        parts = []
        for g in range(n_groups):
            parts.append(_dot(gm_ref[g], vb[:, g * gw:(g + 1) * gw]) + gb[:, g:g + 1])
        yb_rows.append(ub_ref[rs, :] * jnp.concatenate(parts, axis=1))
    yb = jnp.concatenate(yb_rows, axis=0) if len(yb_rows) > 1 else yb_rows[0]
    gt = gt_ref[...]
    merged = (gt[:, :d] * _dot(ya_ref[...].astype(BF16), woa_ref[...])
              + gt[:, d:2 * d] * _dot(yb.astype(BF16), wob_ref[...])
              + gt[:, 2 * d:] * _dot(yc_ref[...].astype(BF16), woc_ref[...]))
    o_ref[...] = x_ref[...] + _dot(merged.astype(BF16), wo_ref[...])


def _merge(x, ya, ub, vb, yc, gt, layer, gm, gb, woa, wob, woc, wo):
    n, d = x.shape
    tm = min(MERGE_TILE, n)

    def rows(w):
        return pl.BlockSpec((tm, w), lambda i: (i, 0))

    kern = functools.partial(_merge_kernel, n_groups=gm.shape[1])
    return pl.pallas_call(
        kern, grid=(n // tm,),
        in_specs=[rows(d), rows(ya.shape[1]), rows(ub.shape[1]), rows(vb.shape[1]), rows(yc.shape[1]),
                  rows(gt.shape[1])] + [_layer_spec(p, layer) for p in (gm, gb, woa, wob, woc, wo)],
        out_specs=rows(d), out_shape=jax.ShapeDtypeStruct((n, d), F32),
        compiler_params=_params("parallel"), name="merge",
    )(x, ya, ub, vb, yc, gt, gm, gb, woa, wob, woc, wo)


def _ffn_kernel(x_ref, n2_ref, wup_ref, wcv_ref, bcv_ref, wdn_ref, buf_ref, nf_ref,
                y_ref, nb_ref, carry_s, ext_s, h_s, act_s, *, shift, base, cw, final_norm):
    ti = pl.program_id(1)
    tm = x_ref.shape[1]
    d_ff = wdn_ref.shape[0]
    hist = 2 * shift

    @pl.when(ti == 0)
    def _():
        carry_s[...] = buf_ref[0]

    n_chunks = d_ff // cw
    n_slots = ext_s.shape[0]

    def col0(c, half):
        return half * d_ff + c * cw

    def stage(c):
        for half in range(2):
            cols = slice(col0(c, half), col0(c, half) + cw)
            ext = ext_s.at[(2 * c + half) % n_slots]
            ext[base - hist:base, :] = carry_s[:, cols]
            ext[base:base + tm, :] = _dot(h_s[...], wup_ref[:, cols])

    def conv(c, half):
        cols = slice(col0(c, half), col0(c, half) + cw)
        ext = ext_s.at[(2 * c + half) % n_slots]
        wcv = wcv_ref[:, cols]
        out = bcv_ref[:, cols] + (wcv[0:1, :] * ext[base - hist:base - hist + tm, :]
                                  + wcv[1:2, :] * ext[base - shift:base - shift + tm, :]
                                  + wcv[2:3, :] * ext[base:base + tm, :])
        carry_s[:, cols] = ext[base + tm - hist:base + tm, :]
        return out

    x = x_ref[0]
    h_s[...] = _rms(x, n2_ref[...]).astype(BF16)
    y = x
    stage(0)
    for c in range(n_chunks):
        if c + 1 < n_chunks:
            stage(c + 1)
        act_s[:, c * cw:(c + 1) * cw] = (_silu(conv(c, 0)) * conv(c, 1)).astype(BF16)
        if (c + 1) % FFN_DOWN_CHUNKS == 0 or c + 1 == n_chunks:
            r0 = (c // FFN_DOWN_CHUNKS) * FFN_DOWN_CHUNKS * cw
            y = y + _dot(act_s[:, r0:(c + 1) * cw], wdn_ref[r0:(c + 1) * cw, :])
    if final_norm:
        y = _rms(y, nf_ref[...])
    y_ref[0] = y
    nb_ref[0] = carry_s[...]


def _ffn(x, layer, n2, wup, wcv, bcv, wdn, buf, nf, shift, final_norm):
    nseq, t, d = x.shape
    tm = min(FFN_TILE, t)
    hist = 2 * shift
    base = -(-hist // SUBLANES) * SUBLANES
    kern = functools.partial(_ffn_kernel, shift=shift, base=base, cw=FFN_COLS, final_norm=final_norm)
    state = pl.BlockSpec((1, hist, buf.shape[2]), lambda i, j: (i, 0, 0))
    return pl.pallas_call(
        kern, grid=(nseq, t // tm),
        in_specs=[pl.BlockSpec((1, tm, d), lambda i, j: (i, j, 0)), _layer_spec(n2, layer),
                  _layer_spec(wup, layer), _layer_spec(wcv, layer), _layer_spec(bcv, layer),
                  _layer_spec(wdn, layer), state, _const_spec(nf.shape)],
        out_specs=(pl.BlockSpec((1, tm, d), lambda i, j: (i, j, 0)), state),
        out_shape=(jax.ShapeDtypeStruct(x.shape, F32), jax.ShapeDtypeStruct(buf.shape, F32)),
        scratch_shapes=[pltpu.VMEM((hist, buf.shape[2]), F32),
                        pltpu.VMEM((FFN_EXT_SLOTS, base + tm, FFN_COLS), F32),
                        pltpu.VMEM((tm, d), BF16), pltpu.VMEM((tm, wdn.shape[1]), BF16)],
        compiler_params=_params("parallel", "arbitrary"), name="conv_ffn",
    )(x, n2, wup, wcv, bcv, wdn, buf, nf)


def kernel(x_prompt, x_sample, cache_k, cache_v, cache_logf, state_hgrn, state_conv, page_table,
           norm1, w_in, b_f, gmlp_norm, w_s, b_s, hgrn_norm, hgrn_lb, w_oa, w_ob, w_oc, w_o,
           norm2, w_up, w_conv, b_conv, w_down, norm_f):
    depth = w_in.shape[0]
    bp, tp, d = x_prompt.shape
    bs, ts, _ = x_sample.shape
    _, n_phys, page, nh_a, hd_a = cache_k.shape
    w_a = nh_a * hd_a
    n_grp, chunk_b, _ = w_s.shape[1:]
    w_b = gmlp_norm.shape[1]
    _, _, nh_c, k_c, v_c = state_hgrn.shape
    w_c = nh_c * v_c
    d_ff = w_down.shape[1]

    sm = jax.nn.softmax(hgrn_lb.astype(F32), axis=0)
    lower = jnp.cumsum(sm, axis=0) - sm[:1]

    cache_kt = jnp.transpose(cache_k, (0, 1, 3, 4, 2))
    cache_vt = jnp.transpose(cache_v, (0, 1, 3, 4, 2))
    cache_ft = jnp.swapaxes(cache_logf, 2, 3)

    w_t = jnp.swapaxes(w_in, 1, 2).astype(BF16)
    o_f = 3 * w_a
    inproj_w = (norm1.reshape(depth, 1, d), w_t[:, :o_f],
                jnp.pad(w_t[:, o_f:o_f + nh_a], ((0, 0), (0, LANES - nh_a), (0, 0))), w_t[:, o_f + nh_a:],
                jnp.pad(b_f, ((0, 0), (0, LANES - nh_a))).reshape(depth, 1, LANES), b_f.reshape(depth, nh_a, 1),
                gmlp_norm.reshape(depth, 1, w_b), lower.reshape(depth, 1, w_c))
    qscale = hd_a ** -0.5 * LOG2E
    nw = hgrn_norm.reshape(depth, 1, w_c)
    tril_w = jnp.where(jnp.tril(jnp.ones((chunk_b, chunk_b), bool)), w_s, 0)
    merge_w = (w_oa.astype(BF16), w_ob.astype(BF16), w_oc.astype(BF16), w_o.astype(BF16))
    gate_p = (tril_w.astype(BF16), jnp.swapaxes(b_s, 1, 2))
    same_seq = np.kron(np.eye(bs, dtype=np.float32), np.ones((ts, ts), np.float32))
    rep = np.tile(np.eye(ts, dtype=np.float32), (bs, 1))
    gate_s = ((jnp.einsum("rt,lgts,cs->lgrc", rep, tril_w[:, :, :ts, :ts], rep) * same_seq).astype(BF16),
              jnp.tile(jnp.swapaxes(b_s[:, :, :ts], 1, 2), (1, bs, 1)))
    ffn_w = (norm2.reshape(depth, 1, d), w_up.astype(BF16), w_conv, b_conv.reshape(depth, 1, 2 * d_ff),
             w_down.astype(BF16))
    nf = norm_f.reshape(1, d)
    s0 = jnp.zeros((1, bp, nh_c, k_c, v_c), F32)
    buf0 = jnp.zeros((bp, 2, 2 * d_ff), F32)

    xp = x_prompt.reshape(bp * tp, d)
    xs = x_sample.reshape(bs * ts, d)
    st_p, st_s = [], []
    kv_bufs = None
    for l in range(depth):
        last = l == depth - 1

        (q, k, kt_all, vt_all, lf, lft_all, ub, vb, qs, gl, kf, iv, gcs, gt) = _inproj(
            xp, l, qscale, *inproj_w, stacked=(depth, bp, kv_bufs))
        kv_bufs = (kt_all, vt_all, lft_all)
        ya = _fox_prompt(q.reshape(bp, tp, w_a), k.reshape(bp, tp, w_a), vt_all, l,
                         lf.reshape(bp, tp, LANES), nh_a)
        yc, s_new = _hgrn(*(a.reshape(bp, tp, w_c) for a in (qs, gl, kf, iv, gcs)), nw, l, s0, 0)
        x1 = _merge(xp, ya.reshape(bp * tp, w_a), ub, vb, yc.reshape(bp * tp, w_c), gt, l, *gate_p, *merge_w)
        x2, conv_p = _ffn(x1.reshape(bp, tp, d), l, *ffn_w, buf0, nf, 1, last)
        xp = x2.reshape(bp * tp, d)
        st_p.append((s_new, conv_p))

        (q, k, v, lf, ub, vb, qs, gl, kf, iv, gcs, gt) = _inproj(xs, l, qscale, *inproj_w)
        k4 = k.reshape(bs, ts, nh_a, hd_a)
        v4 = v.reshape(bs, ts, nh_a, hd_a)
        lf3 = lf[:, :nh_a].reshape(bs, ts, nh_a)
        ya = _fox_sample(q.astype(F32), k, v, jnp.swapaxes(lf3, 1, 2),
                         cache_kt, cache_vt, cache_ft, page_table, l)
        yc, s_new = _hgrn(*(a.reshape(bs, ts, w_c) for a in (qs, gl, kf, iv, gcs)), nw, l, state_hgrn, l)
        x1 = _merge(xs, ya, ub, vb, yc.reshape(bs * ts, w_c), gt, l, *gate_s, *merge_w)
        x1t = jnp.swapaxes(x1.reshape(bs, ts, d), 0, 1).reshape(1, ts * bs, d)
        buf_t = jnp.swapaxes(state_conv[l], 0, 1).reshape(1, 2 * bs, 2 * d_ff)
        x2t, nb = _ffn(x1t, l, *ffn_w, buf_t, nf, bs, last)
        xs = jnp.swapaxes(x2t.reshape(ts, bs, d), 0, 1).reshape(bs * ts, d)
        conv_s = nb.reshape(2, bs, 2 * d_ff)
        st_s.append((k4, v4, lf3, vb.reshape(bs, ts, w_b), s_new, jnp.swapaxes(conv_s, 0, 1)))

    def stk(sts, i):
        return jnp.stack([st[i] for st in sts], axis=0)

    kt_all, vt_all, lft_all = kv_bufs
    k_prompt = jnp.transpose(kt_all.reshape(depth, bp, nh_a, hd_a, tp), (0, 1, 4, 2, 3))
    v_prompt = jnp.transpose(vt_all.reshape(depth, bp, nh_a, hd_a, tp), (0, 1, 4, 2, 3))
    return (xp.reshape(bp, tp, d), xs.reshape(bs, ts, d),
            k_prompt, v_prompt, jnp.swapaxes(lft_all, 2, 3), stk(st_p, 0), stk(st_p, 1),
            stk(st_s, 0), stk(st_s, 1), stk(st_s, 2), stk(st_s, 3), stk(st_s, 4), stk(st_s, 5))
```

```python
import functools

import numpy as np

import jax
import jax.numpy as jnp
from jax import lax
from jax.experimental import pallas as pl
from jax.experimental.pallas import tpu as pltpu

F32 = jnp.float32
BF16 = jnp.bfloat16
EPS = 1e-6
NEG_INF = float("-inf")
LOG2E = 1.4426950408889634

LANES = 128
SUBLANES = 8
VMEM_LIMIT_BYTES = 56 * 1024 * 1024

ROW_TILE = 256
MERGE_TILE = 512
FFN_TILE = 256
PAGES_PER_STEP = 32
HGRN_SUB = 16
HGRN_TILE = 256
HGRN_GROUP = 8
FFN_COLS = 256
FFN_EXT_SLOTS = 4
FFN_DOWN_CHUNKS = 11
N_SPLIT = 3


def _dot(a, b):
    return jnp.dot(a, b, preferred_element_type=F32)


def _dot_nt(a, b):
    return lax.dot_general(a, b, (((1,), (1,)), ((), ())), preferred_element_type=F32)


def _dot_tn(a, b):
    return lax.dot_general(a, b, (((0,), (0,)), ((), ())), preferred_element_type=F32)


def _rms(x, g):
    return x * lax.rsqrt(jnp.mean(x * x, axis=-1, keepdims=True) + EPS) * g


def _sigmoid(x):
    return 1.0 / (1.0 + jnp.exp(-x))


def _silu(x):
    return x * _sigmoid(x)


def _gelu_tanh(x):
    c = 0.7978845608028654
    return x * (0.5 * (1.0 + jnp.tanh(c * (x + 0.044715 * (x * x * x)))))


def _log_sigmoid(z):
    return jnp.minimum(z, 0.0) - jnp.log1p(jnp.exp(-jnp.abs(z)))


def _split3(x):
    hi = x.astype(BF16)
    r = x - hi.astype(F32)
    mid = r.astype(BF16)
    lo = (r - mid.astype(F32)).astype(BF16)
    return hi, mid, lo


def _ones_tri(n, upper):
    r = lax.broadcasted_iota(jnp.int32, (n, n), 0)
    c = lax.broadcasted_iota(jnp.int32, (n, n), 1)
    keep = (r <= c) if upper else (r >= c)
    return jnp.where(keep, 1.0, 0.0).astype(BF16)


def _const_spec(shape):
    nd = len(shape)
    return pl.BlockSpec(shape, lambda *_: (0,) * nd, pipeline_mode=pl.Buffered(1))


def _layer_spec(arr, layer):
    nd = arr.ndim - 1
    return pl.BlockSpec((None,) + arr.shape[1:], lambda *_: (layer,) + (0,) * nd, pipeline_mode=pl.Buffered(1))


def _params(*sem):
    return pltpu.CompilerParams(dimension_semantics=sem, vmem_limit_bytes=VMEM_LIMIT_BYTES)


def _inproj_kernel(x_ref, n1_ref, wa_ref, wf_ref, wr_ref, bf_ref, bfc_ref, gn_ref, lb_ref,
                   *rest, n_alias, seq_major, qscale, tps):
    if seq_major:
        selq_ref, selk_ref, oneq_ref, onek_ref = rest[:4]
        outs, carry_s = rest[4 + n_alias:-1], rest[-1]
        qa_ref, ka_ref, kt_ref, vt_ref, vtb_ref, lft_ref = outs[:6]
        ub_ref, vb_ref, qs_ref, gl_ref, kf_ref, iv_ref, gc_ref, gt_ref = outs[6:]
        wa_w = kt_ref.shape[0]
    else:
        q_ref, k_ref, v_ref, lf_ref = rest[:4]
        ub_ref, vb_ref, qs_ref, gl_ref, kf_ref, iv_ref, gc_ref, gt_ref = rest[4:]
        wa_w = k_ref.shape[-1]
    wb_w = ub_ref.shape[-1]
    wc_w = qs_ref.shape[-1]
    h = _rms(x_ref[...], n1_ref[...]).astype(BF16)
    lf = _log_sigmoid(_dot_nt(h, wf_ref[...]) + bf_ref[...])

    if seq_major:
        tm = x_ref.shape[0]

        @pl.when(pl.program_id(0) % tps == 0)
        def _():
            carry_s[...] = jnp.zeros_like(carry_s)

        lower = _ones_tri(tm, upper=False)
        hi, mid, lo = _split3(lf)
        c = _dot(lower, hi) + _dot(lower, mid) + _dot(lower, lo) + carry_s[...]
        carry_s[...] = c[tm - 1:tm, :]
        c3 = jnp.concatenate(_split3(c * LOG2E), axis=1)
        eq = _dot(c3, selq_ref[...]) + oneq_ref[...]
        ek = _dot(c3, selk_ref[...]) + onek_ref[...]

        pa = _dot_nt(h, wa_ref[:2 * wa_w, :])
        kv_t = _dot_nt(wa_ref[wa_w:, :], h)
        kt_ref[...] = kv_t[:wa_w, :]
        vt_ref[...] = kv_t[wa_w:, :]
        vtb_ref[...] = kv_t[wa_w:, :].astype(BF16)
        lft_ref[...] = _log_sigmoid(_dot_nt(wf_ref[...], h)[:bfc_ref.shape[0], :] + bfc_ref[...])
        q = pa[:, :wa_w] * qscale
        k = pa[:, wa_w:]
        qa_parts, ka_parts = [], []
        for p in range(wa_w // LANES):
            ls = slice(p * LANES, (p + 1) * LANES)
            qa_parts += [q[:, ls], eq[:, ls]]
            ka_parts += [k[:, ls], ek[:, ls]]
        qa_ref[...] = jnp.concatenate(qa_parts, axis=1).astype(BF16)
        ka_ref[...] = jnp.concatenate(ka_parts, axis=1).astype(BF16)
    else:
        pa = _dot_nt(h, wa_ref[...])
        q_ref[...] = (pa[:, :wa_w] * qscale).astype(q_ref.dtype)
        k_ref[...] = pa[:, wa_w:2 * wa_w]
        v_ref[...] = pa[:, 2 * wa_w:]
        lf_ref[...] = lf

    r_b, r_c = 2 * wb_w, 2 * wb_w + 4 * wc_w
    pb = _dot_nt(h, wr_ref[:r_b, :])
    ub_ref[...] = _gelu_tanh(pb[:, :wb_w])
    vb_ref[...] = _rms(_gelu_tanh(pb[:, wb_w:]), gn_ref[...])

    pc = _dot_nt(h, wr_ref[r_b:r_c, :])
    lb = lb_ref[...]
    qs_ref[...] = _silu(pc[:, :wc_w])
    f = lb + (1.0 - lb) * _sigmoid(pc[:, wc_w:2 * wc_w])
    gl_ref[...] = jnp.log(f)
    kf_ref[...] = 1.0 - f
    iv_ref[...] = pc[:, 2 * wc_w:3 * wc_w]
    gc_ref[...] = _silu(pc[:, 3 * wc_w:])

    gt_ref[...] = _sigmoid(_dot_nt(h, wr_ref[r_c:, :]))


def _inproj(x, layer, qscale, n1, wa, wf, wr, bf_row, bf_col, gn, lb, stacked=None):
    n, d = x.shape
    tm = min(ROW_TILE, n)
    wa_w, wb_w, wc_w = wa.shape[1] // 3, gn.shape[-1], lb.shape[-1]
    wg_w = wr.shape[1] - 2 * wb_w - 4 * wc_w
    nh = bf_col.shape[1]

    def rows(w):
        return pl.BlockSpec((tm, w), lambda i: (i, 0))

    extra_in, extra_specs, aliases, scratch, tps = [], [], {}, [], 1
    if stacked is None:
        head_shape = (jax.ShapeDtypeStruct((n, wa_w), BF16),
                      jax.ShapeDtypeStruct((n, wa_w), F32), jax.ShapeDtypeStruct((n, wa_w), F32),
                      jax.ShapeDtypeStruct((n, LANES), F32))
        head_specs = (rows(wa_w), rows(wa_w), rows(wa_w), rows(LANES))
    else:
        depth, b, prev = stacked
        t = n // b
        tps = t // tm

        def feat_major(f):
            return (jax.ShapeDtypeStruct((depth, b, f, t), F32),
                    pl.BlockSpec((None, None, f, tm), lambda i: (layer, i // tps, 0, i % tps)))

        (kt_shape, kt_spec), (lft_shape, lft_spec) = feat_major(wa_w), feat_major(nh)
        aug_shape = jax.ShapeDtypeStruct((n, 2 * wa_w), BF16)
        head_shape = (aug_shape, aug_shape, kt_shape, kt_shape,
                      jax.ShapeDtypeStruct((b, tps, wa_w, tm), BF16), lft_shape)
        head_specs = (rows(2 * wa_w), rows(2 * wa_w), kt_spec, kt_spec,
                      pl.BlockSpec((None, None, wa_w, tm), lambda i: (i // tps, i % tps, 0, 0)), lft_spec)
        sel = _decay_selectors(nh, wa_w // nh)
        extra_in = list(sel)
        extra_specs = [_const_spec(s.shape) for s in sel]
        scratch = [pltpu.VMEM((1, LANES), F32)]
        if prev is not None:
            extra_in += list(prev)
            extra_specs += [pl.BlockSpec(memory_space=pl.ANY)] * len(prev)
            aliases = {13: 2, 14: 3, 15: 5}

    out_shape = head_shape + (
        jax.ShapeDtypeStruct((n, wb_w), F32),
        jax.ShapeDtypeStruct((n, wb_w), F32),
        jax.ShapeDtypeStruct((n, wc_w), F32),
        jax.ShapeDtypeStruct((n, wc_w), F32),
        jax.ShapeDtypeStruct((n, wc_w), F32),
        jax.ShapeDtypeStruct((n, wc_w), F32),
        jax.ShapeDtypeStruct((n, wc_w), F32),
        jax.ShapeDtypeStruct((n, wg_w), F32),
    )
    out_specs = head_specs + (rows(wb_w), rows(wb_w), rows(wc_w), rows(wc_w), rows(wc_w), rows(wc_w),
                              rows(wc_w), rows(wg_w))
    params = (n1, wa, wf, wr, bf_row, bf_col, gn, lb)
    in_specs = [rows(d)] + [_layer_spec(p, layer) for p in params] + extra_specs
    kern = functools.partial(_inproj_kernel, n_alias=len(aliases), seq_major=stacked is not None,
                             qscale=qscale, tps=tps)
    return pl.pallas_call(
        kern, grid=(n // tm,), in_specs=in_specs, out_specs=out_specs, out_shape=out_shape,
        scratch_shapes=scratch, input_output_aliases=aliases,
        compiler_params=_params("parallel" if stacked is None else "arbitrary"), name="inproj",
    )(x, *params, *extra_in)


def _decay_selectors(nh, hd):
    sel_q = np.zeros((N_SPLIT * LANES, nh * hd), np.float32)
    sel_k = np.zeros((N_SPLIT * LANES, nh * hd), np.float32)
    one_q = np.zeros((1, nh * hd), np.float32)
    one_k = np.zeros((1, nh * hd), np.float32)
    for h in range(nh):
        for p in range(N_SPLIT):
            sel_k[p * LANES + h, h * hd + p] = -1.0
            one_k[0, h * hd + N_SPLIT + p] = 1.0
            one_q[0, h * hd + p] = 1.0
            sel_q[p * LANES + h, h * hd + N_SPLIT + p] = 1.0
    return (jnp.asarray(sel_q, BF16), jnp.asarray(sel_k, BF16), jnp.asarray(one_q), jnp.asarray(one_k))


def _fox_prompt_kernel(qa_ref, ka_ref, vt_ref, o_ref, qh_s, m_s, l_s, acc_s, *, hd, tq):
    qi = pl.program_id(1)
    heads_per_slab = LANES // hd
    n_slab = vt_ref.shape[2] // LANES
    heads = [(sl, hh) for sl in range(n_slab) for hh in range(heads_per_slab)]
    lane2 = lax.broadcasted_iota(jnp.int32, (tq, 2 * LANES), 1) % LANES
    key = lax.broadcasted_iota(jnp.int32, (tq, tq), 0)
    qry = lax.broadcasted_iota(jnp.int32, (tq, tq), 1)

    for n, (sl, hh) in enumerate(heads):
        qa = qa_ref[0, :, sl * 2 * LANES:(sl + 1) * 2 * LANES].astype(F32)
        qh_s[n] = jnp.where(lane2 // hd == hh, qa, 0.0).astype(BF16)
        m_s[n] = jnp.full((1, tq), NEG_INF, F32)
        l_s[n] = jnp.zeros((1, tq), F32)
        acc_s[n] = jnp.zeros((LANES, tq), F32)

    def tiles(js):
        s_all = []
        for n, (sl, _) in enumerate(heads):
            s_n = []
            for j, masked in js:
                k0 = pl.multiple_of(j * tq, tq)
                s = _dot_nt(ka_ref[0, pl.ds(k0, tq), sl * 2 * LANES:(sl + 1) * 2 * LANES], qh_s[n])
                s_n.append(jnp.where(key <= qry, s, NEG_INF) if masked else s)
            s_all.append(s_n)
        m_old = [m_s[n] for n in range(len(heads))]
        m_new = []
        for m, s_n in zip(m_old, s_all):
            for s in s_n:
                m = jnp.maximum(m, jnp.max(s, axis=0, keepdims=True))
            m_new.append(m)
        p_all = [[jnp.exp2(s - m) for s in s_n] for s_n, m in zip(s_all, m_new)]
        for n, (sl, _) in enumerate(heads):
            alpha = jnp.exp2(m_old[n] - m_new[n])
            l_new = alpha * l_s[n]
            acc = alpha * acc_s[n]
            for (j, _), p in zip(js, p_all[n]):
                l_new = l_new + jnp.sum(p, axis=0, keepdims=True)
                acc = acc + _dot(vt_ref[0, j, sl * LANES:(sl + 1) * LANES, :], p.astype(BF16))
            l_s[n] = l_new
            acc_s[n] = acc
            m_s[n] = m_new[n]

    def body(jj, carry):
        tiles([(2 * jj, False), (2 * jj + 1, False)])
        return carry

    lax.fori_loop(0, qi // 2, body, 0)

    @pl.when(qi % 2 == 1)
    def _():
        tiles([(qi - 1, False), (qi, True)])

    @pl.when(qi % 2 == 0)
    def _():
        tiles([(qi, True)])

    for sl in range(n_slab):
        parts = [(acc_s[n] / l_s[n])[hh * hd:(hh + 1) * hd, :] for n, (s2, hh) in enumerate(heads) if s2 == sl]
        o_ref[0, :, sl * LANES:(sl + 1) * LANES] = jnp.concatenate(parts, axis=0).T.astype(o_ref.dtype)


def _fox_prompt(qa, ka, vt, nh):
    b, nkv, w, tq = vt.shape
    t = nkv * tq
    hd = w // nh
    kern = functools.partial(_fox_prompt_kernel, hd=hd, tq=tq)
    return pl.pallas_call(
        kern, grid=(b, nkv),
        in_specs=[pl.BlockSpec((1, tq, 2 * w), lambda i, j: (i, j, 0)),
                  pl.BlockSpec((1, t, 2 * w), lambda i, j: (i, 0, 0)),
                  pl.BlockSpec((1, nkv, w, tq), lambda i, j: (i, 0, 0, 0))],
        out_specs=pl.BlockSpec((1, tq, w), lambda i, j: (i, j, 0)),
        out_shape=jax.ShapeDtypeStruct((b, t, w), BF16),
        scratch_shapes=[pltpu.VMEM((nh, tq, 2 * LANES), BF16), pltpu.VMEM((nh, 1, tq), F32),
                        pltpu.VMEM((nh, 1, tq), F32), pltpu.VMEM((nh, LANES, tq), F32)],
        compiler_params=_params("parallel", "arbitrary"), name="fox_prompt",
    )(qa, ka, vt)


def _fox_sample_kernel(pt_ref, q_ref, kn_ref, vn_ref, lnt_ref, *rest, nh, hd, g_pages):
    k_pages = rest[:g_pages]
    v_pages = rest[g_pages:2 * g_pages]
    f_pages = rest[2 * g_pages:3 * g_pages]
    o_ref, qbd_s, m_s, l_s, acc_s, carry_s, cn_s = rest[3 * g_pages:]
    del pt_ref
    step = pl.program_id(1)
    nt = q_ref.shape[0]
    w = q_ref.shape[1]
    page = k_pages[0].shape[-1]
    rows = nt * nh

    def update(s3, vs):
        m_old = m_s[...]
        m_new = jnp.maximum(m_old, jnp.max(s3, axis=-1, keepdims=True))
        alpha = jnp.exp2(m_old - m_new)
        p3 = jnp.exp2(s3 - m_new)
        l_s[...] = alpha * l_s[...] + jnp.sum(p3, axis=-1, keepdims=True)
        p = p3.reshape(rows, p3.shape[-1]).astype(BF16)
        pv = None
        off = 0
        for vt in vs:
            part = _dot_nt(p[:, off:off + vt.shape[1]], vt)
            pv = part if pv is None else pv + part
            off += vt.shape[1]
        acc_s[...] = alpha * acc_s[...] + pv.reshape(nt, nh, w)
        m_s[...] = m_new

    @pl.when(step == 0)
    def _():
        rep_r = lax.broadcasted_iota(jnp.int32, (rows, nt), 0) // nh
        rep_c = lax.broadcasted_iota(jnp.int32, (rows, nt), 1)
        q_rep = _dot(jnp.where(rep_r == rep_c, 1.0, 0.0), q_ref[...])
        head_of_lane = lax.broadcasted_iota(jnp.int32, (rows, w), 1) // hd
        head_of_row = lax.broadcasted_iota(jnp.int32, (rows, w), 0) % nh
        qbd = jnp.where(head_of_lane == head_of_row, q_rep, 0.0).astype(BF16)
        qbd_s[...] = qbd

        lane = lax.broadcasted_iota(jnp.int32, (nh, page), 1)
        cn = jnp.concatenate([lnt_ref[...] * LOG2E, jnp.zeros((nh, page - nt), F32)], axis=1)
        sh = 1
        while sh < nt:
            cn = cn + jnp.where(lane >= sh, pltpu.roll(cn, sh, 1), 0.0)
            sh *= 2
        cn_s[...] = cn
        carry_s[...] = jnp.zeros_like(carry_s)
        m_s[...] = jnp.full(m_s.shape, NEG_INF, F32)
        l_s[...] = jnp.zeros_like(l_s)
        acc_s[...] = jnp.zeros_like(acc_s)

        kn = jnp.concatenate([kn_ref[...], jnp.zeros((page - nt, w), F32)], axis=0).astype(BF16)
        vn = jnp.concatenate([vn_ref[...], jnp.zeros((page - nt, w), F32)], axis=0).T.astype(BF16)
        s3 = _dot_nt(qbd, kn).reshape(nt, nh, page)
        cq = jnp.stack([cn[:, t:t + 1] for t in range(nt)], axis=0)
        s3 = s3 + (cq - cn[None, :, :])
        tk = lax.broadcasted_iota(jnp.int32, (nt, nh, page), 2)
        tq = lax.broadcasted_iota(jnp.int32, (nt, nh, page), 0)
        update(jnp.where(tk <= tq, s3, NEG_INF), [vn])

    lane = lax.broadcasted_iota(jnp.int32, (nh, page), 1)
    cn = cn_s[...]
    cq = jnp.stack([cn[:, t:t + 1] for t in range(nt)], axis=0)
    qbd = qbd_s[...]
    carry = carry_s[...]
    s_tiles, v_tiles = [], []
    for g in range(g_pages):
        x = f_pages[g][...] * LOG2E
        incl = x
        sh = 1
        while sh < page:
            incl = incl + jnp.where(lane + sh < page, pltpu.roll(incl, page - sh, 1), 0.0)
            sh *= 2
        r = carry + (incl - x)
        carry = carry + incl[:, 0:1]
        s3 = _dot(qbd, k_pages[g][...].reshape(w, page).astype(BF16)).reshape(nt, nh, page)
        s_tiles.append(s3 + (cq + r[None, :, :]))
        v_tiles.append(v_pages[g][...].reshape(w, page).astype(BF16))
    carry_s[...] = carry
    update(jnp.concatenate(s_tiles, axis=-1), v_tiles)

    @pl.when(step == pl.num_programs(1) - 1)
    def _():
        head_of_lane = lax.broadcasted_iota(jnp.int32, (nt, nh, w), 2) // hd
        head_of_row = lax.broadcasted_iota(jnp.int32, (nt, nh, w), 1)
        o3 = jnp.where(head_of_lane == head_of_row, acc_s[...] / l_s[...], 0.0)
        o2 = o3.reshape(rows, w).astype(BF16)
        sel_r = lax.broadcasted_iota(jnp.int32, (nt, rows), 0)
        sel_c = lax.broadcasted_iota(jnp.int32, (nt, rows), 1) // nh
        sel = jnp.where(sel_r == sel_c, 1.0, 0.0).astype(BF16)
        o_ref[...] = _dot(sel, o2).astype(o_ref.dtype)


def _fox_sample(q, kn, vn, lnt, cache_kt, cache_vt, cache_ft, page_table, layer):
    nseq, n_pages = page_table.shape
    n, w = q.shape
    nt = n // nseq
    _, _, nh, hd, page = cache_kt.shape
    g_pages = min(PAGES_PER_STEP, n_pages)
    n_steps = n_pages // g_pages

    def page_spec(shape, g):
        nd = len(shape)

        def idx(b, s, pt):
            return (layer, pt[b * n_pages + (n_pages - 1 - (s * g_pages + g))]) + (0,) * nd
        return pl.BlockSpec((None, None) + shape, idx)

    seq_rows = pl.BlockSpec((nt, w), lambda b, s, pt: (b, 0))
    in_specs = [seq_rows, seq_rows, seq_rows, pl.BlockSpec((None, nh, nt), lambda b, s, pt: (b, 0, 0))]
    in_specs += [page_spec((nh, hd, page), g) for g in range(g_pages)]
    in_specs += [page_spec((nh, hd, page), g) for g in range(g_pages)]
    in_specs += [page_spec((nh, page), g) for g in range(g_pages)]
    kern = functools.partial(_fox_sample_kernel, nh=nh, hd=hd, g_pages=g_pages)
    grid_spec = pltpu.PrefetchScalarGridSpec(
        num_scalar_prefetch=1, grid=(nseq, n_steps), in_specs=in_specs,
        out_specs=pl.BlockSpec((nt, w), lambda b, s, pt: (b, 0)),
        scratch_shapes=[pltpu.VMEM((nt * nh, w), BF16),
                        pltpu.VMEM((nt, nh, 1), F32), pltpu.VMEM((nt, nh, 1), F32),
                        pltpu.VMEM((nt, nh, w), F32),
                        pltpu.VMEM((nh, page), F32), pltpu.VMEM((nh, page), F32)])
    return pl.pallas_call(
        kern, grid_spec=grid_spec, out_shape=jax.ShapeDtypeStruct((n, w), F32),
        compiler_params=_params("parallel", "arbitrary"), name="fox_sample",
    )(page_table.reshape(-1), q, kn, vn, lnt,
      *([cache_kt] * g_pages), *([cache_vt] * g_pages), *([cache_ft] * g_pages))


def _hgrn_kernel(q_ref, g_ref, kf_ref, v_ref, gc_ref, nw_ref, s0_ref, y_ref, sn_ref, st_s, *, c, nsub, kd):
    ti = pl.program_id(1)
    nh = s0_ref.shape[1]

    @pl.when(ti == 0)
    def _():
        for h in range(nh):
            st_s[h] = s0_ref[0, h].T

    blk = SUBLANES
    group = HGRN_GROUP if nsub % HGRN_GROUP == 0 else 1
    rowi = lax.broadcasted_iota(jnp.int32, (blk, 1), 0)
    subi = lax.broadcasted_iota(jnp.int32, (blk, kd), 0)

    def cumsum_blocks(g):
        b_blocks = []
        off = None
        for i in range(c // blk):
            x = g[i * blk:(i + 1) * blk, :]
            sh = 1
            while sh < blk:
                x = x + jnp.where(subi >= sh, pltpu.roll(x, sh, 0), 0.0)
                sh *= 2
            x = x if off is None else x + off
            off = x[blk - 1:blk, :]
            b_blocks.append(x)
        return b_blocks

    def cat(blocks):
        return jnp.concatenate(blocks, axis=0) if len(blocks) > 1 else blocks[0]

    def body(i, carry):
        streams = []
        for u in range(group):
            rs = pl.ds(0 if nsub == 1 else pl.multiple_of((i * group + u) * c, c), c)
            for h in range(nh):
                hs = slice(h * kd, (h + 1) * kd)
                q, kf, v = q_ref[0, rs, hs], kf_ref[0, rs, hs], v_ref[0, rs, hs]
                b_blocks = cumsum_blocks(g_ref[0, rs, hs])
                b = cat(b_blocks)
                a_prev = [None]
                for n in range(1, len(b_blocks)):
                    e = b_blocks[n - 1][blk - 1:blk, :]
                    a_prev.append(_dot_nt((q[n * blk:(n + 1) * blk, :] * jnp.exp(b_blocks[n] - e)).astype(BF16),
                                          (kf[:n * blk, :] * jnp.exp(e - b[:n * blk, :])).astype(BF16)))
                streams.append((rs, h, hs, q, kf, v, b_blocks, b, a_prev))

        diag = []
        for rs, h, hs, q, kf, v, b_blocks, b, a_prev in streams:
            o_blocks = []
            for n, bn in enumerate(b_blocks):
                r0 = n * blk
                qn = q[r0:r0 + blk, :]
                on = jnp.zeros((blk, kd), F32)
                for s in range(blk):
                    xs = qn * jnp.exp(jnp.minimum(bn - bn[s:s + 1, :], 0.0)) * kf[r0 + s:r0 + s + 1, :]
                    a = jnp.sum(xs, axis=-1, keepdims=True)
                    on = on + jnp.where(rowi >= s, a, 0.0) * v[r0 + s:r0 + s + 1, :]
                o_blocks.append(on)
            diag.append(o_blocks)

        for (rs, h, hs, q, kf, v, b_blocks, b, a_prev), o_blocks in zip(streams, diag):
            st = st_s[h]
            for n in range(1, len(b_blocks)):
                o_blocks[n] = o_blocks[n] + _dot(a_prev[n].astype(BF16), v[:n * blk, :].astype(BF16))
            o = cat(o_blocks) + _dot_nt((q * jnp.exp(b)).astype(BF16), st.astype(BF16))
            bl = b_blocks[-1][blk - 1:blk, :]
            kd_ = kf * jnp.exp(bl - b)
            st_s[h] = st * jnp.exp(bl) + _dot_tn(v.astype(BF16), kd_.astype(BF16))
            y = _rms(o, nw_ref[:, hs]) * gc_ref[0, rs, hs]
            y_ref[0, rs, hs] = y.astype(y_ref.dtype)
        return carry

    if nsub == 1:
        body(0, 0)
    else:
        lax.fori_loop(0, nsub // group, body, 0)

    @pl.when(ti == pl.num_programs(1) - 1)
    def _():
        for h in range(nh):
            sn_ref[0, h] = st_s[h].T


def _hgrn(qs, gl, kf, iv, gcs, nw, nw_layer, s0_all, layer):
    b, t, w = qs.shape
    _, _, nh, kd, vd = s0_all.shape
    c = HGRN_SUB if t % HGRN_SUB == 0 else t
    tt = HGRN_TILE if t % HGRN_TILE == 0 else t
    kern = functools.partial(_hgrn_kernel, c=c, nsub=tt // c, kd=kd)
    seq = pl.BlockSpec((1, tt, w), lambda i, j: (i, j, 0))
    state = pl.BlockSpec((1, nh, kd, vd), lambda i, j: (i, 0, 0, 0))
    state_in = pl.BlockSpec((None, 1, nh, kd, vd), lambda i, j: (layer, i, 0, 0, 0))
    return pl.pallas_call(
        kern, grid=(b, t // tt),
        in_specs=[seq, seq, seq, seq, seq, _layer_spec(nw, nw_layer), state_in],
        out_specs=(seq, state),
        out_shape=(jax.ShapeDtypeStruct((b, t, nh * vd), BF16), jax.ShapeDtypeStruct((b, nh, kd, vd), F32)),
        scratch_shapes=[pltpu.VMEM((nh, vd, kd), F32)],
        compiler_params=_params("parallel", "arbitrary"), name="hgrn",
    )(qs, gl, kf, iv, gcs, nw, s0_all)


def _merge_kernel(x_ref, ya_ref, ub_ref, vb_ref, yc_ref, gt_ref, gm_ref, gb_ref,
                  woa_ref, wob_ref, woc_ref, wo_ref, o_ref, *, n_groups):
    tm = x_ref.shape[0]
    d = x_ref.shape[1]
    cn = gm_ref.shape[1]
    gw = ub_ref.shape[1] // n_groups
    yb_rows = []
    for ci in range(tm // cn):
        rs = slice(ci * cn, (ci + 1) * cn)
        vb = vb_ref[rs, :].astype(BF16)
        gb = gb_ref[...]
        parts = []
        for g in range(n_groups):
            parts.append(_dot(gm_ref[g], vb[:, g * gw:(g + 1) * gw]) + gb[:, g:g + 1])
        yb_rows.append(ub_ref[rs, :] * jnp.concatenate(parts, axis=1))
    yb = jnp.concatenate(yb_rows, axis=0) if len(yb_rows) > 1 else yb_rows[0]
    gt = gt_ref[...]
    merged = (gt[:, :d] * _dot(ya_ref[...].astype(BF16), woa_ref[...])
              + gt[:, d:2 * d] * _dot(yb.astype(BF16), wob_ref[...])
              + gt[:, 2 * d:] * _dot(yc_ref[...].astype(BF16), woc_ref[...]))
    o_ref[...] = x_ref[...] + _dot(merged.astype(BF16), wo_ref[...])


def _merge(x, ya, ub, vb, yc, gt, layer, gm, gb, woa, wob, woc, wo):
    n, d = x.shape
    tm = min(MERGE_TILE, n)

    def rows(w):
        return pl.BlockSpec((tm, w), lambda i: (i, 0))

    kern = functools.partial(_merge_kernel, n_groups=gm.shape[1])
    return pl.pallas_call(
        kern, grid=(n // tm,),
        in_specs=[rows(d), rows(ya.shape[1]), rows(ub.shape[1]), rows(vb.shape[1]), rows(yc.shape[1]),
                  rows(gt.shape[1])] + [_layer_spec(p, layer) for p in (gm, gb, woa, wob, woc, wo)],
        out_specs=rows(d), out_shape=jax.ShapeDtypeStruct((n, d), F32),
        compiler_params=_params("parallel"), name="merge",
    )(x, ya, ub, vb, yc, gt, gm, gb, woa, wob, woc, wo)


def _ffn_kernel(x_ref, n2_ref, wup_ref, wcv_ref, bcv_ref, wdn_ref, buf_ref, nf_ref,
                y_ref, nb_ref, carry_s, ext_s, h_s, act_s, *, shift, base, cw, final_norm):
    ti = pl.program_id(1)
    tm = x_ref.shape[1]
    d_ff = wdn_ref.shape[0]
    hist = 2 * shift

    @pl.when(ti == 0)
    def _():
        carry_s[...] = buf_ref[0]

    n_chunks = d_ff // cw
    n_slots = ext_s.shape[0]

    def col0(c, half):
        return half * d_ff + c * cw

    def stage(c):
        for half in range(2):
            cols = slice(col0(c, half), col0(c, half) + cw)
            ext = ext_s.at[(2 * c + half) % n_slots]
            ext[base - hist:base, :] = carry_s[:, cols]
            ext[base:base + tm, :] = _dot(h_s[...], wup_ref[:, cols])

    def conv(c, half):
        cols = slice(col0(c, half), col0(c, half) + cw)
        ext = ext_s.at[(2 * c + half) % n_slots]
        wcv = wcv_ref[:, cols]
        out = bcv_ref[:, cols] + (wcv[0:1, :] * ext[base - hist:base - hist + tm, :]
                                  + wcv[1:2, :] * ext[base - shift:base - shift + tm, :]
                                  + wcv[2:3, :] * ext[base:base + tm, :])
        carry_s[:, cols] = ext[base + tm - hist:base + tm, :]
        return out

    x = x_ref[0]
    h_s[...] = _rms(x, n2_ref[...]).astype(BF16)
    y = x
    stage(0)
    for c in range(n_chunks):
        if c + 1 < n_chunks:
            stage(c + 1)
        act_s[:, c * cw:(c + 1) * cw] = (_silu(conv(c, 0)) * conv(c, 1)).astype(BF16)
        if (c + 1) % FFN_DOWN_CHUNKS == 0 or c + 1 == n_chunks:
            r0 = (c // FFN_DOWN_CHUNKS) * FFN_DOWN_CHUNKS * cw
            y = y + _dot(act_s[:, r0:(c + 1) * cw], wdn_ref[r0:(c + 1) * cw, :])
    if final_norm:
        y = _rms(y, nf_ref[...])
    y_ref[0] = y
    nb_ref[0] = carry_s[...]


def _ffn(x, layer, n2, wup, wcv, bcv, wdn, buf, nf, shift, final_norm):
    nseq, t, d = x.shape
    tm = min(FFN_TILE, t)
    hist = 2 * shift
    base = -(-hist // SUBLANES) * SUBLANES
    kern = functools.partial(_ffn_kernel, shift=shift, base=base, cw=FFN_COLS, final_norm=final_norm)
    state = pl.BlockSpec((1, hist, buf.shape[2]), lambda i, j: (i, 0, 0))
    return pl.pallas_call(
        kern, grid=(nseq, t // tm),
        in_specs=[pl.BlockSpec((1, tm, d), lambda i, j: (i, j, 0)), _layer_spec(n2, layer),
                  _layer_spec(wup, layer), _layer_spec(wcv, layer), _layer_spec(bcv, layer),
                  _layer_spec(wdn, layer), state, _const_spec(nf.shape)],
        out_specs=(pl.BlockSpec((1, tm, d), lambda i, j: (i, j, 0)), state),
        out_shape=(jax.ShapeDtypeStruct(x.shape, F32), jax.ShapeDtypeStruct(buf.shape, F32)),
        scratch_shapes=[pltpu.VMEM((hist, buf.shape[2]), F32),
                        pltpu.VMEM((FFN_EXT_SLOTS, base + tm, FFN_COLS), F32),
                        pltpu.VMEM((tm, d), BF16), pltpu.VMEM((tm, wdn.shape[1]), BF16)],
        compiler_params=_params("parallel", "arbitrary"), name="conv_ffn",
    )(x, n2, wup, wcv, bcv, wdn, buf, nf)


def kernel(x_prompt, x_sample, cache_k, cache_v, cache_logf, state_hgrn, state_conv, page_table,
           norm1, w_in, b_f, gmlp_norm, w_s, b_s, hgrn_norm, hgrn_lb, w_oa, w_ob, w_oc, w_o,
           norm2, w_up, w_conv, b_conv, w_down, norm_f):
    depth = w_in.shape[0]
    bp, tp, d = x_prompt.shape
    bs, ts, _ = x_sample.shape
    _, n_phys, page, nh_a, hd_a = cache_k.shape
    w_a = nh_a * hd_a
    n_grp, chunk_b, _ = w_s.shape[1:]
    w_b = gmlp_norm.shape[1]
    _, _, nh_c, k_c, v_c = state_hgrn.shape
    w_c = nh_c * v_c
    d_ff = w_down.shape[1]

    sm = jax.nn.softmax(hgrn_lb.astype(F32), axis=0)
    lower = jnp.cumsum(sm, axis=0) - sm[:1]

    cache_kt = jnp.transpose(cache_k, (0, 1, 3, 4, 2))
    cache_vt = jnp.transpose(cache_v, (0, 1, 3, 4, 2))
    cache_ft = jnp.swapaxes(cache_logf, 2, 3)

    w_t = jnp.swapaxes(w_in, 1, 2).astype(BF16)
    o_f = 3 * w_a
    inproj_w = (norm1.reshape(depth, 1, d), w_t[:, :o_f],
                jnp.pad(w_t[:, o_f:o_f + nh_a], ((0, 0), (0, LANES - nh_a), (0, 0))), w_t[:, o_f + nh_a:],
                jnp.pad(b_f, ((0, 0), (0, LANES - nh_a))).reshape(depth, 1, LANES), b_f.reshape(depth, nh_a, 1),
                gmlp_norm.reshape(depth, 1, w_b), lower.reshape(depth, 1, w_c))
    qscale = hd_a ** -0.5 * LOG2E
    nw = hgrn_norm.reshape(depth, 1, w_c)
    tril_w = jnp.where(jnp.tril(jnp.ones((chunk_b, chunk_b), bool)), w_s, 0)
    merge_w = (w_oa.astype(BF16), w_ob.astype(BF16), w_oc.astype(BF16), w_o.astype(BF16))
    gate_p = (tril_w.astype(BF16), jnp.swapaxes(b_s, 1, 2))
    same_seq = np.kron(np.eye(bs, dtype=np.float32), np.ones((ts, ts), np.float32))
    rep = np.tile(np.eye(ts, dtype=np.float32), (bs, 1))
    gate_s = ((jnp.einsum("rt,lgts,cs->lgrc", rep, tril_w[:, :, :ts, :ts], rep) * same_seq).astype(BF16),
              jnp.tile(jnp.swapaxes(b_s[:, :, :ts], 1, 2), (1, bs, 1)))
    ffn_w = (norm2.reshape(depth, 1, d), w_up.astype(BF16), w_conv, b_conv.reshape(depth, 1, 2 * d_ff),
             w_down.astype(BF16))
    nf = norm_f.reshape(1, d)
    s0 = jnp.zeros((1, bp, nh_c, k_c, v_c), F32)
    buf0 = jnp.zeros((bp, 2, 2 * d_ff), F32)

    xp = x_prompt.reshape(bp * tp, d)
    xs = x_sample.reshape(bs * ts, d)
    st_p, st_s = [], []
    kv_bufs = None
    for l in range(depth):
        last = l == depth - 1

        (qa, ka, kt_all, vt_all, vt, lft_all, ub, vb, qs, gl, kf, iv, gcs, gt) = _inproj(
            xp, l, qscale, *inproj_w, stacked=(depth, bp, kv_bufs))
        kv_bufs = (kt_all, vt_all, lft_all)
        ya = _fox_prompt(qa.reshape(bp, tp, 2 * w_a), ka.reshape(bp, tp, 2 * w_a), vt, nh_a)
        yc, s_new = _hgrn(*(a.reshape(bp, tp, w_c) for a in (qs, gl, kf, iv, gcs)), nw, l, s0, 0)
        x1 = _merge(xp, ya.reshape(bp * tp, w_a), ub, vb, yc.reshape(bp * tp, w_c), gt, l, *gate_p, *merge_w)
        x2, conv_p = _ffn(x1.reshape(bp, tp, d), l, *ffn_w, buf0, nf, 1, last)
        xp = x2.reshape(bp * tp, d)
        st_p.append((s_new, conv_p))

        (q, k, v, lf, ub, vb, qs, gl, kf, iv, gcs, gt) = _inproj(xs, l, qscale, *inproj_w)
        k4 = k.reshape(bs, ts, nh_a, hd_a)
        v4 = v.reshape(bs, ts, nh_a, hd_a)
        lf3 = lf[:, :nh_a].reshape(bs, ts, nh_a)
        ya = _fox_sample(q.astype(F32), k, v, jnp.swapaxes(lf3, 1, 2),
                         cache_kt, cache_vt, cache_ft, page_table, l)
        yc, s_new = _hgrn(*(a.reshape(bs, ts, w_c) for a in (qs, gl, kf, iv, gcs)), nw, l, state_hgrn, l)
        x1 = _merge(xs, ya, ub, vb, yc.reshape(bs * ts, w_c), gt, l, *gate_s, *merge_w)
        x1t = jnp.swapaxes(x1.reshape(bs, ts, d), 0, 1).reshape(1, ts * bs, d)
        buf_t = jnp.swapaxes(state_conv[l], 0, 1).reshape(1, 2 * bs, 2 * d_ff)
        x2t, nb = _ffn(x1t, l, *ffn_w, buf_t, nf, bs, last)
        xs = jnp.swapaxes(x2t.reshape(ts, bs, d), 0, 1).reshape(bs * ts, d)
        conv_s = nb.reshape(2, bs, 2 * d_ff)
        st_s.append((k4, v4, lf3, vb.reshape(bs, ts, w_b), s_new, jnp.swapaxes(conv_s, 0, 1)))

    def stk(sts, i):
        return jnp.stack([st[i] for st in sts], axis=0)

    kt_all, vt_all, lft_all = kv_bufs
    k_prompt = jnp.transpose(kt_all.reshape(depth, bp, nh_a, hd_a, tp), (0, 1, 4, 2, 3))
    v_prompt = jnp.transpose(vt_all.reshape(depth, bp, nh_a, hd_a, tp), (0, 1, 4, 2, 3))
    return (xp.reshape(bp, tp, d), xs.reshape(bs, ts, d),
            k_prompt, v_prompt, jnp.swapaxes(lft_all, 2, 3), stk(st_p, 0), stk(st_p, 1),
            stk(st_s, 0), stk(st_s, 1), stk(st_s, 2), stk(st_s, 3), stk(st_s, 4), stk(st_s, 5))
```

```python
import functools

import numpy as np

import jax
import jax.numpy as jnp
from jax import lax
from jax.experimental import pallas as pl
from jax.experimental.pallas import tpu as pltpu

F32 = jnp.float32
BF16 = jnp.bfloat16
EPS = 1e-6
NEG_INF = float("-inf")
LOG2E = 1.4426950408889634

LANES = 128
SUBLANES = 8
VMEM_LIMIT_BYTES = 56 * 1024 * 1024

ROW_TILE = 256
MERGE_TILE = 512
FFN_TILE = 256
PAGES_PER_STEP = 32
HGRN_SUB = 16
HGRN_TILE = 256
HGRN_GROUP = 8
HGRN_SEQS = 4
FFN_COLS = 256
FFN_EXT_SLOTS = 4
FFN_DOWN_CHUNKS = 11
N_SPLIT = 3


def _dot(a, b):
    return jnp.dot(a, b, preferred_element_type=F32)


def _dot_nt(a, b):
    return lax.dot_general(a, b, (((1,), (1,)), ((), ())), preferred_element_type=F32)


def _dot_tn(a, b):
    return lax.dot_general(a, b, (((0,), (0,)), ((), ())), preferred_element_type=F32)


def _rms(x, g):
    return x * lax.rsqrt(jnp.mean(x * x, axis=-1, keepdims=True) + EPS) * g


def _sigmoid(x):
    return 1.0 / (1.0 + jnp.exp(-x))


def _silu(x):
    return x * _sigmoid(x)


def _gelu_tanh(x):
    c = 0.7978845608028654
    return x * (0.5 * (1.0 + jnp.tanh(c * (x + 0.044715 * (x * x * x)))))


def _log_sigmoid(z):
    return jnp.minimum(z, 0.0) - jnp.log1p(jnp.exp(-jnp.abs(z)))


def _split3(x):
    hi = x.astype(BF16)
    r = x - hi.astype(F32)
    mid = r.astype(BF16)
    lo = (r - mid.astype(F32)).astype(BF16)
    return hi, mid, lo


def _ones_tri(n, upper):
    r = lax.broadcasted_iota(jnp.int32, (n, n), 0)
    c = lax.broadcasted_iota(jnp.int32, (n, n), 1)
    keep = (r <= c) if upper else (r >= c)
    return jnp.where(keep, 1.0, 0.0).astype(BF16)


def _const_spec(shape):
    nd = len(shape)
    return pl.BlockSpec(shape, lambda *_: (0,) * nd, pipeline_mode=pl.Buffered(1))


def _layer_spec(arr, layer):
    nd = arr.ndim - 1
    return pl.BlockSpec((None,) + arr.shape[1:], lambda *_: (layer,) + (0,) * nd, pipeline_mode=pl.Buffered(1))


def _params(*sem):
    return pltpu.CompilerParams(dimension_semantics=sem, vmem_limit_bytes=VMEM_LIMIT_BYTES)


def _inproj_kernel(x_ref, n1_ref, wa_ref, wf_ref, wr_ref, bf_ref, bfc_ref, gn_ref, lb_ref,
                   *rest, n_alias, seq_major, qscale, tps):
    if seq_major:
        selq_ref, selk_ref, oneq_ref, onek_ref = rest[:4]
        outs, carry_s = rest[4 + n_alias:-1], rest[-1]
        qa_ref, ka_ref, kt_ref, vt_ref, vtb_ref, lft_ref = outs[:6]
        ub_ref, vb_ref, qs_ref, gl_ref, kf_ref, iv_ref, gc_ref, gt_ref = outs[6:]
        wa_w = kt_ref.shape[0]
    else:
        q_ref, k_ref, v_ref, lf_ref = rest[:4]
        ub_ref, vb_ref, qs_ref, gl_ref, kf_ref, iv_ref, gc_ref, gt_ref = rest[4:]
        wa_w = k_ref.shape[-1]
    wb_w = ub_ref.shape[-1]
    wc_w = qs_ref.shape[-1]
    h = _rms(x_ref[...], n1_ref[...]).astype(BF16)
    lf = _log_sigmoid(_dot_nt(h, wf_ref[...]) + bf_ref[...])

    if seq_major:
        tm = x_ref.shape[0]

        @pl.when(pl.program_id(0) % tps == 0)
        def _():
            carry_s[...] = jnp.zeros_like(carry_s)

        lower = _ones_tri(tm, upper=False)
        hi, mid, lo = _split3(lf)
        c = _dot(lower, hi) + _dot(lower, mid) + _dot(lower, lo) + carry_s[...]
        carry_s[...] = c[tm - 1:tm, :]
        c3 = jnp.concatenate(_split3(c * LOG2E), axis=1)
        eq = _dot(c3, selq_ref[...]) + oneq_ref[...]
        ek = _dot(c3, selk_ref[...]) + onek_ref[...]

        pa = _dot_nt(h, wa_ref[:2 * wa_w, :])
        kv_t = _dot_nt(wa_ref[wa_w:, :], h)
        kt_ref[...] = kv_t[:wa_w, :]
        vt_ref[...] = kv_t[wa_w:, :]
        vtb_ref[...] = kv_t[wa_w:, :].astype(BF16)
        lft_ref[...] = _log_sigmoid(_dot_nt(wf_ref[...], h)[:bfc_ref.shape[0], :] + bfc_ref[...])
        q = pa[:, :wa_w] * qscale
        k = pa[:, wa_w:]
        qa_parts, ka_parts = [], []
        for p in range(wa_w // LANES):
            ls = slice(p * LANES, (p + 1) * LANES)
            qa_parts += [q[:, ls], eq[:, ls]]
            ka_parts += [k[:, ls], ek[:, ls]]
        qa_ref[...] = jnp.concatenate(qa_parts, axis=1).astype(BF16)
        ka_ref[...] = jnp.concatenate(ka_parts, axis=1).astype(BF16)
    else:
        pa = _dot_nt(h, wa_ref[...])
        q_ref[...] = (pa[:, :wa_w] * qscale).astype(q_ref.dtype)
        k_ref[...] = pa[:, wa_w:2 * wa_w]
        v_ref[...] = pa[:, 2 * wa_w:]
        lf_ref[...] = lf

    r_b, r_c = 2 * wb_w, 2 * wb_w + 4 * wc_w
    pb = _dot_nt(h, wr_ref[:r_b, :])
    ub_ref[...] = _gelu_tanh(pb[:, :wb_w])
    vb_ref[...] = _rms(_gelu_tanh(pb[:, wb_w:]), gn_ref[...])

    pc = _dot_nt(h, wr_ref[r_b:r_c, :])
    lb = lb_ref[...]
    qs_ref[...] = _silu(pc[:, :wc_w])
    f = lb + (1.0 - lb) * _sigmoid(pc[:, wc_w:2 * wc_w])
    gl_ref[...] = jnp.log(f)
    kf_ref[...] = 1.0 - f
    iv_ref[...] = pc[:, 2 * wc_w:3 * wc_w]
    gc_ref[...] = _silu(pc[:, 3 * wc_w:])

    gt_ref[...] = _sigmoid(_dot_nt(h, wr_ref[r_c:, :]))


def _inproj(x, layer, qscale, n1, wa, wf, wr, bf_row, bf_col, gn, lb, stacked=None):
    n, d = x.shape
    tm = min(ROW_TILE, n)
    wa_w, wb_w, wc_w = wa.shape[1] // 3, gn.shape[-1], lb.shape[-1]
    wg_w = wr.shape[1] - 2 * wb_w - 4 * wc_w
    nh = bf_col.shape[1]

    def rows(w):
        return pl.BlockSpec((tm, w), lambda i: (i, 0))

    extra_in, extra_specs, aliases, scratch, tps = [], [], {}, [], 1
    if stacked is None:
        head_shape = (jax.ShapeDtypeStruct((n, wa_w), BF16),
                      jax.ShapeDtypeStruct((n, wa_w), F32), jax.ShapeDtypeStruct((n, wa_w), F32),
                      jax.ShapeDtypeStruct((n, LANES), F32))
        head_specs = (rows(wa_w), rows(wa_w), rows(wa_w), rows(LANES))
    else:
        depth, b, prev = stacked
        t = n // b
        tps = t // tm

        def feat_major(f):
            return (jax.ShapeDtypeStruct((depth, b, f, t), F32),
                    pl.BlockSpec((None, None, f, tm), lambda i: (layer, i // tps, 0, i % tps)))

        (kt_shape, kt_spec), (lft_shape, lft_spec) = feat_major(wa_w), feat_major(nh)
        aug_shape = jax.ShapeDtypeStruct((n, 2 * wa_w), BF16)
        head_shape = (aug_shape, aug_shape, kt_shape, kt_shape,
                      jax.ShapeDtypeStruct((b, tps, wa_w, tm), BF16), lft_shape)
        head_specs = (rows(2 * wa_w), rows(2 * wa_w), kt_spec, kt_spec,
                      pl.BlockSpec((None, None, wa_w, tm), lambda i: (i // tps, i % tps, 0, 0)), lft_spec)
        sel = _decay_selectors(nh, wa_w // nh)
        extra_in = list(sel)
        extra_specs = [_const_spec(s.shape) for s in sel]
        scratch = [pltpu.VMEM((1, LANES), F32)]
        if prev is not None:
            extra_in += list(prev)
            extra_specs += [pl.BlockSpec(memory_space=pl.ANY)] * len(prev)
            aliases = {13: 2, 14: 3, 15: 5}

    out_shape = head_shape + (
        jax.ShapeDtypeStruct((n, wb_w), F32),
        jax.ShapeDtypeStruct((n, wb_w), F32),
        jax.ShapeDtypeStruct((n, wc_w), F32),
        jax.ShapeDtypeStruct((n, wc_w), F32),
        jax.ShapeDtypeStruct((n, wc_w), F32),
        jax.ShapeDtypeStruct((n, wc_w), F32),
        jax.ShapeDtypeStruct((n, wc_w), F32),
        jax.ShapeDtypeStruct((n, wg_w), F32),
    )
    out_specs = head_specs + (rows(wb_w), rows(wb_w), rows(wc_w), rows(wc_w), rows(wc_w), rows(wc_w),
                              rows(wc_w), rows(wg_w))
    params = (n1, wa, wf, wr, bf_row, bf_col, gn, lb)
    in_specs = [rows(d)] + [_layer_spec(p, layer) for p in params] + extra_specs
    kern = functools.partial(_inproj_kernel, n_alias=len(aliases), seq_major=stacked is not None,
                             qscale=qscale, tps=tps)
    return pl.pallas_call(
        kern, grid=(n // tm,), in_specs=in_specs, out_specs=out_specs, out_shape=out_shape,
        scratch_shapes=scratch, input_output_aliases=aliases,
        compiler_params=_params("parallel" if stacked is None else "arbitrary"), name="inproj",
    )(x, *params, *extra_in)


def _decay_selectors(nh, hd):
    sel_q = np.zeros((N_SPLIT * LANES, nh * hd), np.float32)
    sel_k = np.zeros((N_SPLIT * LANES, nh * hd), np.float32)
    one_q = np.zeros((1, nh * hd), np.float32)
    one_k = np.zeros((1, nh * hd), np.float32)
    for h in range(nh):
        for p in range(N_SPLIT):
            sel_k[p * LANES + h, h * hd + p] = -1.0
            one_k[0, h * hd + N_SPLIT + p] = 1.0
            one_q[0, h * hd + p] = 1.0
            sel_q[p * LANES + h, h * hd + N_SPLIT + p] = 1.0
    return (jnp.asarray(sel_q, BF16), jnp.asarray(sel_k, BF16), jnp.asarray(one_q), jnp.asarray(one_k))


def _fox_prompt_kernel(qa_ref, ka_ref, vt_ref, o_ref, qh_s, m_s, l_s, acc_s, *, hd, tq):
    qi = pl.program_id(1)
    heads_per_slab = LANES // hd
    n_slab = vt_ref.shape[2] // LANES
    heads = [(sl, hh) for sl in range(n_slab) for hh in range(heads_per_slab)]
    lane2 = lax.broadcasted_iota(jnp.int32, (tq, 2 * LANES), 1) % LANES
    key = lax.broadcasted_iota(jnp.int32, (tq, tq), 0)
    qry = lax.broadcasted_iota(jnp.int32, (tq, tq), 1)

    for n, (sl, hh) in enumerate(heads):
        qa = qa_ref[0, :, sl * 2 * LANES:(sl + 1) * 2 * LANES].astype(F32)
        qh_s[n] = jnp.where(lane2 // hd == hh, qa, 0.0).astype(BF16)
        m_s[n] = jnp.full((1, tq), NEG_INF, F32)
        l_s[n] = jnp.zeros((1, tq), F32)
        acc_s[n] = jnp.zeros((LANES, tq), F32)

    def tiles(js):
        s_all = []
        for n, (sl, _) in enumerate(heads):
            s_n = []
            for j, masked in js:
                k0 = pl.multiple_of(j * tq, tq)
                s = _dot_nt(ka_ref[0, pl.ds(k0, tq), sl * 2 * LANES:(sl + 1) * 2 * LANES], qh_s[n])
                s_n.append(jnp.where(key <= qry, s, NEG_INF) if masked else s)
            s_all.append(s_n)
        m_old = [m_s[n] for n in range(len(heads))]
        m_new = []
        for m, s_n in zip(m_old, s_all):
            for s in s_n:
                m = jnp.maximum(m, jnp.max(s, axis=0, keepdims=True))
            m_new.append(m)
        p_all = [[jnp.exp2(s - m) for s in s_n] for s_n, m in zip(s_all, m_new)]
        for n, (sl, _) in enumerate(heads):
            alpha = jnp.exp2(m_old[n] - m_new[n])
            l_new = alpha * l_s[n]
            acc = alpha * acc_s[n]
            for (j, _), p in zip(js, p_all[n]):
                l_new = l_new + jnp.sum(p, axis=0, keepdims=True)
                acc = acc + _dot(vt_ref[0, j, sl * LANES:(sl + 1) * LANES, :], p.astype(BF16))
            l_s[n] = l_new
            acc_s[n] = acc
            m_s[n] = m_new[n]

    def body(jj, carry):
        tiles([(2 * jj, False), (2 * jj + 1, False)])
        return carry

    lax.fori_loop(0, qi // 2, body, 0)

    @pl.when(qi % 2 == 1)
    def _():
        tiles([(qi - 1, False), (qi, True)])

    @pl.when(qi % 2 == 0)
    def _():
        tiles([(qi, True)])

    for sl in range(n_slab):
        parts = [(acc_s[n] / l_s[n])[hh * hd:(hh + 1) * hd, :] for n, (s2, hh) in enumerate(heads) if s2 == sl]
        o_ref[0, :, sl * LANES:(sl + 1) * LANES] = jnp.concatenate(parts, axis=0).T.astype(o_ref.dtype)


def _fox_prompt(qa, ka, vt, nh):
    b, nkv, w, tq = vt.shape
    t = nkv * tq
    hd = w // nh
    kern = functools.partial(_fox_prompt_kernel, hd=hd, tq=tq)
    return pl.pallas_call(
        kern, grid=(b, nkv),
        in_specs=[pl.BlockSpec((1, tq, 2 * w), lambda i, j: (i, j, 0)),
                  pl.BlockSpec((1, t, 2 * w), lambda i, j: (i, 0, 0)),
                  pl.BlockSpec((1, nkv, w, tq), lambda i, j: (i, 0, 0, 0))],
        out_specs=pl.BlockSpec((1, tq, w), lambda i, j: (i, j, 0)),
        out_shape=jax.ShapeDtypeStruct((b, t, w), BF16),
        scratch_shapes=[pltpu.VMEM((nh, tq, 2 * LANES), BF16), pltpu.VMEM((nh, 1, tq), F32),
                        pltpu.VMEM((nh, 1, tq), F32), pltpu.VMEM((nh, LANES, tq), F32)],
        compiler_params=_params("parallel", "arbitrary"), name="fox_prompt",
    )(qa, ka, vt)


def _fox_sample_kernel(pt_ref, q_ref, kn_ref, vn_ref, lnt_ref, *rest, nh, hd, g_pages):
    k_pages = rest[:g_pages]
    v_pages = rest[g_pages:2 * g_pages]
    f_pages = rest[2 * g_pages:3 * g_pages]
    o_ref, qbd_s, m_s, l_s, acc_s, carry_s, cn_s = rest[3 * g_pages:]
    del pt_ref
    step = pl.program_id(1)
    nt = q_ref.shape[0]
    w = q_ref.shape[1]
    page = k_pages[0].shape[-1]
    rows = nt * nh

    def update(s3, vs):
        m_old = m_s[...]
        m_new = jnp.maximum(m_old, jnp.max(s3, axis=-1, keepdims=True))
        alpha = jnp.exp2(m_old - m_new)
        p3 = jnp.exp2(s3 - m_new)
        l_s[...] = alpha * l_s[...] + jnp.sum(p3, axis=-1, keepdims=True)
        p = p3.reshape(rows, p3.shape[-1]).astype(BF16)
        pv = None
        off = 0
        for vt in vs:
            part = _dot_nt(p[:, off:off + vt.shape[1]], vt)
            pv = part if pv is None else pv + part
            off += vt.shape[1]
        acc_s[...] = alpha * acc_s[...] + pv.reshape(nt, nh, w)
        m_s[...] = m_new

    @pl.when(step == 0)
    def _():
        rep_r = lax.broadcasted_iota(jnp.int32, (rows, nt), 0) // nh
        rep_c = lax.broadcasted_iota(jnp.int32, (rows, nt), 1)
        q_rep = _dot(jnp.where(rep_r == rep_c, 1.0, 0.0), q_ref[...])
        head_of_lane = lax.broadcasted_iota(jnp.int32, (rows, w), 1) // hd
        head_of_row = lax.broadcasted_iota(jnp.int32, (rows, w), 0) % nh
        qbd = jnp.where(head_of_lane == head_of_row, q_rep, 0.0).astype(BF16)
        qbd_s[...] = qbd

        lane = lax.broadcasted_iota(jnp.int32, (nh, page), 1)
        cn = jnp.concatenate([lnt_ref[...] * LOG2E, jnp.zeros((nh, page - nt), F32)], axis=1)
        sh = 1
        while sh < nt:
            cn = cn + jnp.where(lane >= sh, pltpu.roll(cn, sh, 1), 0.0)
            sh *= 2
        cn_s[...] = cn
        carry_s[...] = jnp.zeros_like(carry_s)
        m_s[...] = jnp.full(m_s.shape, NEG_INF, F32)
        l_s[...] = jnp.zeros_like(l_s)
        acc_s[...] = jnp.zeros_like(acc_s)

        kn = jnp.concatenate([kn_ref[...], jnp.zeros((page - nt, w), F32)], axis=0).astype(BF16)
        vn = jnp.concatenate([vn_ref[...], jnp.zeros((page - nt, w), F32)], axis=0).T.astype(BF16)
        s3 = _dot_nt(qbd, kn).reshape(nt, nh, page)
        cq = jnp.stack([cn[:, t:t + 1] for t in range(nt)], axis=0)
        s3 = s3 + (cq - cn[None, :, :])
        tk = lax.broadcasted_iota(jnp.int32, (nt, nh, page), 2)
        tq = lax.broadcasted_iota(jnp.int32, (nt, nh, page), 0)
        update(jnp.where(tk <= tq, s3, NEG_INF), [vn])

    lane = lax.broadcasted_iota(jnp.int32, (nh, page), 1)
    cn = cn_s[...]
    cq = jnp.stack([cn[:, t:t + 1] for t in range(nt)], axis=0)
    qbd = qbd_s[...]
    carry = carry_s[...]
    s_tiles, v_tiles = [], []
    for g in range(g_pages):
        x = f_pages[g][...] * LOG2E
        incl = x
        sh = 1
        while sh < page:
            incl = incl + jnp.where(lane + sh < page, pltpu.roll(incl, page - sh, 1), 0.0)
            sh *= 2
        r = carry + (incl - x)
        carry = carry + incl[:, 0:1]
        s3 = _dot(qbd, k_pages[g][...].reshape(w, page).astype(BF16)).reshape(nt, nh, page)
        s_tiles.append(s3 + (cq + r[None, :, :]))
        v_tiles.append(v_pages[g][...].reshape(w, page).astype(BF16))
    carry_s[...] = carry
    update(jnp.concatenate(s_tiles, axis=-1), v_tiles)

    @pl.when(step == pl.num_programs(1) - 1)
    def _():
        head_of_lane = lax.broadcasted_iota(jnp.int32, (nt, nh, w), 2) // hd
        head_of_row = lax.broadcasted_iota(jnp.int32, (nt, nh, w), 1)
        o3 = jnp.where(head_of_lane == head_of_row, acc_s[...] / l_s[...], 0.0)
        o2 = o3.reshape(rows, w).astype(BF16)
        sel_r = lax.broadcasted_iota(jnp.int32, (nt, rows), 0)
        sel_c = lax.broadcasted_iota(jnp.int32, (nt, rows), 1) // nh
        sel = jnp.where(sel_r == sel_c, 1.0, 0.0).astype(BF16)
        o_ref[...] = _dot(sel, o2).astype(o_ref.dtype)


def _fox_sample(q, kn, vn, lnt, cache_kt, cache_vt, cache_ft, page_table, layer):
    nseq, n_pages = page_table.shape
    n, w = q.shape
    nt = n // nseq
    _, _, nh, hd, page = cache_kt.shape
    g_pages = min(PAGES_PER_STEP, n_pages)
    n_steps = n_pages // g_pages

    def page_spec(shape, g):
        nd = len(shape)

        def idx(b, s, pt):
            return (layer, pt[b * n_pages + (n_pages - 1 - (s * g_pages + g))]) + (0,) * nd
        return pl.BlockSpec((None, None) + shape, idx)

    seq_rows = pl.BlockSpec((nt, w), lambda b, s, pt: (b, 0))
    in_specs = [seq_rows, seq_rows, seq_rows, pl.BlockSpec((None, nh, nt), lambda b, s, pt: (b, 0, 0))]
    in_specs += [page_spec((nh, hd, page), g) for g in range(g_pages)]
    in_specs += [page_spec((nh, hd, page), g) for g in range(g_pages)]
    in_specs += [page_spec((nh, page), g) for g in range(g_pages)]
    kern = functools.partial(_fox_sample_kernel, nh=nh, hd=hd, g_pages=g_pages)
    grid_spec = pltpu.PrefetchScalarGridSpec(
        num_scalar_prefetch=1, grid=(nseq, n_steps), in_specs=in_specs,
        out_specs=pl.BlockSpec((nt, w), lambda b, s, pt: (b, 0)),
        scratch_shapes=[pltpu.VMEM((nt * nh, w), BF16),
                        pltpu.VMEM((nt, nh, 1), F32), pltpu.VMEM((nt, nh, 1), F32),
                        pltpu.VMEM((nt, nh, w), F32),
                        pltpu.VMEM((nh, page), F32), pltpu.VMEM((nh, page), F32)])
    return pl.pallas_call(
        kern, grid_spec=grid_spec, out_shape=jax.ShapeDtypeStruct((n, w), F32),
        compiler_params=_params("parallel", "arbitrary"), name="fox_sample",
    )(page_table.reshape(-1), q, kn, vn, lnt,
      *([cache_kt] * g_pages), *([cache_vt] * g_pages), *([cache_ft] * g_pages))


def _hgrn_kernel(q_ref, g_ref, kf_ref, v_ref, gc_ref, nw_ref, s0_ref, y_ref, sn_ref, st_s, *, c, nsub, kd):
    ti = pl.program_id(1)
    nb, nh = s0_ref.shape[:2]

    @pl.when(ti == 0)
    def _():
        for sq in range(nb):
            for h in range(nh):
                st_s[sq * nh + h] = s0_ref[sq, h].T

    blk = SUBLANES
    group = HGRN_GROUP if nsub % HGRN_GROUP == 0 else 1
    rowi = lax.broadcasted_iota(jnp.int32, (blk, 1), 0)
    subi = lax.broadcasted_iota(jnp.int32, (blk, kd), 0)

    def cumsum_blocks(g):
        b_blocks = []
        off = None
        for i in range(c // blk):
            x = g[i * blk:(i + 1) * blk, :]
            sh = 1
            while sh < blk:
                x = x + jnp.where(subi >= sh, pltpu.roll(x, sh, 0), 0.0)
                sh *= 2
            x = x if off is None else x + off
            off = x[blk - 1:blk, :]
            b_blocks.append(x)
        return b_blocks

    def cat(blocks):
        return jnp.concatenate(blocks, axis=0) if len(blocks) > 1 else blocks[0]

    def body(i, carry):
        streams = []
        for sq, u, h in [(sq, u, h) for sq in range(nb) for u in range(group) for h in range(nh)]:
            rs = pl.ds(0 if nsub == 1 else pl.multiple_of((i * group + u) * c, c), c)
            hs = slice(h * kd, (h + 1) * kd)
            q, kf, v = q_ref[sq, rs, hs], kf_ref[sq, rs, hs], v_ref[sq, rs, hs]
            b_blocks = cumsum_blocks(g_ref[sq, rs, hs])
            b = cat(b_blocks)
            a_prev = [None]
            for n in range(1, len(b_blocks)):
                e = b_blocks[n - 1][blk - 1:blk, :]
                a_prev.append(_dot_nt((q[n * blk:(n + 1) * blk, :] * jnp.exp(b_blocks[n] - e)).astype(BF16),
                                      (kf[:n * blk, :] * jnp.exp(e - b[:n * blk, :])).astype(BF16)))
            streams.append(((sq, rs, hs), sq * nh + h, hs, q, kf, v, b_blocks, b, a_prev))

        diag = []
        for rs, h, hs, q, kf, v, b_blocks, b, a_prev in streams:
            o_blocks = []
            for n, bn in enumerate(b_blocks):
                r0 = n * blk
                qn = q[r0:r0 + blk, :]
                on = jnp.zeros((blk, kd), F32)
                for s in range(blk):
                    xs = qn * jnp.exp(jnp.minimum(bn - bn[s:s + 1, :], 0.0)) * kf[r0 + s:r0 + s + 1, :]
                    a = jnp.sum(xs, axis=-1, keepdims=True)
                    on = on + jnp.where(rowi >= s, a, 0.0) * v[r0 + s:r0 + s + 1, :]
                o_blocks.append(on)
            diag.append(o_blocks)

        for (rs, h, hs, q, kf, v, b_blocks, b, a_prev), o_blocks in zip(streams, diag):
            st = st_s[h]
            for n in range(1, len(b_blocks)):
                o_blocks[n] = o_blocks[n] + _dot(a_prev[n].astype(BF16), v[:n * blk, :].astype(BF16))
            o = cat(o_blocks) + _dot_nt((q * jnp.exp(b)).astype(BF16), st.astype(BF16))
            bl = b_blocks[-1][blk - 1:blk, :]
            kd_ = kf * jnp.exp(bl - b)
            st_s[h] = st * jnp.exp(bl) + _dot_tn(v.astype(BF16), kd_.astype(BF16))
            y = _rms(o, nw_ref[:, hs]) * gc_ref[rs]
            y_ref[rs] = y.astype(y_ref.dtype)
        return carry

    if nsub == 1:
        body(0, 0)
    else:
        lax.fori_loop(0, nsub // group, body, 0)

    @pl.when(ti == pl.num_programs(1) - 1)
    def _():
        for sq in range(nb):
            for h in range(nh):
                sn_ref[sq, h] = st_s[sq * nh + h].T


def _hgrn(qs, gl, kf, iv, gcs, nw, nw_layer, s0_all, layer):
    b, t, w = qs.shape
    _, _, nh, kd, vd = s0_all.shape
    c = HGRN_SUB if t % HGRN_SUB == 0 else t
    tt = HGRN_TILE if t % HGRN_TILE == 0 else t
    kern = functools.partial(_hgrn_kernel, c=c, nsub=tt // c, kd=kd)
    nb = HGRN_SEQS if (tt == t == c and b % HGRN_SEQS == 0) else 1
    seq = pl.BlockSpec((nb, tt, w), lambda i, j: (i, j, 0))
    state = pl.BlockSpec((nb, nh, kd, vd), lambda i, j: (i, 0, 0, 0))
    state_in = pl.BlockSpec((None, nb, nh, kd, vd), lambda i, j: (layer, i, 0, 0, 0))
    return pl.pallas_call(
        kern, grid=(b // nb, t // tt),
        in_specs=[seq, seq, seq, seq, seq, _layer_spec(nw, nw_layer), state_in],
        out_specs=(seq, state),
        out_shape=(jax.ShapeDtypeStruct((b, t, nh * vd), BF16), jax.ShapeDtypeStruct((b, nh, kd, vd), F32)),
        scratch_shapes=[pltpu.VMEM((nb * nh, vd, kd), F32)],
        compiler_params=_params("parallel", "arbitrary"), name="hgrn",
    )(qs, gl, kf, iv, gcs, nw, s0_all)


def _merge_kernel(x_ref, ya_ref, ub_ref, vb_ref, yc_ref, gt_ref, gm_ref, gb_ref,
                  woa_ref, wob_ref, woc_ref, wo_ref, o_ref, *, n_groups):
    tm = x_ref.shape[0]
    d = x_ref.shape[1]
    cn = gm_ref.shape[1]
    gw = ub_ref.shape[1] // n_groups
    yb_rows = []
    for ci in range(tm // cn):
        rs = slice(ci * cn, (ci + 1) * cn)
        vb = vb_ref[rs, :].astype(BF16)
        gb = gb_ref[...]
        parts = []
        for g in range(n_groups):
            parts.append(_dot(gm_ref[g], vb[:, g * gw:(g + 1) * gw]) + gb[:, g:g + 1])
        yb_rows.append(ub_ref[rs, :] * jnp.concatenate(parts, axis=1))
    yb = jnp.concatenate(yb_rows, axis=0) if len(yb_rows) > 1 else yb_rows[0]
    gt = gt_ref[...]
    merged = (gt[:, :d] * _dot(ya_ref[...].astype(BF16), woa_ref[...])
              + gt[:, d:2 * d] * _dot(yb.astype(BF16), wob_ref[...])
              + gt[:, 2 * d:] * _dot(yc_ref[...].astype(BF16), woc_ref[...]))
    o_ref[...] = x_ref[...] + _dot(merged.astype(BF16), wo_ref[...])


def _merge(x, ya, ub, vb, yc, gt, layer, gm, gb, woa, wob, woc, wo):
    n, d = x.shape
    tm = min(MERGE_TILE, n)

    def rows(w):
        return pl.BlockSpec((tm, w), lambda i: (i, 0))

    kern = functools.partial(_merge_kernel, n_groups=gm.shape[1])
    return pl.pallas_call(
        kern, grid=(n // tm,),
        in_specs=[rows(d), rows(ya.shape[1]), rows(ub.shape[1]), rows(vb.shape[1]), rows(yc.shape[1]),
                  rows(gt.shape[1])] + [_layer_spec(p, layer) for p in (gm, gb, woa, wob, woc, wo)],
        out_specs=rows(d), out_shape=jax.ShapeDtypeStruct((n, d), F32),
        compiler_params=_params("parallel"), name="merge",
    )(x, ya, ub, vb, yc, gt, gm, gb, woa, wob, woc, wo)


def _ffn_kernel(x_ref, n2_ref, wup_ref, wcv_ref, bcv_ref, wdn_ref, buf_ref, nf_ref,
                y_ref, nb_ref, carry_s, ext_s, h_s, act_s, *, shift, base, cw, final_norm):
    ti = pl.program_id(1)
    tm = x_ref.shape[1]
    d_ff = wdn_ref.shape[0]
    hist = 2 * shift

    @pl.when(ti == 0)
    def _():
        carry_s[...] = buf_ref[0]

    n_chunks = d_ff // cw
    n_slots = ext_s.shape[0]

    def col0(c, half):
        return half * d_ff + c * cw

    def stage(c):
        for half in range(2):
            cols = slice(col0(c, half), col0(c, half) + cw)
            ext = ext_s.at[(2 * c + half) % n_slots]
            ext[base - hist:base, :] = carry_s[:, cols]
            ext[base:base + tm, :] = _dot(h_s[...], wup_ref[:, cols])

    def conv(c, half):
        cols = slice(col0(c, half), col0(c, half) + cw)
        ext = ext_s.at[(2 * c + half) % n_slots]
        wcv = wcv_ref[:, cols]
        out = bcv_ref[:, cols] + (wcv[0:1, :] * ext[base - hist:base - hist + tm, :]
                                  + wcv[1:2, :] * ext[base - shift:base - shift + tm, :]
                                  + wcv[2:3, :] * ext[base:base + tm, :])
        carry_s[:, cols] = ext[base + tm - hist:base + tm, :]
        return out

    x = x_ref[0]
    h_s[...] = _rms(x, n2_ref[...]).astype(BF16)
    y = x
    stage(0)
    for c in range(n_chunks):
        if c + 1 < n_chunks:
            stage(c + 1)
        act_s[:, c * cw:(c + 1) * cw] = (_silu(conv(c, 0)) * conv(c, 1)).astype(BF16)
        if (c + 1) % FFN_DOWN_CHUNKS == 0 or c + 1 == n_chunks:
            r0 = (c // FFN_DOWN_CHUNKS) * FFN_DOWN_CHUNKS * cw
            y = y + _dot(act_s[:, r0:(c + 1) * cw], wdn_ref[r0:(c + 1) * cw, :])
    if final_norm:
        y = _rms(y, nf_ref[...])
    y_ref[0] = y
    nb_ref[0] = carry_s[...]


def _ffn(x, layer, n2, wup, wcv, bcv, wdn, buf, nf, shift, final_norm):
    nseq, t, d = x.shape
    tm = min(FFN_TILE, t)
    hist = 2 * shift
    base = -(-hist // SUBLANES) * SUBLANES
    kern = functools.partial(_ffn_kernel, shift=shift, base=base, cw=FFN_COLS, final_norm=final_norm)
    state = pl.BlockSpec((1, hist, buf.shape[2]), lambda i, j: (i, 0, 0))
    return pl.pallas_call(
        kern, grid=(nseq, t // tm),
        in_specs=[pl.BlockSpec((1, tm, d), lambda i, j: (i, j, 0)), _layer_spec(n2, layer),
                  _layer_spec(wup, layer), _layer_spec(wcv, layer), _layer_spec(bcv, layer),
                  _layer_spec(wdn, layer), state, _const_spec(nf.shape)],
        out_specs=(pl.BlockSpec((1, tm, d), lambda i, j: (i, j, 0)), state),
        out_shape=(jax.ShapeDtypeStruct(x.shape, F32), jax.ShapeDtypeStruct(buf.shape, F32)),
        scratch_shapes=[pltpu.VMEM((hist, buf.shape[2]), F32),
                        pltpu.VMEM((FFN_EXT_SLOTS, base + tm, FFN_COLS), F32),
                        pltpu.VMEM((tm, d), BF16), pltpu.VMEM((tm, wdn.shape[1]), BF16)],
        compiler_params=_params("parallel", "arbitrary"), name="conv_ffn",
    )(x, n2, wup, wcv, bcv, wdn, buf, nf)


def kernel(x_prompt, x_sample, cache_k, cache_v, cache_logf, state_hgrn, state_conv, page_table,
           norm1, w_in, b_f, gmlp_norm, w_s, b_s, hgrn_norm, hgrn_lb, w_oa, w_ob, w_oc, w_o,
           norm2, w_up, w_conv, b_conv, w_down, norm_f):
    depth = w_in.shape[0]
    bp, tp, d = x_prompt.shape
    bs, ts, _ = x_sample.shape
    _, n_phys, page, nh_a, hd_a = cache_k.shape
    w_a = nh_a * hd_a
    n_grp, chunk_b, _ = w_s.shape[1:]
    w_b = gmlp_norm.shape[1]
    _, _, nh_c, k_c, v_c = state_hgrn.shape
    w_c = nh_c * v_c
    d_ff = w_down.shape[1]

    sm = jax.nn.softmax(hgrn_lb.astype(F32), axis=0)
    lower = jnp.cumsum(sm, axis=0) - sm[:1]

    cache_kt = jnp.transpose(cache_k, (0, 1, 3, 4, 2))
    cache_vt = jnp.transpose(cache_v, (0, 1, 3, 4, 2))
    cache_ft = jnp.swapaxes(cache_logf, 2, 3)

    w_t = jnp.swapaxes(w_in, 1, 2).astype(BF16)
    o_f = 3 * w_a
    inproj_w = (norm1.reshape(depth, 1, d), w_t[:, :o_f],
                jnp.pad(w_t[:, o_f:o_f + nh_a], ((0, 0), (0, LANES - nh_a), (0, 0))), w_t[:, o_f + nh_a:],
                jnp.pad(b_f, ((0, 0), (0, LANES - nh_a))).reshape(depth, 1, LANES), b_f.reshape(depth, nh_a, 1),
                gmlp_norm.reshape(depth, 1, w_b), lower.reshape(depth, 1, w_c))
    qscale = hd_a ** -0.5 * LOG2E
    nw = hgrn_norm.reshape(depth, 1, w_c)
    tril_w = jnp.where(jnp.tril(jnp.ones((chunk_b, chunk_b), bool)), w_s, 0)
    merge_w = (w_oa.astype(BF16), w_ob.astype(BF16), w_oc.astype(BF16), w_o.astype(BF16))
    gate_p = (tril_w.astype(BF16), jnp.swapaxes(b_s, 1, 2))
    same_seq = np.kron(np.eye(bs, dtype=np.float32), np.ones((ts, ts), np.float32))
    rep = np.tile(np.eye(ts, dtype=np.float32), (bs, 1))
    gate_s = ((jnp.einsum("rt,lgts,cs->lgrc", rep, tril_w[:, :, :ts, :ts], rep) * same_seq).astype(BF16),
              jnp.tile(jnp.swapaxes(b_s[:, :, :ts], 1, 2), (1, bs, 1)))
    ffn_w = (norm2.reshape(depth, 1, d), w_up.astype(BF16), w_conv, b_conv.reshape(depth, 1, 2 * d_ff),
             w_down.astype(BF16))
    nf = norm_f.reshape(1, d)
    s0 = jnp.zeros((1, bp, nh_c, k_c, v_c), F32)
    buf0 = jnp.zeros((bp, 2, 2 * d_ff), F32)

    xp = x_prompt.reshape(bp * tp, d)
    xs = x_sample.reshape(bs * ts, d)
    st_p, st_s = [], []
    kv_bufs = None
    for l in range(depth):
        last = l == depth - 1

        (qa, ka, kt_all, vt_all, vt, lft_all, ub, vb, qs, gl, kf, iv, gcs, gt) = _inproj(
            xp, l, qscale, *inproj_w, stacked=(depth, bp, kv_bufs))
        kv_bufs = (kt_all, vt_all, lft_all)
        ya = _fox_prompt(qa.reshape(bp, tp, 2 * w_a), ka.reshape(bp, tp, 2 * w_a), vt, nh_a)
        yc, s_new = _hgrn(*(a.reshape(bp, tp, w_c) for a in (qs, gl, kf, iv, gcs)), nw, l, s0, 0)
        x1 = _merge(xp, ya.reshape(bp * tp, w_a), ub, vb, yc.reshape(bp * tp, w_c), gt, l, *gate_p, *merge_w)
        x2, conv_p = _ffn(x1.reshape(bp, tp, d), l, *ffn_w, buf0, nf, 1, last)
        xp = x2.reshape(bp * tp, d)
        st_p.append((s_new, conv_p))

        (q, k, v, lf, ub, vb, qs, gl, kf, iv, gcs, gt) = _inproj(xs, l, qscale, *inproj_w)
        k4 = k.reshape(bs, ts, nh_a, hd_a)
        v4 = v.reshape(bs, ts, nh_a, hd_a)
        lf3 = lf[:, :nh_a].reshape(bs, ts, nh_a)
        ya = _fox_sample(q.astype(F32), k, v, jnp.swapaxes(lf3, 1, 2),
                         cache_kt, cache_vt, cache_ft, page_table, l)
        yc, s_new = _hgrn(*(a.reshape(bs, ts, w_c) for a in (qs, gl, kf, iv, gcs)), nw, l, state_hgrn, l)
        x1 = _merge(xs, ya, ub, vb, yc.reshape(bs * ts, w_c), gt, l, *gate_s, *merge_w)
        x1t = jnp.swapaxes(x1.reshape(bs, ts, d), 0, 1).reshape(1, ts * bs, d)
        buf_t = jnp.swapaxes(state_conv[l], 0, 1).reshape(1, 2 * bs, 2 * d_ff)
        x2t, nb = _ffn(x1t, l, *ffn_w, buf_t, nf, bs, last)
        xs = jnp.swapaxes(x2t.reshape(ts, bs, d), 0, 1).reshape(bs * ts, d)
        conv_s = nb.reshape(2, bs, 2 * d_ff)
        st_s.append((k4, v4, lf3, vb.reshape(bs, ts, w_b), s_new, jnp.swapaxes(conv_s, 0, 1)))

    def stk(sts, i):
        return jnp.stack([st[i] for st in sts], axis=0)

    kt_all, vt_all, lft_all = kv_bufs
    k_prompt = jnp.transpose(kt_all.reshape(depth, bp, nh_a, hd_a, tp), (0, 1, 4, 2, 3))
    v_prompt = jnp.transpose(vt_all.reshape(depth, bp, nh_a, hd_a, tp), (0, 1, 4, 2, 3))
    return (xp.reshape(bp, tp, d), xs.reshape(bs, ts, d),
            k_prompt, v_prompt, jnp.swapaxes(lft_all, 2, 3), stk(st_p, 0), stk(st_p, 1),
            stk(st_s, 0), stk(st_s, 1), stk(st_s, 2), stk(st_s, 3), stk(st_s, 4), stk(st_s, 5))
```

```python
import functools

import numpy as np

import jax
import jax.numpy as jnp
from jax import lax
from jax.experimental import pallas as pl
from jax.experimental.pallas import tpu as pltpu

F32 = jnp.float32
BF16 = jnp.bfloat16
EPS = 1e-6
NEG_INF = float("-inf")
LOG2E = 1.4426950408889634

LANES = 128
SUBLANES = 8
VMEM_LIMIT_BYTES = 56 * 1024 * 1024

ROW_TILE = 256
MERGE_TILE = 512
FFN_TILE = 256
PAGES_PER_STEP = 32
HGRN_SUB = 16
HGRN_TILE = 512
HGRN_GROUP = 8
HGRN_SEQS = 4
FFN_COLS = 256
FFN_EXT_SLOTS = 4
FFN_DOWN_CHUNKS = 11
N_SPLIT = 3


def _dot(a, b):
    return jnp.dot(a, b, preferred_element_type=F32)


def _dot_nt(a, b):
    return lax.dot_general(a, b, (((1,), (1,)), ((), ())), preferred_element_type=F32)


def _dot_tn(a, b):
    return lax.dot_general(a, b, (((0,), (0,)), ((), ())), preferred_element_type=F32)


def _rms(x, g):
    return x * lax.rsqrt(jnp.mean(x * x, axis=-1, keepdims=True) + EPS) * g


def _sigmoid(x):
    return 1.0 / (1.0 + jnp.exp(-x))


def _silu(x):
    return x * _sigmoid(x)


def _gelu_tanh(x):
    c = 0.7978845608028654
    return x * (0.5 * (1.0 + jnp.tanh(c * (x + 0.044715 * (x * x * x)))))


def _log_sigmoid(z):
    return jnp.minimum(z, 0.0) - jnp.log1p(jnp.exp(-jnp.abs(z)))


def _split3(x):
    hi = x.astype(BF16)
    r = x - hi.astype(F32)
    mid = r.astype(BF16)
    lo = (r - mid.astype(F32)).astype(BF16)
    return hi, mid, lo


def _ones_tri(n, upper):
    r = lax.broadcasted_iota(jnp.int32, (n, n), 0)
    c = lax.broadcasted_iota(jnp.int32, (n, n), 1)
    keep = (r <= c) if upper else (r >= c)
    return jnp.where(keep, 1.0, 0.0).astype(BF16)


def _const_spec(shape):
    nd = len(shape)
    return pl.BlockSpec(shape, lambda *_: (0,) * nd, pipeline_mode=pl.Buffered(1))


def _layer_spec(arr, layer):
    nd = arr.ndim - 1
    return pl.BlockSpec((None,) + arr.shape[1:], lambda *_: (layer,) + (0,) * nd, pipeline_mode=pl.Buffered(1))


def _params(*sem):
    return pltpu.CompilerParams(dimension_semantics=sem, vmem_limit_bytes=VMEM_LIMIT_BYTES)


def _inproj_kernel(x_ref, n1_ref, wa_ref, wf_ref, wr_ref, bf_ref, bfc_ref, gn_ref, lb_ref,
                   *rest, n_alias, seq_major, qscale, tps):
    if seq_major:
        selq_ref, selk_ref, oneq_ref, onek_ref = rest[:4]
        outs, carry_s = rest[4 + n_alias:-1], rest[-1]
        qa_ref, ka_ref, kt_ref, vt_ref, vtb_ref, lft_ref = outs[:6]
        ub_ref, vb_ref, qs_ref, gl_ref, kf_ref, iv_ref, gc_ref, gt_ref = outs[6:]
        wa_w = kt_ref.shape[0]
    else:
        q_ref, k_ref, v_ref, lf_ref = rest[:4]
        ub_ref, vb_ref, qs_ref, gl_ref, kf_ref, iv_ref, gc_ref, gt_ref = rest[4:]
        wa_w = k_ref.shape[-1]
    wb_w = ub_ref.shape[-1]
    wc_w = qs_ref.shape[-1]
    h = _rms(x_ref[...], n1_ref[...]).astype(BF16)
    lf = _log_sigmoid(_dot_nt(h, wf_ref[...]) + bf_ref[...])

    if seq_major:
        tm = x_ref.shape[0]

        @pl.when(pl.program_id(0) % tps == 0)
        def _():
            carry_s[...] = jnp.zeros_like(carry_s)

        lower = _ones_tri(tm, upper=False)
        hi, mid, lo = _split3(lf)
        c = _dot(lower, hi) + _dot(lower, mid) + _dot(lower, lo) + carry_s[...]
        carry_s[...] = c[tm - 1:tm, :]
        c3 = jnp.concatenate(_split3(c * LOG2E), axis=1)
        eq = _dot(c3, selq_ref[...]) + oneq_ref[...]
        ek = _dot(c3, selk_ref[...]) + onek_ref[...]

        pa = _dot_nt(h, wa_ref[:2 * wa_w, :])
        kv_t = _dot_nt(wa_ref[wa_w:, :], h)
        kt_ref[...] = kv_t[:wa_w, :]
        vt_ref[...] = kv_t[wa_w:, :]
        vtb_ref[...] = kv_t[wa_w:, :].astype(BF16)
        lft_ref[...] = _log_sigmoid(_dot_nt(wf_ref[...], h)[:bfc_ref.shape[0], :] + bfc_ref[...])
        q = pa[:, :wa_w] * qscale
        k = pa[:, wa_w:]
        qa_parts, ka_parts = [], []
        for p in range(wa_w // LANES):
            ls = slice(p * LANES, (p + 1) * LANES)
            qa_parts += [q[:, ls], eq[:, ls]]
            ka_parts += [k[:, ls], ek[:, ls]]
        qa_ref[...] = jnp.concatenate(qa_parts, axis=1).astype(BF16)
        ka_ref[...] = jnp.concatenate(ka_parts, axis=1).astype(BF16)
    else:
        pa = _dot_nt(h, wa_ref[...])
        q_ref[...] = (pa[:, :wa_w] * qscale).astype(q_ref.dtype)
        k_ref[...] = pa[:, wa_w:2 * wa_w]
        v_ref[...] = pa[:, 2 * wa_w:]
        lf_ref[...] = lf

    r_b, r_c = 2 * wb_w, 2 * wb_w + 4 * wc_w
    pb = _dot_nt(h, wr_ref[:r_b, :])
    ub_ref[...] = _gelu_tanh(pb[:, :wb_w])
    vb_ref[...] = _rms(_gelu_tanh(pb[:, wb_w:]), gn_ref[...])

    pc = _dot_nt(h, wr_ref[r_b:r_c, :])
    lb = lb_ref[...]
    qs_ref[...] = _silu(pc[:, :wc_w])
    f = lb + (1.0 - lb) * _sigmoid(pc[:, wc_w:2 * wc_w])
    gl_ref[...] = jnp.log(f)
    kf_ref[...] = 1.0 - f
    iv_ref[...] = pc[:, 2 * wc_w:3 * wc_w]
    gc_ref[...] = _silu(pc[:, 3 * wc_w:])

    gt_ref[...] = _sigmoid(_dot_nt(h, wr_ref[r_c:, :]))


def _inproj(x, layer, qscale, n1, wa, wf, wr, bf_row, bf_col, gn, lb, stacked=None):
    n, d = x.shape
    tm = min(ROW_TILE, n)
    wa_w, wb_w, wc_w = wa.shape[1] // 3, gn.shape[-1], lb.shape[-1]
    wg_w = wr.shape[1] - 2 * wb_w - 4 * wc_w
    nh = bf_col.shape[1]

    def rows(w):
        return pl.BlockSpec((tm, w), lambda i: (i, 0))

    extra_in, extra_specs, aliases, scratch, tps = [], [], {}, [], 1
    if stacked is None:
        head_shape = (jax.ShapeDtypeStruct((n, wa_w), BF16),
                      jax.ShapeDtypeStruct((n, wa_w), F32), jax.ShapeDtypeStruct((n, wa_w), F32),
                      jax.ShapeDtypeStruct((n, LANES), F32))
        head_specs = (rows(wa_w), rows(wa_w), rows(wa_w), rows(LANES))
    else:
        depth, b, prev = stacked
        t = n // b
        tps = t // tm

        def feat_major(f):
            return (jax.ShapeDtypeStruct((depth, b, f, t), F32),
                    pl.BlockSpec((None, None, f, tm), lambda i: (layer, i // tps, 0, i % tps)))

        (kt_shape, kt_spec), (lft_shape, lft_spec) = feat_major(wa_w), feat_major(nh)
        aug_shape = jax.ShapeDtypeStruct((n, 2 * wa_w), BF16)
        head_shape = (aug_shape, aug_shape, kt_shape, kt_shape,
                      jax.ShapeDtypeStruct((b, tps, wa_w, tm), BF16), lft_shape)
        head_specs = (rows(2 * wa_w), rows(2 * wa_w), kt_spec, kt_spec,
                      pl.BlockSpec((None, None, wa_w, tm), lambda i: (i // tps, i % tps, 0, 0)), lft_spec)
        sel = _decay_selectors(nh, wa_w // nh)
        extra_in = list(sel)
        extra_specs = [_const_spec(s.shape) for s in sel]
        scratch = [pltpu.VMEM((1, LANES), F32)]
        if prev is not None:
            extra_in += list(prev)
            extra_specs += [pl.BlockSpec(memory_space=pl.ANY)] * len(prev)
            aliases = {13: 2, 14: 3, 15: 5}

    out_shape = head_shape + (
        jax.ShapeDtypeStruct((n, wb_w), F32),
        jax.ShapeDtypeStruct((n, wb_w), F32),
        jax.ShapeDtypeStruct((n, wc_w), F32),
        jax.ShapeDtypeStruct((n, wc_w), F32),
        jax.ShapeDtypeStruct((n, wc_w), F32),
        jax.ShapeDtypeStruct((n, wc_w), F32),
        jax.ShapeDtypeStruct((n, wc_w), F32),
        jax.ShapeDtypeStruct((n, wg_w), F32),
    )
    out_specs = head_specs + (rows(wb_w), rows(wb_w), rows(wc_w), rows(wc_w), rows(wc_w), rows(wc_w),
                              rows(wc_w), rows(wg_w))
    params = (n1, wa, wf, wr, bf_row, bf_col, gn, lb)
    in_specs = [rows(d)] + [_layer_spec(p, layer) for p in params] + extra_specs
    kern = functools.partial(_inproj_kernel, n_alias=len(aliases), seq_major=stacked is not None,
                             qscale=qscale, tps=tps)
    return pl.pallas_call(
        kern, grid=(n // tm,), in_specs=in_specs, out_specs=out_specs, out_shape=out_shape,
        scratch_shapes=scratch, input_output_aliases=aliases,
        compiler_params=_params("parallel" if stacked is None else "arbitrary"), name="inproj",
    )(x, *params, *extra_in)


def _decay_selectors(nh, hd):
    sel_q = np.zeros((N_SPLIT * LANES, nh * hd), np.float32)
    sel_k = np.zeros((N_SPLIT * LANES, nh * hd), np.float32)
    one_q = np.zeros((1, nh * hd), np.float32)
    one_k = np.zeros((1, nh * hd), np.float32)
    for h in range(nh):
        for p in range(N_SPLIT):
            sel_k[p * LANES + h, h * hd + p] = -1.0
            one_k[0, h * hd + N_SPLIT + p] = 1.0
            one_q[0, h * hd + p] = 1.0
            sel_q[p * LANES + h, h * hd + N_SPLIT + p] = 1.0
    return (jnp.asarray(sel_q, BF16), jnp.asarray(sel_k, BF16), jnp.asarray(one_q), jnp.asarray(one_k))


def _fox_prompt_kernel(qa_ref, ka_ref, vt_ref, o_ref, qh_s, m_s, l_s, acc_s, *, hd, tq):
    qi = pl.program_id(1)
    heads_per_slab = LANES // hd
    n_slab = vt_ref.shape[2] // LANES
    heads = [(sl, hh) for sl in range(n_slab) for hh in range(heads_per_slab)]
    lane2 = lax.broadcasted_iota(jnp.int32, (tq, 2 * LANES), 1) % LANES
    key = lax.broadcasted_iota(jnp.int32, (tq, tq), 0)
    qry = lax.broadcasted_iota(jnp.int32, (tq, tq), 1)

    for n, (sl, hh) in enumerate(heads):
        qa = qa_ref[0, :, sl * 2 * LANES:(sl + 1) * 2 * LANES].astype(F32)
        qh_s[n] = jnp.where(lane2 // hd == hh, qa, 0.0).astype(BF16)
        m_s[n] = jnp.full((1, tq), NEG_INF, F32)
        l_s[n] = jnp.zeros((1, tq), F32)
        acc_s[n] = jnp.zeros((LANES, tq), F32)

    def tiles(js):
        s_all = []
        for n, (sl, _) in enumerate(heads):
            s_n = []
            for j, masked in js:
                k0 = pl.multiple_of(j * tq, tq)
                s = _dot_nt(ka_ref[0, pl.ds(k0, tq), sl * 2 * LANES:(sl + 1) * 2 * LANES], qh_s[n])
                s_n.append(jnp.where(key <= qry, s, NEG_INF) if masked else s)
            s_all.append(s_n)
        m_old = [m_s[n] for n in range(len(heads))]
        m_new = []
        for m, s_n in zip(m_old, s_all):
            for s in s_n:
                m = jnp.maximum(m, jnp.max(s, axis=0, keepdims=True))
            m_new.append(m)
        p_all = [[jnp.exp2(s - m) for s in s_n] for s_n, m in zip(s_all, m_new)]
        for n, (sl, _) in enumerate(heads):
            alpha = jnp.exp2(m_old[n] - m_new[n])
            l_new = alpha * l_s[n]
            acc = alpha * acc_s[n]
            for (j, _), p in zip(js, p_all[n]):
                l_new = l_new + jnp.sum(p, axis=0, keepdims=True)
                acc = acc + _dot(vt_ref[0, j, sl * LANES:(sl + 1) * LANES, :], p.astype(BF16))
            l_s[n] = l_new
            acc_s[n] = acc
            m_s[n] = m_new[n]

    def body(jj, carry):
        tiles([(2 * jj, False), (2 * jj + 1, False)])
        return carry

    lax.fori_loop(0, qi // 2, body, 0)

    @pl.when(qi % 2 == 1)
    def _():
        tiles([(qi - 1, False), (qi, True)])

    @pl.when(qi % 2 == 0)
    def _():
        tiles([(qi, True)])

    for sl in range(n_slab):
        parts = [(acc_s[n] / l_s[n])[hh * hd:(hh + 1) * hd, :] for n, (s2, hh) in enumerate(heads) if s2 == sl]
        o_ref[0, :, sl * LANES:(sl + 1) * LANES] = jnp.concatenate(parts, axis=0).T.astype(o_ref.dtype)


def _fox_prompt(qa, ka, vt, nh):
    b, nkv, w, tq = vt.shape
    t = nkv * tq
    hd = w // nh
    kern = functools.partial(_fox_prompt_kernel, hd=hd, tq=tq)
    return pl.pallas_call(
        kern, grid=(b, nkv),
        in_specs=[pl.BlockSpec((1, tq, 2 * w), lambda i, j: (i, j, 0)),
                  pl.BlockSpec((1, t, 2 * w), lambda i, j: (i, 0, 0)),
                  pl.BlockSpec((1, nkv, w, tq), lambda i, j: (i, 0, 0, 0))],
        out_specs=pl.BlockSpec((1, tq, w), lambda i, j: (i, j, 0)),
        out_shape=jax.ShapeDtypeStruct((b, t, w), BF16),
        scratch_shapes=[pltpu.VMEM((nh, tq, 2 * LANES), BF16), pltpu.VMEM((nh, 1, tq), F32),
                        pltpu.VMEM((nh, 1, tq), F32), pltpu.VMEM((nh, LANES, tq), F32)],
        compiler_params=_params("parallel", "arbitrary"), name="fox_prompt",
    )(qa, ka, vt)


def _fox_sample_kernel(pt_ref, q_ref, kn_ref, vn_ref, lnt_ref, *rest, nh, hd, g_pages):
    k_pages = rest[:g_pages]
    v_pages = rest[g_pages:2 * g_pages]
    f_pages = rest[2 * g_pages:3 * g_pages]
    o_ref, qbd_s, m_s, l_s, acc_s, carry_s, cn_s = rest[3 * g_pages:]
    del pt_ref
    step = pl.program_id(1)
    nt = q_ref.shape[0]
    w = q_ref.shape[1]
    page = k_pages[0].shape[-1]
    rows = nt * nh

    def update(s3, vs):
        m_old = m_s[...]
        m_new = jnp.maximum(m_old, jnp.max(s3, axis=-1, keepdims=True))
        alpha = jnp.exp2(m_old - m_new)
        p3 = jnp.exp2(s3 - m_new)
        l_s[...] = alpha * l_s[...] + jnp.sum(p3, axis=-1, keepdims=True)
        p = p3.reshape(rows, p3.shape[-1]).astype(BF16)
        pv = None
        off = 0
        for vt in vs:
            part = _dot_nt(p[:, off:off + vt.shape[1]], vt)
            pv = part if pv is None else pv + part
            off += vt.shape[1]
        acc_s[...] = alpha * acc_s[...] + pv.reshape(nt, nh, w)
        m_s[...] = m_new

    @pl.when(step == 0)
    def _():
        rep_r = lax.broadcasted_iota(jnp.int32, (rows, nt), 0) // nh
        rep_c = lax.broadcasted_iota(jnp.int32, (rows, nt), 1)
        q_rep = _dot(jnp.where(rep_r == rep_c, 1.0, 0.0), q_ref[...])
        head_of_lane = lax.broadcasted_iota(jnp.int32, (rows, w), 1) // hd
        head_of_row = lax.broadcasted_iota(jnp.int32, (rows, w), 0) % nh
        qbd = jnp.where(head_of_lane == head_of_row, q_rep, 0.0).astype(BF16)
        qbd_s[...] = qbd

        lane = lax.broadcasted_iota(jnp.int32, (nh, page), 1)
        cn = jnp.concatenate([lnt_ref[...] * LOG2E, jnp.zeros((nh, page - nt), F32)], axis=1)
        sh = 1
        while sh < nt:
            cn = cn + jnp.where(lane >= sh, pltpu.roll(cn, sh, 1), 0.0)
            sh *= 2
        cn_s[...] = cn
        carry_s[...] = jnp.zeros_like(carry_s)
        m_s[...] = jnp.full(m_s.shape, NEG_INF, F32)
        l_s[...] = jnp.zeros_like(l_s)
        acc_s[...] = jnp.zeros_like(acc_s)

        kn = jnp.concatenate([kn_ref[...], jnp.zeros((page - nt, w), F32)], axis=0).astype(BF16)
        vn = jnp.concatenate([vn_ref[...], jnp.zeros((page - nt, w), F32)], axis=0).T.astype(BF16)
        s3 = _dot_nt(qbd, kn).reshape(nt, nh, page)
        cq = jnp.stack([cn[:, t:t + 1] for t in range(nt)], axis=0)
        s3 = s3 + (cq - cn[None, :, :])
        tk = lax.broadcasted_iota(jnp.int32, (nt, nh, page), 2)
        tq = lax.broadcasted_iota(jnp.int32, (nt, nh, page), 0)
        update(jnp.where(tk <= tq, s3, NEG_INF), [vn])

    lane = lax.broadcasted_iota(jnp.int32, (nh, page), 1)
    cn = cn_s[...]
    cq = jnp.stack([cn[:, t:t + 1] for t in range(nt)], axis=0)
    qbd = qbd_s[...]
    carry = carry_s[...]
    s_tiles, v_tiles = [], []
    for g in range(g_pages):
        x = f_pages[g][...] * LOG2E
        incl = x
        sh = 1
        while sh < page:
            incl = incl + jnp.where(lane + sh < page, pltpu.roll(incl, page - sh, 1), 0.0)
            sh *= 2
        r = carry + (incl - x)
        carry = carry + incl[:, 0:1]
        s3 = _dot(qbd, k_pages[g][...].reshape(w, page).astype(BF16)).reshape(nt, nh, page)
        s_tiles.append(s3 + (cq + r[None, :, :]))
        v_tiles.append(v_pages[g][...].reshape(w, page).astype(BF16))
    carry_s[...] = carry
    update(jnp.concatenate(s_tiles, axis=-1), v_tiles)

    @pl.when(step == pl.num_programs(1) - 1)
    def _():
        head_of_lane = lax.broadcasted_iota(jnp.int32, (nt, nh, w), 2) // hd
        head_of_row = lax.broadcasted_iota(jnp.int32, (nt, nh, w), 1)
        o3 = jnp.where(head_of_lane == head_of_row, acc_s[...] / l_s[...], 0.0)
        o2 = o3.reshape(rows, w).astype(BF16)
        sel_r = lax.broadcasted_iota(jnp.int32, (nt, rows), 0)
        sel_c = lax.broadcasted_iota(jnp.int32, (nt, rows), 1) // nh
        sel = jnp.where(sel_r == sel_c, 1.0, 0.0).astype(BF16)
        o_ref[...] = _dot(sel, o2).astype(o_ref.dtype)


def _fox_sample(q, kn, vn, lnt, cache_kt, cache_vt, cache_ft, page_table, layer):
    nseq, n_pages = page_table.shape
    n, w = q.shape
    nt = n // nseq
    _, _, nh, hd, page = cache_kt.shape
    g_pages = min(PAGES_PER_STEP, n_pages)
    n_steps = n_pages // g_pages

    def page_spec(shape, g):
        nd = len(shape)

        def idx(b, s, pt):
            return (layer, pt[b * n_pages + (n_pages - 1 - (s * g_pages + g))]) + (0,) * nd
        return pl.BlockSpec((None, None) + shape, idx)

    seq_rows = pl.BlockSpec((nt, w), lambda b, s, pt: (b, 0))
    in_specs = [seq_rows, seq_rows, seq_rows, pl.BlockSpec((None, nh, nt), lambda b, s, pt: (b, 0, 0))]
    in_specs += [page_spec((nh, hd, page), g) for g in range(g_pages)]
    in_specs += [page_spec((nh, hd, page), g) for g in range(g_pages)]
    in_specs += [page_spec((nh, page), g) for g in range(g_pages)]
    kern = functools.partial(_fox_sample_kernel, nh=nh, hd=hd, g_pages=g_pages)
    grid_spec = pltpu.PrefetchScalarGridSpec(
        num_scalar_prefetch=1, grid=(nseq, n_steps), in_specs=in_specs,
        out_specs=pl.BlockSpec((nt, w), lambda b, s, pt: (b, 0)),
        scratch_shapes=[pltpu.VMEM((nt * nh, w), BF16),
                        pltpu.VMEM((nt, nh, 1), F32), pltpu.VMEM((nt, nh, 1), F32),
                        pltpu.VMEM((nt, nh, w), F32),
                        pltpu.VMEM((nh, page), F32), pltpu.VMEM((nh, page), F32)])
    return pl.pallas_call(
        kern, grid_spec=grid_spec, out_shape=jax.ShapeDtypeStruct((n, w), F32),
        compiler_params=_params("parallel", "arbitrary"), name="fox_sample",
    )(page_table.reshape(-1), q, kn, vn, lnt,
      *([cache_kt] * g_pages), *([cache_vt] * g_pages), *([cache_ft] * g_pages))


def _hgrn_kernel(q_ref, g_ref, kf_ref, v_ref, gc_ref, nw_ref, s0_ref, y_ref, sn_ref, st_s, *, c, nsub, kd):
    ti = pl.program_id(1)
    nb, nh = s0_ref.shape[:2]

    @pl.when(ti == 0)
    def _():
        for sq in range(nb):
            for h in range(nh):
                st_s[sq * nh + h] = s0_ref[sq, h].T

    blk = SUBLANES
    group = HGRN_GROUP if nsub % HGRN_GROUP == 0 else 1
    rowi = lax.broadcasted_iota(jnp.int32, (blk, 1), 0)
    subi = lax.broadcasted_iota(jnp.int32, (blk, kd), 0)

    def cumsum_blocks(g):
        b_blocks = []
        off = None
        for i in range(c // blk):
            x = g[i * blk:(i + 1) * blk, :]
            sh = 1
            while sh < blk:
                x = x + jnp.where(subi >= sh, pltpu.roll(x, sh, 0), 0.0)
                sh *= 2
            x = x if off is None else x + off
            off = x[blk - 1:blk, :]
            b_blocks.append(x)
        return b_blocks

    def cat(blocks):
        return jnp.concatenate(blocks, axis=0) if len(blocks) > 1 else blocks[0]

    def body(i, carry):
        streams = []
        for sq, u, h in [(sq, u, h) for sq in range(nb) for u in range(group) for h in range(nh)]:
            rs = pl.ds(0 if nsub == 1 else pl.multiple_of((i * group + u) * c, c), c)
            hs = slice(h * kd, (h + 1) * kd)
            q, kf, v = q_ref[sq, rs, hs], kf_ref[sq, rs, hs], v_ref[sq, rs, hs]
            b_blocks = cumsum_blocks(g_ref[sq, rs, hs])
            b = cat(b_blocks)
            a_prev = [None]
            for n in range(1, len(b_blocks)):
                e = b_blocks[n - 1][blk - 1:blk, :]
                a_prev.append(_dot_nt((q[n * blk:(n + 1) * blk, :] * jnp.exp(b_blocks[n] - e)).astype(BF16),
                                      (kf[:n * blk, :] * jnp.exp(e - b[:n * blk, :])).astype(BF16)))
            streams.append(((sq, rs, hs), sq * nh + h, hs, q, kf, v, b_blocks, b, a_prev))

        diag = []
        for rs, h, hs, q, kf, v, b_blocks, b, a_prev in streams:
            o_blocks = []
            for n, bn in enumerate(b_blocks):
                r0 = n * blk
                qn = q[r0:r0 + blk, :]
                on = jnp.zeros((blk, kd), F32)
                for s in range(blk):
                    xs = qn * jnp.exp(jnp.minimum(bn - bn[s:s + 1, :], 0.0)) * kf[r0 + s:r0 + s + 1, :]
                    a = jnp.sum(xs, axis=-1, keepdims=True)
                    on = on + jnp.where(rowi >= s, a, 0.0) * v[r0 + s:r0 + s + 1, :]
                o_blocks.append(on)
            diag.append(o_blocks)

        for (rs, h, hs, q, kf, v, b_blocks, b, a_prev), o_blocks in zip(streams, diag):
            st = st_s[h]
            for n in range(1, len(b_blocks)):
                o_blocks[n] = o_blocks[n] + _dot(a_prev[n].astype(BF16), v[:n * blk, :].astype(BF16))
            o = cat(o_blocks) + _dot_nt((q * jnp.exp(b)).astype(BF16), st.astype(BF16))
            bl = b_blocks[-1][blk - 1:blk, :]
            kd_ = kf * jnp.exp(bl - b)
            st_s[h] = st * jnp.exp(bl) + _dot_tn(v.astype(BF16), kd_.astype(BF16))
            y = _rms(o, nw_ref[:, hs]) * gc_ref[rs]
            y_ref[rs] = y.astype(y_ref.dtype)
        return carry

    if nsub == 1:
        body(0, 0)
    else:
        lax.fori_loop(0, nsub // group, body, 0)

    @pl.when(ti == pl.num_programs(1) - 1)
    def _():
        for sq in range(nb):
            for h in range(nh):
                sn_ref[sq, h] = st_s[sq * nh + h].T


def _hgrn(qs, gl, kf, iv, gcs, nw, nw_layer, s0_all, layer):
    b, t, w = qs.shape
    _, _, nh, kd, vd = s0_all.shape
    c = HGRN_SUB if t % HGRN_SUB == 0 else t
    tt = HGRN_TILE if t % HGRN_TILE == 0 else t
    kern = functools.partial(_hgrn_kernel, c=c, nsub=tt // c, kd=kd)
    nb = HGRN_SEQS if (tt == t == c and b % HGRN_SEQS == 0) else 1
    seq = pl.BlockSpec((nb, tt, w), lambda i, j: (i, j, 0))
    state = pl.BlockSpec((nb, nh, kd, vd), lambda i, j: (i, 0, 0, 0))
    state_in = pl.BlockSpec((None, nb, nh, kd, vd), lambda i, j: (layer, i, 0, 0, 0))
    return pl.pallas_call(
        kern, grid=(b // nb, t // tt),
        in_specs=[seq, seq, seq, seq, seq, _layer_spec(nw, nw_layer), state_in],
        out_specs=(seq, state),
        out_shape=(jax.ShapeDtypeStruct((b, t, nh * vd), BF16), jax.ShapeDtypeStruct((b, nh, kd, vd), F32)),
        scratch_shapes=[pltpu.VMEM((nb * nh, vd, kd), F32)],
        compiler_params=_params("parallel", "arbitrary"), name="hgrn",
    )(qs, gl, kf, iv, gcs, nw, s0_all)


def _merge_kernel(x_ref, ya_ref, ub_ref, vb_ref, yc_ref, gt_ref, gm_ref, gb_ref,
                  woa_ref, wob_ref, woc_ref, wo_ref, o_ref, *, n_groups):
    tm = x_ref.shape[0]
    d = x_ref.shape[1]
    cn = gm_ref.shape[1]
    gw = ub_ref.shape[1] // n_groups
    yb_rows = []
    for ci in range(tm // cn):
        rs = slice(ci * cn, (ci + 1) * cn)
        vb = vb_ref[rs, :].astype(BF16)
        gb = gb_ref[...]
        parts = []
        for g in range(n_groups):
            parts.append(_dot(gm_ref[g], vb[:, g * gw:(g + 1) * gw]) + gb[:, g:g + 1])
        yb_rows.append(ub_ref[rs, :] * jnp.concatenate(parts, axis=1))
    yb = jnp.concatenate(yb_rows, axis=0) if len(yb_rows) > 1 else yb_rows[0]
    gt = gt_ref[...]
    merged = (gt[:, :d] * _dot(ya_ref[...].astype(BF16), woa_ref[...])
              + gt[:, d:2 * d] * _dot(yb.astype(BF16), wob_ref[...])
              + gt[:, 2 * d:] * _dot(yc_ref[...].astype(BF16), woc_ref[...]))
    o_ref[...] = x_ref[...] + _dot(merged.astype(BF16), wo_ref[...])


def _merge(x, ya, ub, vb, yc, gt, layer, gm, gb, woa, wob, woc, wo):
    n, d = x.shape
    tm = min(MERGE_TILE, n)

    def rows(w):
        return pl.BlockSpec((tm, w), lambda i: (i, 0))

    kern = functools.partial(_merge_kernel, n_groups=gm.shape[1])
    return pl.pallas_call(
        kern, grid=(n // tm,),
        in_specs=[rows(d), rows(ya.shape[1]), rows(ub.shape[1]), rows(vb.shape[1]), rows(yc.shape[1]),
                  rows(gt.shape[1])] + [_layer_spec(p, layer) for p in (gm, gb, woa, wob, woc, wo)],
        out_specs=rows(d), out_shape=jax.ShapeDtypeStruct((n, d), F32),
        compiler_params=_params("parallel"), name="merge",
    )(x, ya, ub, vb, yc, gt, gm, gb, woa, wob, woc, wo)


def _ffn_kernel(x_ref, n2_ref, wup_ref, wcv_ref, bcv_ref, wdn_ref, buf_ref, nf_ref,
                y_ref, nb_ref, carry_s, ext_s, h_s, act_s, *, shift, base, cw, final_norm):
    ti = pl.program_id(1)
    tm = x_ref.shape[1]
    d_ff = wdn_ref.shape[0]
    hist = 2 * shift

    @pl.when(ti == 0)
    def _():
        carry_s[...] = buf_ref[0]

    n_chunks = d_ff // cw
    n_slots = ext_s.shape[0]

    def col0(c, half):
        return half * d_ff + c * cw

    def stage(c):
        for half in range(2):
            cols = slice(col0(c, half), col0(c, half) + cw)
            ext = ext_s.at[(2 * c + half) % n_slots]
            ext[base - hist:base, :] = carry_s[:, cols]
            ext[base:base + tm, :] = _dot(h_s[...], wup_ref[:, cols])

    def conv(c, half):
        cols = slice(col0(c, half), col0(c, half) + cw)
        ext = ext_s.at[(2 * c + half) % n_slots]
        wcv = wcv_ref[:, cols]
        out = bcv_ref[:, cols] + (wcv[0:1, :] * ext[base - hist:base - hist + tm, :]
                                  + wcv[1:2, :] * ext[base - shift:base - shift + tm, :]
                                  + wcv[2:3, :] * ext[base:base + tm, :])
        carry_s[:, cols] = ext[base + tm - hist:base + tm, :]
        return out

    x = x_ref[0]
    h_s[...] = _rms(x, n2_ref[...]).astype(BF16)
    y = x
    stage(0)
    for c in range(n_chunks):
        if c + 1 < n_chunks:
            stage(c + 1)
        act_s[:, c * cw:(c + 1) * cw] = (_silu(conv(c, 0)) * conv(c, 1)).astype(BF16)
        if (c + 1) % FFN_DOWN_CHUNKS == 0 or c + 1 == n_chunks:
            r0 = (c // FFN_DOWN_CHUNKS) * FFN_DOWN_CHUNKS * cw
            y = y + _dot(act_s[:, r0:(c + 1) * cw], wdn_ref[r0:(c + 1) * cw, :])
    if final_norm:
        y = _rms(y, nf_ref[...])
    y_ref[0] = y
    nb_ref[0] = carry_s[...]


def _ffn(x, layer, n2, wup, wcv, bcv, wdn, buf, nf, shift, final_norm):
    nseq, t, d = x.shape
    tm = min(FFN_TILE, t)
    hist = 2 * shift
    base = -(-hist // SUBLANES) * SUBLANES
    kern = functools.partial(_ffn_kernel, shift=shift, base=base, cw=FFN_COLS, final_norm=final_norm)
    state = pl.BlockSpec((1, hist, buf.shape[2]), lambda i, j: (i, 0, 0))
    return pl.pallas_call(
        kern, grid=(nseq, t // tm),
        in_specs=[pl.BlockSpec((1, tm, d), lambda i, j: (i, j, 0)), _layer_spec(n2, layer),
                  _layer_spec(wup, layer), _layer_spec(wcv, layer), _layer_spec(bcv, layer),
                  _layer_spec(wdn, layer), state, _const_spec(nf.shape)],
        out_specs=(pl.BlockSpec((1, tm, d), lambda i, j: (i, j, 0)), state),
        out_shape=(jax.ShapeDtypeStruct(x.shape, F32), jax.ShapeDtypeStruct(buf.shape, F32)),
        scratch_shapes=[pltpu.VMEM((hist, buf.shape[2]), F32),
                        pltpu.VMEM((FFN_EXT_SLOTS, base + tm, FFN_COLS), F32),
                        pltpu.VMEM((tm, d), BF16), pltpu.VMEM((tm, wdn.shape[1]), BF16)],
        compiler_params=_params("parallel", "arbitrary"), name="conv_ffn",
    )(x, n2, wup, wcv, bcv, wdn, buf, nf)


def kernel(x_prompt, x_sample, cache_k, cache_v, cache_logf, state_hgrn, state_conv, page_table,
           norm1, w_in, b_f, gmlp_norm, w_s, b_s, hgrn_norm, hgrn_lb, w_oa, w_ob, w_oc, w_o,
           norm2, w_up, w_conv, b_conv, w_down, norm_f):
    depth = w_in.shape[0]
    bp, tp, d = x_prompt.shape
    bs, ts, _ = x_sample.shape
    _, n_phys, page, nh_a, hd_a = cache_k.shape
    w_a = nh_a * hd_a
    n_grp, chunk_b, _ = w_s.shape[1:]
    w_b = gmlp_norm.shape[1]
    _, _, nh_c, k_c, v_c = state_hgrn.shape
    w_c = nh_c * v_c
    d_ff = w_down.shape[1]

    sm = jax.nn.softmax(hgrn_lb.astype(F32), axis=0)
    lower = jnp.cumsum(sm, axis=0) - sm[:1]

    cache_kt = jnp.transpose(cache_k, (0, 1, 3, 4, 2))
    cache_vt = jnp.transpose(cache_v, (0, 1, 3, 4, 2))
    cache_ft = jnp.swapaxes(cache_logf, 2, 3)

    w_t = jnp.swapaxes(w_in, 1, 2).astype(BF16)
    o_f = 3 * w_a
    inproj_w = (norm1.reshape(depth, 1, d), w_t[:, :o_f],
                jnp.pad(w_t[:, o_f:o_f + nh_a], ((0, 0), (0, LANES - nh_a), (0, 0))), w_t[:, o_f + nh_a:],
                jnp.pad(b_f, ((0, 0), (0, LANES - nh_a))).reshape(depth, 1, LANES), b_f.reshape(depth, nh_a, 1),
                gmlp_norm.reshape(depth, 1, w_b), lower.reshape(depth, 1, w_c))
    qscale = hd_a ** -0.5 * LOG2E
    nw = hgrn_norm.reshape(depth, 1, w_c)
    tril_w = jnp.where(jnp.tril(jnp.ones((chunk_b, chunk_b), bool)), w_s, 0)
    merge_w = (w_oa.astype(BF16), w_ob.astype(BF16), w_oc.astype(BF16), w_o.astype(BF16))
    gate_p = (tril_w.astype(BF16), jnp.swapaxes(b_s, 1, 2))
    same_seq = np.kron(np.eye(bs, dtype=np.float32), np.ones((ts, ts), np.float32))
    rep = np.tile(np.eye(ts, dtype=np.float32), (bs, 1))
    gate_s = ((jnp.einsum("rt,lgts,cs->lgrc", rep, tril_w[:, :, :ts, :ts], rep) * same_seq).astype(BF16),
              jnp.tile(jnp.swapaxes(b_s[:, :, :ts], 1, 2), (1, bs, 1)))
    ffn_w = (norm2.reshape(depth, 1, d), w_up.astype(BF16), w_conv, b_conv.reshape(depth, 1, 2 * d_ff),
             w_down.astype(BF16))
    nf = norm_f.reshape(1, d)
    s0 = jnp.zeros((1, bp, nh_c, k_c, v_c), F32)
    buf0 = jnp.zeros((bp, 2, 2 * d_ff), F32)

    xp = x_prompt.reshape(bp * tp, d)
    xs = x_sample.reshape(bs * ts, d)
    st_p, st_s = [], []
    kv_bufs = None
    for l in range(depth):
        last = l == depth - 1

        (qa, ka, kt_all, vt_all, vt, lft_all, ub, vb, qs, gl, kf, iv, gcs, gt) = _inproj(
            xp, l, qscale, *inproj_w, stacked=(depth, bp, kv_bufs))
        kv_bufs = (kt_all, vt_all, lft_all)
        ya = _fox_prompt(qa.reshape(bp, tp, 2 * w_a), ka.reshape(bp, tp, 2 * w_a), vt, nh_a)
        yc, s_new = _hgrn(*(a.reshape(bp, tp, w_c) for a in (qs, gl, kf, iv, gcs)), nw, l, s0, 0)
        x1 = _merge(xp, ya.reshape(bp * tp, w_a), ub, vb, yc.reshape(bp * tp, w_c), gt, l, *gate_p, *merge_w)
        x2, conv_p = _ffn(x1.reshape(bp, tp, d), l, *ffn_w, buf0, nf, 1, last)
        xp = x2.reshape(bp * tp, d)
        st_p.append((s_new, conv_p))

        (q, k, v, lf, ub, vb, qs, gl, kf, iv, gcs, gt) = _inproj(xs, l, qscale, *inproj_w)
        k4 = k.reshape(bs, ts, nh_a, hd_a)
        v4 = v.reshape(bs, ts, nh_a, hd_a)
        lf3 = lf[:, :nh_a].reshape(bs, ts, nh_a)
        ya = _fox_sample(q.astype(F32), k, v, jnp.swapaxes(lf3, 1, 2),
                         cache_kt, cache_vt, cache_ft, page_table, l)
        yc, s_new = _hgrn(*(a.reshape(bs, ts, w_c) for a in (qs, gl, kf, iv, gcs)), nw, l, state_hgrn, l)
        x1 = _merge(xs, ya, ub, vb, yc.reshape(bs * ts, w_c), gt, l, *gate_s, *merge_w)
        x1t = jnp.swapaxes(x1.reshape(bs, ts, d), 0, 1).reshape(1, ts * bs, d)
        buf_t = jnp.swapaxes(state_conv[l], 0, 1).reshape(1, 2 * bs, 2 * d_ff)
        x2t, nb = _ffn(x1t, l, *ffn_w, buf_t, nf, bs, last)
        xs = jnp.swapaxes(x2t.reshape(ts, bs, d), 0, 1).reshape(bs * ts, d)
        conv_s = nb.reshape(2, bs, 2 * d_ff)
        st_s.append((k4, v4, lf3, vb.reshape(bs, ts, w_b), s_new, jnp.swapaxes(conv_s, 0, 1)))

    def stk(sts, i):
        return jnp.stack([st[i] for st in sts], axis=0)

    kt_all, vt_all, lft_all = kv_bufs
    k_prompt = jnp.transpose(kt_all.reshape(depth, bp, nh_a, hd_a, tp), (0, 1, 4, 2, 3))
    v_prompt = jnp.transpose(vt_all.reshape(depth, bp, nh_a, hd_a, tp), (0, 1, 4, 2, 3))
    return (xp.reshape(bp, tp, d), xs.reshape(bs, ts, d),
            k_prompt, v_prompt, jnp.swapaxes(lft_all, 2, 3), stk(st_p, 0), stk(st_p, 1),
            stk(st_s, 0), stk(st_s, 1), stk(st_s, 2), stk(st_s, 3), stk(st_s, 4), stk(st_s, 5))
```
